```python
import math
import jax, jax.numpy as jnp
from jax import lax
import numpy as np

D_MODEL = 1024
BATCH = 16
SEQ = 2048
DEPTH = 2

CHUNK = 64
D_PLE = 256
EPS = 1e-6

A_HEADS = 8
A_HEAD_DIM = 64
A_WIDTH = A_HEADS * A_HEAD_DIM
A_LEFT_CHUNKS = 8
A_BAND = (A_LEFT_CHUNKS + 1) * CHUNK
REL_MAX = 128

B_HEADS = 4
B_NOPE = 128
B_ROPE = 64
B_VDIM = 128
B_WIDTH = B_HEADS * B_VDIM
Q_LORA = 256
KV_LORA = 128
ROPE_THETA = 10000.0
Q_BLOCK = 128
MAX_POS_OFFSET = 4096

D_MIX = A_WIDTH + B_WIDTH
IN_SIZES = (A_WIDTH, A_WIDTH, A_WIDTH, A_WIDTH, Q_LORA, KV_LORA, B_ROPE, B_WIDTH)
D_IN = 4 * A_WIDTH + Q_LORA + KV_LORA + B_ROPE + B_WIDTH

kernel_name = "hybrid_chunked_relpos_mla_trunk"


def rmsnorm(x, g):
    xf = x.astype(jnp.float32)
    y = xf * lax.rsqrt(jnp.mean(xf * xf, axis=-1, keepdims=True) + EPS)
    return (y * g.astype(jnp.float32)).astype(x.dtype)


def split_cols(z, sizes):
    idx = np.cumsum(np.array(sizes[:-1])).tolist()
    return jnp.split(z, idx, axis=-1)


def rope_tables(positions, dim):
    inv_freq = ROPE_THETA ** (-jnp.arange(0, dim, 2, dtype=jnp.float32) / dim)
    ang = positions.astype(jnp.float32)[..., None] * inv_freq
    return jnp.cos(ang), jnp.sin(ang)


def apply_rope(x, cos, sin):
    half = x.shape[-1] // 2
    x1 = x[..., :half].astype(jnp.float32)
    x2 = x[..., half:].astype(jnp.float32)
    out = jnp.concatenate([x1 * cos - x2 * sin, x2 * cos + x1 * sin], axis=-1)
    return out.astype(x.dtype)


def chunked_relpos_attention(q, k, v, rel_bias):
    B, S = q.shape[0], q.shape[1]
    nc = S // CHUNK
    qc = q.reshape(B, nc, CHUNK, A_HEADS, A_HEAD_DIM)
    pad = jnp.zeros((B, A_LEFT_CHUNKS * CHUNK, A_HEADS, A_HEAD_DIM), k.dtype)
    kc = jnp.concatenate([pad, k], axis=1).reshape(B, nc + A_LEFT_CHUNKS, CHUNK, A_HEADS, A_HEAD_DIM)
    vc = jnp.concatenate([pad, v], axis=1).reshape(B, nc + A_LEFT_CHUNKS, CHUNK, A_HEADS, A_HEAD_DIM)
    band = jnp.arange(nc)[:, None] + jnp.arange(A_LEFT_CHUNKS + 1)[None, :]
    kb = kc[:, band].reshape(B, nc, A_BAND, A_HEADS, A_HEAD_DIM)
    vb = vc[:, band].reshape(B, nc, A_BAND, A_HEADS, A_HEAD_DIM)
    s = jnp.einsum('bnqhd,bnkhd->bhnqk', qc, kb).astype(jnp.float32) * (A_HEAD_DIM ** -0.5)
    q_off = jnp.arange(CHUNK) + A_LEFT_CHUNKS * CHUNK
    k_off = jnp.arange(A_BAND)
    rel = jnp.clip(q_off[:, None] - k_off[None, :], -REL_MAX, REL_MAX) + REL_MAX
    bias = rel_bias.astype(jnp.float32)[:, rel]
    valid = jnp.repeat(band >= A_LEFT_CHUNKS, CHUNK, axis=1)
    s = jnp.where(valid[None, None, :, None, :], s + bias[:, None], -jnp.inf)
    pr = jax.nn.softmax(s, axis=-1).astype(v.dtype)
    o = jnp.einsum('bhnqk,bnkhd->bnqhd', pr, vb)
    return o.reshape(B, S, A_WIDTH)


def mla_attention(c_q, c_kv, k_rope_raw, g_q, w_uq, g_kv, w_ukv, cos, sin):
    B, S = c_q.shape[0], c_q.shape[1]
    q = (rmsnorm(c_q, g_q) @ w_uq).reshape(B, S, B_HEADS, B_NOPE + B_ROPE)
    q_nope, q_rope = q[..., :B_NOPE], q[..., B_NOPE:]
    q_rope = apply_rope(q_rope, cos[:, :, None, :], sin[:, :, None, :])
    kv = (rmsnorm(c_kv, g_kv) @ w_ukv).reshape(B, S, B_HEADS, B_NOPE + B_VDIM)
    k_nope, v = kv[..., :B_NOPE], kv[..., B_NOPE:]
    k_rope = apply_rope(k_rope_raw, cos, sin)
    scale = (B_NOPE + B_ROPE) ** -0.5
    nqb = S // Q_BLOCK
    qn = q_nope.reshape(B, nqb, Q_BLOCK, B_HEADS, B_NOPE).transpose(1, 0, 2, 3, 4)
    qr = q_rope.reshape(B, nqb, Q_BLOCK, B_HEADS, B_ROPE).transpose(1, 0, 2, 3, 4)
    key_chunk = jnp.arange(S) // CHUNK

    def block(args):
        qn_b, qr_b, qb = args
        s = (jnp.einsum('bqhd,bkhd->bhqk', qn_b, k_nope)
             + jnp.einsum('bqhd,bkd->bhqk', qr_b, k_rope)).astype(jnp.float32) * scale
        q_chunk = (qb * Q_BLOCK + jnp.arange(Q_BLOCK)) // CHUNK
        mask = key_chunk[None, :] <= q_chunk[:, None]
        s = jnp.where(mask[None, None], s, -jnp.inf)
        pr = jax.nn.softmax(s, axis=-1).astype(v.dtype)
        return jnp.einsum('bhqk,bkhd->bqhd', pr, v)

    o = lax.map(block, (qn, qr, jnp.arange(nqb)))
    return o.transpose(1, 0, 2, 3, 4).reshape(B, S, B_WIDTH)


def setup_inputs(seed: int = 0) -> dict:
    key = jax.random.key(seed)
    ks = jax.random.split(key, 20)
    f32 = jnp.float32
    nrm = lambda k, shape, fan_in: jax.random.normal(k, shape, f32) * (fan_in ** -0.5)
    gain = lambda k, shape: 1.0 + 0.05 * jax.random.normal(k, shape, f32)
    x = jax.random.normal(ks[0], (BATCH, SEQ, D_MODEL), f32)
    p = jax.random.normal(ks[1], (DEPTH, BATCH, SEQ, D_PLE), f32)
    offs = jax.random.randint(ks[2], (BATCH, 1), 0, MAX_POS_OFFSET, dtype=jnp.int32)
    positions = (jnp.arange(SEQ, dtype=jnp.int32)[None, :] + offs).astype(jnp.int32)
    return {
        "x": x,
        "p": p,
        "positions": positions,
        "norm_mix": gain(ks[3], (DEPTH, D_MODEL)),
        "w_in": nrm(ks[4], (DEPTH, D_MODEL, D_IN), D_MODEL),
        "rel_bias": 0.5 * jax.random.normal(ks[5], (DEPTH, A_HEADS, 2 * REL_MAX + 1), f32),
        "g_q": gain(ks[6], (DEPTH, Q_LORA)),
        "w_uq": nrm(ks[7], (DEPTH, Q_LORA, B_HEADS * (B_NOPE + B_ROPE)), Q_LORA),
        "g_kv": gain(ks[8], (DEPTH, KV_LORA)),
        "w_ukv": nrm(ks[9], (DEPTH, KV_LORA, B_HEADS * (B_NOPE + B_VDIM)), KV_LORA),
        "w_out": nrm(ks[10], (DEPTH, D_MIX, D_MODEL), D_MIX),
        "norm_ple": gain(ks[11], (DEPTH, D_MODEL)),
        "w_pe": nrm(ks[12], (DEPTH, D_PLE, D_MODEL), D_PLE),
        "w_pg": nrm(ks[13], (DEPTH, D_MODEL, D_MODEL), D_MODEL),
        "b_pg": 0.02 * jax.random.normal(ks[14], (DEPTH, D_MODEL), f32),
        "norm_final": gain(ks[15], (D_MODEL,)),
    }


def reference(x, p, positions, norm_mix, w_in, rel_bias, g_q, w_uq, g_kv, w_ukv, w_out,
              norm_ple, w_pe, w_pg, b_pg, norm_final):
    B, S = x.shape[0], x.shape[1]
    cos, sin = rope_tables(positions, B_ROPE)
    h = x
    for i in range(DEPTH):
        xn = rmsnorm(h, norm_mix[i])
        z = xn @ w_in[i]
        q_a, k_a, v_a, gate_a, c_q, c_kv, k_rope_raw, gate_b = split_cols(z, IN_SIZES)
        shp = (B, S, A_HEADS, A_HEAD_DIM)
        o_a = chunked_relpos_attention(q_a.reshape(shp), k_a.reshape(shp), v_a.reshape(shp), rel_bias[i])
        o_b = mla_attention(c_q, c_kv, k_rope_raw, g_q[i], w_uq[i], g_kv[i], w_ukv[i], cos, sin)
        mixed = jnp.concatenate([o_a * jax.nn.silu(gate_a), o_b * jax.nn.silu(gate_b)], axis=-1)
        h = h + mixed @ w_out[i]
        gate = jax.nn.sigmoid(rmsnorm(h, norm_ple[i]) @ w_pg[i] + b_pg[i])
        h = h + (p[i] @ w_pe[i]) * gate
    return rmsnorm(h, norm_final)
```

```python
import functools
import math

import jax
import jax.numpy as jnp
from jax import lax
from jax.experimental import pallas as pl
from jax.experimental.pallas import tpu as pltpu

D_MODEL = 1024
CHUNK = 64
D_PLE = 256
EPS = 1e-6
A_HEADS = 8
A_HEAD_DIM = 64
A_WIDTH = A_HEADS * A_HEAD_DIM
A_LEFT_CHUNKS = 8
REL_MAX = 128
B_HEADS = 4
B_NOPE = 128
B_ROPE = 64
B_VDIM = 128
B_WIDTH = B_HEADS * B_VDIM
Q_LORA = 256
KV_LORA = 128
ROPE_THETA = 10000.0

LANES = 128
TOKEN_TILE = 512
Q_BLOCK = 256
A_LEFT = A_LEFT_CHUNKS * CHUNK
A_WINDOW = A_LEFT + Q_BLOCK
B_QW = B_NOPE + LANES
NEG_BIG = -1e30
VMEM_LIMIT = 56 * 1024 * 1024

BF16 = jnp.bfloat16
F32 = jnp.float32


def _dot(a, b):
  return jnp.dot(a, b, preferred_element_type=F32)


def _dot_nt(a, b):
  return lax.dot_general(a, b, (((1,), (1,)), ((), ())), preferred_element_type=F32)


def _rms(x, g):
  y = x * lax.rsqrt(jnp.mean(x * x, axis=-1, keepdims=True) + EPS)
  return y * g


def _silu(g):
  return g * (1.0 / (1.0 + jnp.exp(-g)))


def _rope_kernel(pos_ref, invf_ref, cmask_ref, ssign_ref, c_ref, s_ref):
  ang = pos_ref[...].astype(F32) * invf_ref[...]
  c_ref[...] = jnp.cos(ang) * cmask_ref[...]
  s_ref[...] = jnp.sin(ang) * ssign_ref[...]


def _rope_tables(positions):
  n = positions.size
  half = B_ROPE // 2
  inv_freq = ROPE_THETA ** (-jnp.arange(0, B_ROPE, 2, dtype=F32) / B_ROPE)
  zeros = jnp.zeros((LANES - 2 * half,), F32)
  invf = jnp.concatenate([inv_freq, inv_freq, zeros])[None, :]
  cmask = jnp.concatenate([jnp.ones((2 * half,), F32), zeros])[None, :]
  ssign = jnp.concatenate([-jnp.ones((half,), F32), jnp.ones((half,), F32), zeros])[None, :]
  pos = positions.reshape(n, 1)
  tile = 2048
  row = pl.BlockSpec((1, LANES), lambda i: (0, 0))
  return pl.pallas_call(
      _rope_kernel,
      grid=(n // tile,),
      in_specs=[pl.BlockSpec((tile, 1), lambda i: (i, 0)), row, row, row],
      out_specs=[pl.BlockSpec((tile, LANES), lambda i: (i, 0))] * 2,
      out_shape=[jax.ShapeDtypeStruct((n, LANES), F32)] * 2,
      compiler_params=pltpu.CompilerParams(dimension_semantics=("parallel",)),
      name="rope_tables",
  )(pos, invf, cmask, ssign)


G_LEN = 1024


def _bias_table_kernel(g_ref, tab_ref):
  g = jnp.broadcast_to(g_ref[0], (Q_BLOCK, G_LEN))
  t = pltpu.roll(g, G_LEN - Q_BLOCK, 1, stride=1, stride_axis=0)[:, :A_WINDOW]
  qi = lax.broadcasted_iota(jnp.int32, (Q_BLOCK, A_WINDOW), 0)
  kk = lax.broadcasted_iota(jnp.int32, (Q_BLOCK, A_WINDOW), 1)
  qc = (qi + A_LEFT) // CHUNK
  kc = kk // CHUNK
  valid = (kc <= qc) & (kc >= qc - A_LEFT_CHUNKS)
  tab_ref[0] = jnp.where(valid, t, NEG_BIG)


def _bias_tables(rel_bias):
  flat = A_LEFT + Q_BLOCK - REL_MAX
  rev = rel_bias[:, ::-1]
  g = jnp.concatenate([
      jnp.broadcast_to(rel_bias[:, -1:], (A_HEADS, flat)),
      rev[:, : 2 * REL_MAX],
      jnp.broadcast_to(rel_bias[:, :1], (A_HEADS, G_LEN - flat - 2 * REL_MAX)),
  ], axis=1)[:, None, :]
  return pl.pallas_call(
      _bias_table_kernel,
      grid=(A_HEADS,),
      in_specs=[pl.BlockSpec((1, 1, G_LEN), lambda h: (h, 0, 0))],
      out_specs=pl.BlockSpec((1, Q_BLOCK, A_WINDOW), lambda h: (h, 0, 0)),
      out_shape=jax.ShapeDtypeStruct((A_HEADS, Q_BLOCK, A_WINDOW), F32),
      compiler_params=pltpu.CompilerParams(dimension_semantics=("parallel",)),
      name="bias_table",
  )(g)


def _in_proj_kernel(h_ref, gmix_ref, wq_ref, wk_ref, wv_ref, wga_ref, wgb_ref, wmla_ref,
                    gq_ref, gkv_ref, wuq_ref, wukv_ref, cos_ref, sin_ref,
                    q_out, k_out, v_out, ga_out, gb_out, qb_out, kb_out, vb_out):
  xn = _rms(h_ref[...], gmix_ref[...]).astype(BF16)
  q_out[...] = (_dot(xn, wq_ref[...]) * (A_HEAD_DIM ** -0.5)).astype(BF16)
  k_out[...] = _dot(xn, wk_ref[...]).astype(BF16)
  v_out[...] = _dot(xn, wv_ref[...]).astype(BF16)
  ga_out[...] = _dot(xn, wga_ref[...])
  gb_out[...] = _dot(xn, wgb_ref[...])

  zm = _dot(xn, wmla_ref[...])
  cos = cos_ref[...]
  sin = sin_ref[...]
  cqn = _rms(zm[:, :Q_LORA], gq_ref[...]).astype(BF16)
  ckvn = _rms(zm[:, Q_LORA:Q_LORA + KV_LORA], gkv_ref[...]).astype(BF16)
  kr0 = Q_LORA + KV_LORA
  k_rope = (zm[:, kr0:kr0 + LANES] * cos + zm[:, kr0 + LANES:kr0 + 2 * LANES] * sin).astype(BF16)

  qall = _dot(cqn, wuq_ref[...])
  kv = _dot(ckvn, wukv_ref[...])
  for h in range(B_HEADS):
    qb = 3 * LANES * h
    q_rope = qall[:, qb + LANES:qb + 2 * LANES] * cos + qall[:, qb + 2 * LANES:qb + 3 * LANES] * sin
    qb_out[:, B_QW * h:B_QW * h + B_NOPE] = qall[:, qb:qb + LANES].astype(BF16)
    qb_out[:, B_QW * h + B_NOPE:B_QW * (h + 1)] = q_rope.astype(BF16)
    kb_out[:, B_QW * h:B_QW * h + B_NOPE] = kv[:, 2 * LANES * h:2 * LANES * h + B_NOPE].astype(BF16)
    kb_out[:, B_QW * h + B_NOPE:B_QW * (h + 1)] = k_rope
    vb_out[:, B_VDIM * h:B_VDIM * (h + 1)] = kv[:, 2 * LANES * h + B_NOPE:2 * LANES * (h + 1)].astype(BF16)


def _in_proj(h, w, cos, sin):
  n = h.shape[0]
  t = TOKEN_TILE
  tok = lambda width: pl.BlockSpec((t, width), lambda i: (i, 0))
  full = lambda a: pl.BlockSpec(a.shape, lambda i: (0, 0))
  weights = [w["gmix"], w["wq"], w["wk"], w["wv"], w["wga"], w["wgb"], w["wmla"],
             w["gq"], w["gkv"], w["wuq"], w["wukv"]]
  out_widths = [(A_WIDTH, BF16), (A_WIDTH, BF16), (A_WIDTH, BF16), (A_WIDTH, F32), (B_WIDTH, F32),
                (B_HEADS * B_QW, BF16), (B_HEADS * B_QW, BF16), (B_WIDTH, BF16)]
  return pl.pallas_call(
      _in_proj_kernel,
      grid=(n // t,),
      in_specs=[tok(D_MODEL)] + [full(a) for a in weights] + [tok(LANES), tok(LANES)],
      out_specs=[tok(wd) for wd, _ in out_widths],
      out_shape=[jax.ShapeDtypeStruct((n, wd), dt) for wd, dt in out_widths],
      compiler_params=pltpu.CompilerParams(dimension_semantics=("parallel",),
                                           vmem_limit_bytes=VMEM_LIMIT),
      name="in_proj",
  )(h, *weights, cos, sin)


def _attn_a_block(q_ref, k_ref, v_ref, gate_ref, tab_ref, o_ref, kstart, nkeys):
  col0 = A_WINDOW - nkeys
  lane = lax.broadcasted_iota(jnp.int32, (1, LANES), 1)
  lo_row = (lane < A_HEAD_DIM).astype(BF16)
  hi_row = (lane >= A_HEAD_DIM).astype(BF16)
  lo_mask = lax.broadcasted_iota(jnp.int32, (Q_BLOCK, LANES), 1) < A_HEAD_DIM
  for p in range(A_HEADS // 2):
    ls = slice(LANES * p, LANES * (p + 1))
    qp = q_ref[0, :, ls]
    kp = k_ref[0, pl.ds(kstart, nkeys), ls]
    vp = v_ref[0, pl.ds(kstart, nkeys), ls]
    outs = []
    for hh, row in enumerate((lo_row, hi_row)):
      s = _dot_nt(qp * row, kp) + tab_ref[2 * p + hh, :, col0:]
      m = jnp.max(s, axis=-1, keepdims=True)
      e = jnp.exp(s - m)
      l = jnp.sum(e, axis=-1, keepdims=True)
      outs.append(_dot(e.astype(BF16), vp) / l)
    o = jnp.where(lo_mask, outs[0], outs[1])
    o_ref[0, :, ls] = (o * _silu(gate_ref[0, :, ls])).astype(BF16)


def _attn_a_kernel(q_ref, k_ref, v_ref, gate_ref, tab_ref, o_ref):
  qb = pl.program_id(1)
  n_early = A_LEFT // Q_BLOCK
  for e in range(n_early):
    @pl.when(qb == e)
    def _():
      _attn_a_block(q_ref, k_ref, v_ref, gate_ref, tab_ref, o_ref, 0, Q_BLOCK * (e + 1))

  @pl.when(qb >= n_early)
  def _():
    kstart = pl.multiple_of(qb * Q_BLOCK - A_LEFT, Q_BLOCK)
    _attn_a_block(q_ref, k_ref, v_ref, gate_ref, tab_ref, o_ref, kstart, A_WINDOW)


def _attn_a(q, k, v, gate, tab):
  b, s, _ = q.shape
  blk = pl.BlockSpec((1, Q_BLOCK, A_WIDTH), lambda i, j: (i, j, 0))
  seq = pl.BlockSpec((1, s, A_WIDTH), lambda i, j: (i, 0, 0))
  return pl.pallas_call(
      _attn_a_kernel,
      grid=(b, s // Q_BLOCK),
      in_specs=[blk, seq, seq, blk,
                pl.BlockSpec((A_HEADS, Q_BLOCK, A_WINDOW), lambda i, j: (0, 0, 0))],
      out_specs=blk,
      out_shape=jax.ShapeDtypeStruct((b, s, A_WIDTH), BF16),
      compiler_params=pltpu.CompilerParams(dimension_semantics=("parallel", "parallel"),
                                           vmem_limit_bytes=VMEM_LIMIT),
      name="attn_a",
  )(q, k, v, gate, tab)


def _attn_b_kernel(q_ref, k_ref, v_ref, gate_ref, o_ref, s_scr):
  qb = pl.program_id(1)
  scale = (B_NOPE + B_ROPE) ** -0.5
  half = Q_BLOCK // 2
  qi = lax.broadcasted_iota(jnp.int32, (Q_BLOCK, Q_BLOCK), 0)
  kj = lax.broadcasted_iota(jnp.int32, (Q_BLOCK, Q_BLOCK), 1)
  diag_ok = (kj // CHUNK) <= (qi // CHUNK)

  def fold(x):
    return jnp.maximum(x[:, :half], x[:, half:])

  for h in range(B_HEADS):
    qh = q_ref[0, :, B_QW * h:B_QW * (h + 1)]

    def scores(kb):
      k0 = pl.multiple_of(kb * Q_BLOCK, Q_BLOCK)
      return _dot_nt(qh, k_ref[0, pl.ds(k0, Q_BLOCK), B_QW * h:B_QW * (h + 1)])

    def pass1(kb, mx):
      s = scores(kb)
      s_scr[kb] = s
      return jnp.maximum(mx, fold(s))

    mx = lax.fori_loop(0, qb, pass1, jnp.full((Q_BLOCK, half), -jnp.inf, F32))
    s = jnp.where(diag_ok, scores(qb), -jnp.inf)
    s_scr[qb] = s
    m = jnp.max(jnp.maximum(mx, fold(s)), axis=-1, keepdims=True)

    def pass2(kb, carry):
      lsum, acc = carry
      k0 = pl.multiple_of(kb * Q_BLOCK, Q_BLOCK)
      e = jnp.exp((s_scr[kb] - m) * scale)
      lsum = lsum + (e[:, :half] + e[:, half:])
      acc = acc + _dot(e.astype(BF16), v_ref[0, pl.ds(k0, Q_BLOCK), B_VDIM * h:B_VDIM * (h + 1)])
      return lsum, acc

    zero = jnp.zeros((Q_BLOCK, half), F32)
    lsum, acc = lax.fori_loop(0, qb + 1, pass2, (zero, zero))
    o = acc / jnp.sum(lsum, axis=-1, keepdims=True)
    hs = slice(B_VDIM * h, B_VDIM * (h + 1))
    o_ref[0, :, hs] = (o * _silu(gate_ref[0, :, hs])).astype(BF16)


def _attn_b(q, k, v, gate):
  b, s, _ = q.shape
  return pl.pallas_call(
      _attn_b_kernel,
      grid=(b, s // Q_BLOCK),
      in_specs=[pl.BlockSpec((1, Q_BLOCK, B_HEADS * B_QW), lambda i, j: (i, j, 0)),
                pl.BlockSpec((1, s, B_HEADS * B_QW), lambda i, j: (i, 0, 0)),
                pl.BlockSpec((1, s, B_WIDTH), lambda i, j: (i, 0, 0)),
                pl.BlockSpec((1, Q_BLOCK, B_WIDTH), lambda i, j: (i, j, 0))],
      out_specs=pl.BlockSpec((1, Q_BLOCK, B_WIDTH), lambda i, j: (i, j, 0)),
      out_shape=jax.ShapeDtypeStruct((b, s, B_WIDTH), BF16),
      scratch_shapes=[pltpu.VMEM((s // Q_BLOCK, Q_BLOCK, Q_BLOCK), F32)],
      compiler_params=pltpu.CompilerParams(dimension_semantics=("parallel", "arbitrary"),
                                           vmem_limit_bytes=VMEM_LIMIT),
      name="attn_b",
  )(q, k, v, gate)


def _out_ple_kernel(ma_ref, mb_ref, h_ref, p_ref, woa_ref, wob_ref, gple_ref, wpg_ref, bpg_ref,
                    wpe_ref, gfin_ref, o_ref, *, final_norm):
  h1 = h_ref[...] + (_dot(ma_ref[...], woa_ref[...]) + _dot(mb_ref[...], wob_ref[...]))
  xn = _rms(h1, gple_ref[...]).astype(BF16)
  z = _dot(xn, wpg_ref[...]) + bpg_ref[...]
  gate = 1.0 / (1.0 + jnp.exp(-z))
  pe = _dot(p_ref[...].astype(BF16), wpe_ref[...])
  h2 = h1 + pe * gate
  o_ref[...] = _rms(h2, gfin_ref[...]) if final_norm else h2


def _out_ple(ma, mb, h, p, w, gfin, final_norm):
  n = h.shape[0]
  t = TOKEN_TILE
  tok = lambda width: pl.BlockSpec((t, width), lambda i: (i, 0))
  full = lambda a: pl.BlockSpec(a.shape, lambda i: (0, 0))
  weights = [w["woa"], w["wob"], w["gple"], w["wpg"], w["bpg"], w["wpe"], gfin]
  return pl.pallas_call(
      functools.partial(_out_ple_kernel, final_norm=final_norm),
      grid=(n // t,),
      in_specs=[tok(A_WIDTH), tok(B_WIDTH), tok(D_MODEL), tok(D_PLE)] + [full(a) for a in weights],
      out_specs=tok(D_MODEL),
      out_shape=jax.ShapeDtypeStruct((n, D_MODEL), F32),
      compiler_params=pltpu.CompilerParams(dimension_semantics=("parallel",),
                                           vmem_limit_bytes=VMEM_LIMIT),
      name="out_ple",
  )(ma, mb, h, p, *weights)


def _prep_layer(norm_mix, w_in, g_q, w_uq, g_kv, w_ukv, w_out, norm_ple, w_pe, w_pg, b_pg):
  half = B_ROPE // 2
  c = 0
  cols = {}
  for name, width in (("q", A_WIDTH), ("k", A_WIDTH), ("v", A_WIDTH), ("ga", A_WIDTH),
                      ("cq", Q_LORA), ("ckv", KV_LORA), ("kr", B_ROPE), ("gb", B_WIDTH)):
    cols[name] = w_in[:, c:c + width]
    c += width
  pad = jnp.zeros((D_MODEL, LANES - B_ROPE), F32)
  kr1, kr2 = cols["kr"][:, :half], cols["kr"][:, half:]
  wmla = jnp.concatenate([cols["cq"], cols["ckv"], kr1, kr2, pad, kr2, kr1, pad], axis=1)
  uq = w_uq.reshape(Q_LORA, B_HEADS, B_NOPE + B_ROPE)
  r1, r2 = uq[..., B_NOPE:B_NOPE + half], uq[..., B_NOPE + half:]
  zpad = jnp.zeros((Q_LORA, B_HEADS, LANES - B_ROPE), F32)
  wuq = jnp.concatenate([uq[..., :B_NOPE], r1, r2, zpad, r2, r1, zpad], axis=-1)
  return {
      "gmix": norm_mix[None, :],
      "wq": cols["q"].astype(BF16), "wk": cols["k"].astype(BF16), "wv": cols["v"].astype(BF16),
      "wga": cols["ga"].astype(BF16), "wgb": cols["gb"].astype(BF16),
      "wmla": wmla.astype(BF16),
      "gq": g_q[None, :], "gkv": g_kv[None, :],
      "wuq": wuq.reshape(Q_LORA, B_HEADS * 3 * LANES).astype(BF16),
      "wukv": w_ukv.astype(BF16),
      "woa": w_out[:A_WIDTH].astype(BF16), "wob": w_out[A_WIDTH:].astype(BF16),
      "gple": norm_ple[None, :], "wpg": w_pg.astype(BF16), "bpg": b_pg[None, :],
      "wpe": w_pe.astype(BF16),
  }


def kernel(x, p, positions, norm_mix, w_in, rel_bias, g_q, w_uq, g_kv, w_ukv, w_out,
           norm_ple, w_pe, w_pg, b_pg, norm_final):
  b, s, d = x.shape
  depth = w_in.shape[0]
  n = b * s
  cos, sin = _rope_tables(positions)
  h = x.reshape(n, d)
  gfin = norm_final[None, :]
  for i in range(depth):
    w = _prep_layer(norm_mix[i], w_in[i], g_q[i], w_uq[i], g_kv[i], w_ukv[i], w_out[i],
                    norm_ple[i], w_pe[i], w_pg[i], b_pg[i])
    tab = _bias_tables(rel_bias[i])
    qa, ka, va, ga, gb, qb, kb, vb = _in_proj(h, w, cos, sin)
    seq = lambda a: a.reshape(b, s, a.shape[-1])
    ma = _attn_a(seq(qa), seq(ka), seq(va), seq(ga), tab)
    mb = _attn_b(seq(qb), seq(kb), seq(vb), seq(gb))
    h = _out_ple(ma.reshape(n, A_WIDTH), mb.reshape(n, B_WIDTH), h, p[i].reshape(n, D_PLE),
                 w, gfin, final_norm=(i == depth - 1))
  return h.reshape(b, s, d)
```

```python
import functools
import math

import jax
import jax.numpy as jnp
from jax import lax
from jax.experimental import pallas as pl
from jax.experimental.pallas import tpu as pltpu

D_MODEL = 1024
CHUNK = 64
D_PLE = 256
EPS = 1e-6
A_HEADS = 8
A_HEAD_DIM = 64
A_WIDTH = A_HEADS * A_HEAD_DIM
A_LEFT_CHUNKS = 8
REL_MAX = 128
B_HEADS = 4
B_NOPE = 128
B_ROPE = 64
B_VDIM = 128
B_WIDTH = B_HEADS * B_VDIM
Q_LORA = 256
KV_LORA = 128
ROPE_THETA = 10000.0

LANES = 128
TOKEN_TILE = 512
Q_BLOCK = 256
A_LEFT = A_LEFT_CHUNKS * CHUNK
A_WINDOW = A_LEFT + Q_BLOCK
B_QW = B_NOPE + LANES
NEG_BIG = -1e30
LOG2E = math.log2(math.e)
VMEM_LIMIT = 56 * 1024 * 1024

BF16 = jnp.bfloat16
F32 = jnp.float32


def _dot(a, b):
  return jnp.dot(a, b, preferred_element_type=F32)


def _dot_nt(a, b):
  return lax.dot_general(a, b, (((1,), (1,)), ((), ())), preferred_element_type=F32)


def _rms(x, g):
  y = x * lax.rsqrt(jnp.mean(x * x, axis=-1, keepdims=True) + EPS)
  return y * g


def _silu(g):
  return g * (1.0 / (1.0 + jnp.exp(-g)))


def _rope_kernel(pos_ref, invf_ref, cmask_ref, ssign_ref, c_ref, s_ref):
  ang = pos_ref[...].astype(F32) * invf_ref[...]
  c_ref[...] = jnp.cos(ang) * cmask_ref[...]
  s_ref[...] = jnp.sin(ang) * ssign_ref[...]


def _rope_tables(positions):
  n = positions.size
  half = B_ROPE // 2
  inv_freq = ROPE_THETA ** (-jnp.arange(0, B_ROPE, 2, dtype=F32) / B_ROPE)
  zeros = jnp.zeros((LANES - 2 * half,), F32)
  invf = jnp.concatenate([inv_freq, inv_freq, zeros])[None, :]
  cmask = jnp.concatenate([jnp.ones((2 * half,), F32), zeros])[None, :]
  ssign = jnp.concatenate([-jnp.ones((half,), F32), jnp.ones((half,), F32), zeros])[None, :]
  pos = positions.reshape(n, 1)
  tile = 2048
  row = pl.BlockSpec((1, LANES), lambda i: (0, 0))
  return pl.pallas_call(
      _rope_kernel,
      grid=(n // tile,),
      in_specs=[pl.BlockSpec((tile, 1), lambda i: (i, 0)), row, row, row],
      out_specs=[pl.BlockSpec((tile, LANES), lambda i: (i, 0))] * 2,
      out_shape=[jax.ShapeDtypeStruct((n, LANES), F32)] * 2,
      compiler_params=pltpu.CompilerParams(dimension_semantics=("parallel",)),
      name="rope_tables",
  )(pos, invf, cmask, ssign)


G_LEN = 1024


def _bias_table_kernel(g_ref, tab_ref):
  g = jnp.broadcast_to(g_ref[0], (Q_BLOCK, G_LEN))
  t = pltpu.roll(g, G_LEN - Q_BLOCK, 1, stride=1, stride_axis=0)[:, :A_WINDOW]
  qi = lax.broadcasted_iota(jnp.int32, (Q_BLOCK, A_WINDOW), 0)
  kk = lax.broadcasted_iota(jnp.int32, (Q_BLOCK, A_WINDOW), 1)
  qc = (qi + A_LEFT) // CHUNK
  kc = kk // CHUNK
  valid = (kc <= qc) & (kc >= qc - A_LEFT_CHUNKS)
  tab_ref[0] = jnp.where(valid, t * LOG2E, NEG_BIG)


def _bias_tables(rel_bias):
  flat = A_LEFT + Q_BLOCK - REL_MAX
  rev = rel_bias[:, ::-1]
  g = jnp.concatenate([
      jnp.broadcast_to(rel_bias[:, -1:], (A_HEADS, flat)),
      rev[:, : 2 * REL_MAX],
      jnp.broadcast_to(rel_bias[:, :1], (A_HEADS, G_LEN - flat - 2 * REL_MAX)),
  ], axis=1)[:, None, :]
  return pl.pallas_call(
      _bias_table_kernel,
      grid=(A_HEADS,),
      in_specs=[pl.BlockSpec((1, 1, G_LEN), lambda h: (h, 0, 0))],
      out_specs=pl.BlockSpec((1, Q_BLOCK, A_WINDOW), lambda h: (h, 0, 0)),
      out_shape=jax.ShapeDtypeStruct((A_HEADS, Q_BLOCK, A_WINDOW), F32),
      compiler_params=pltpu.CompilerParams(dimension_semantics=("parallel",)),
      name="bias_table",
  )(g)


def _in_proj_kernel(h_ref, gmix_ref, wq_ref, wk_ref, wv_ref, wga_ref, wgb_ref, wmla_ref,
                    gq_ref, gkv_ref, wuq_ref, wukv_ref, cos_ref, sin_ref,
                    q_out, k_out, v_out, ga_out, gb_out, qb_out, kb_out, vb_out):
  xn = _rms(h_ref[...], gmix_ref[...]).astype(BF16)
  q_out[...] = (_dot(xn, wq_ref[...]) * (A_HEAD_DIM ** -0.5 * LOG2E)).astype(BF16)
  k_out[...] = _dot(xn, wk_ref[...]).astype(BF16)
  v_out[...] = _dot(xn, wv_ref[...]).astype(BF16)
  ga_out[...] = _dot(xn, wga_ref[...])
  gb_out[...] = _dot(xn, wgb_ref[...])

  zm = _dot(xn, wmla_ref[...])
  cos = cos_ref[...]
  sin = sin_ref[...]
  cqn = _rms(zm[:, :Q_LORA], gq_ref[...]).astype(BF16)
  ckvn = _rms(zm[:, Q_LORA:Q_LORA + KV_LORA], gkv_ref[...]).astype(BF16)
  kr0 = Q_LORA + KV_LORA
  k_rope = (zm[:, kr0:kr0 + LANES] * cos + zm[:, kr0 + LANES:kr0 + 2 * LANES] * sin).astype(BF16)

  qall = _dot(cqn, wuq_ref[...])
  kv = _dot(ckvn, wukv_ref[...])
  ones = jnp.ones((h_ref.shape[0], B_QW - B_VDIM), BF16)
  for h in range(B_HEADS):
    qb = 3 * LANES * h
    q_rope = qall[:, qb + LANES:qb + 2 * LANES] * cos + qall[:, qb + 2 * LANES:qb + 3 * LANES] * sin
    qb_out[:, B_QW * h:B_QW * h + B_NOPE] = qall[:, qb:qb + LANES].astype(BF16)
    qb_out[:, B_QW * h + B_NOPE:B_QW * (h + 1)] = q_rope.astype(BF16)
    kb_out[:, B_QW * h:B_QW * h + B_NOPE] = kv[:, 2 * LANES * h:2 * LANES * h + B_NOPE].astype(BF16)
    kb_out[:, B_QW * h + B_NOPE:B_QW * (h + 1)] = k_rope
    vb_out[:, B_QW * h:B_QW * h + B_VDIM] = kv[:, 2 * LANES * h + B_NOPE:2 * LANES * (h + 1)].astype(BF16)
    vb_out[:, B_QW * h + B_VDIM:B_QW * (h + 1)] = ones


def _in_proj(h, w, cos, sin):
  n = h.shape[0]
  t = TOKEN_TILE
  tok = lambda width: pl.BlockSpec((t, width), lambda i: (i, 0))
  full = lambda a: pl.BlockSpec(a.shape, lambda i: (0, 0))
  weights = [w["gmix"], w["wq"], w["wk"], w["wv"], w["wga"], w["wgb"], w["wmla"],
             w["gq"], w["gkv"], w["wuq"], w["wukv"]]
  out_widths = [(A_WIDTH, BF16), (A_WIDTH, BF16), (A_WIDTH, BF16), (A_WIDTH, F32), (B_WIDTH, F32),
                (B_HEADS * B_QW, BF16), (B_HEADS * B_QW, BF16), (B_HEADS * B_QW, BF16)]
  return pl.pallas_call(
      _in_proj_kernel,
      grid=(n // t,),
      in_specs=[tok(D_MODEL)] + [full(a) for a in weights] + [tok(LANES), tok(LANES)],
      out_specs=[tok(wd) for wd, _ in out_widths],
      out_shape=[jax.ShapeDtypeStruct((n, wd), dt) for wd, dt in out_widths],
      compiler_params=pltpu.CompilerParams(dimension_semantics=("parallel",),
                                           vmem_limit_bytes=VMEM_LIMIT),
      name="in_proj",
  )(h, *weights, cos, sin)


def _attn_a_kernel(q_ref, k_ref, v_ref, gate_ref, tab_ref, o_ref):
  seq_len = q_ref.shape[1]
  lane = lax.broadcasted_iota(jnp.int32, (1, LANES), 1)
  head_rows = ((lane < A_HEAD_DIM).astype(BF16), (lane >= A_HEAD_DIM).astype(BF16))
  lo_mask = lax.broadcasted_iota(jnp.int32, (Q_BLOCK, LANES), 1) < A_HEAD_DIM
  ones = jnp.ones((A_WINDOW, LANES), BF16)
  for qb in range(seq_len // Q_BLOCK):
    rows = slice(Q_BLOCK * qb, Q_BLOCK * (qb + 1))
    k0 = max(0, Q_BLOCK * qb - A_LEFT)
    nk = Q_BLOCK * (qb + 1) - k0
    col0 = A_WINDOW - nk
    qp = q_ref[0, rows, :]
    kp = k_ref[0, k0:k0 + nk, :]
    vp = jnp.concatenate([v_ref[0, k0:k0 + nk, :], ones[:nk]], axis=1)
    outs = []
    for hh in range(2):
      s = _dot_nt(qp * head_rows[hh], kp) + tab_ref[hh, :, col0:]
      e = jnp.exp2(s - jnp.max(s, axis=-1, keepdims=True))
      pv = _dot(e.astype(BF16), vp)
      outs.append(pv[:, :LANES] / pv[:, LANES:])
    o = jnp.where(lo_mask, outs[0], outs[1])
    o_ref[0, rows, :] = (o * _silu(gate_ref[0, rows, :])).astype(BF16)


def _attn_a(q, k, v, gate, tab):
  b, s, _ = q.shape
  pair = pl.BlockSpec((1, s, LANES), lambda p, i: (i, 0, p))
  return pl.pallas_call(
      _attn_a_kernel,
      grid=(A_HEADS // 2, b),
      in_specs=[pair, pair, pair, pair,
                pl.BlockSpec((2, Q_BLOCK, A_WINDOW), lambda p, i: (p, 0, 0))],
      out_specs=pair,
      out_shape=jax.ShapeDtypeStruct((b, s, A_WIDTH), BF16),
      compiler_params=pltpu.CompilerParams(dimension_semantics=("parallel", "parallel"),
                                           vmem_limit_bytes=VMEM_LIMIT),
      name="attn_a",
  )(q, k, v, gate, tab)


def _attn_b_kernel(q_ref, k_ref, v_ref, gate_ref, o_ref):
  seq_len = q_ref.shape[1]
  c = (B_NOPE + B_ROPE) ** -0.5 * LOG2E
  qi = lax.broadcasted_iota(jnp.int32, (Q_BLOCK, Q_BLOCK), 0)
  kj = lax.broadcasted_iota(jnp.int32, (Q_BLOCK, Q_BLOCK), 1)
  diag_ok = (kj // CHUNK) <= (qi // CHUNK)
  for qb in range(seq_len // Q_BLOCK):
    rows = slice(Q_BLOCK * qb, Q_BLOCK * (qb + 1))
    past = Q_BLOCK * qb
    qh = q_ref[0, rows, :]
    s_d = jnp.where(diag_ok, _dot_nt(qh, k_ref[0, rows, :]), -jnp.inf)
    m = jnp.max(s_d, axis=-1, keepdims=True)
    if past:
      s_p = _dot_nt(qh, k_ref[0, :past, :])
      m = jnp.maximum(m, jnp.max(s_p, axis=-1, keepdims=True))
    pv = _dot(jnp.exp2((s_d - m) * c).astype(BF16), v_ref[0, rows, :])
    if past:
      pv = pv + _dot(jnp.exp2((s_p - m) * c).astype(BF16), v_ref[0, :past, :])
    o = pv[:, :B_VDIM] / pv[:, B_VDIM:]
    o_ref[0, rows, :] = (o * _silu(gate_ref[0, rows, :])).astype(BF16)


def _attn_b(q, k, v, gate):
  b, s, _ = q.shape
  wide = pl.BlockSpec((1, s, B_QW), lambda i, h: (i, 0, h))
  narrow = pl.BlockSpec((1, s, B_VDIM), lambda i, h: (i, 0, h))
  return pl.pallas_call(
      _attn_b_kernel,
      grid=(b, B_HEADS),
      in_specs=[wide, wide, wide, narrow],
      out_specs=narrow,
      out_shape=jax.ShapeDtypeStruct((b, s, B_WIDTH), BF16),
      compiler_params=pltpu.CompilerParams(dimension_semantics=("parallel", "parallel"),
                                           vmem_limit_bytes=VMEM_LIMIT),
      name="attn_b",
  )(q, k, v, gate)


def _out_ple_kernel(ma_ref, mb_ref, h_ref, p_ref, woa_ref, wob_ref, gple_ref, wpg_ref, bpg_ref,
                    wpe_ref, gfin_ref, o_ref, *, final_norm):
  h1 = h_ref[...] + (_dot(ma_ref[...], woa_ref[...]) + _dot(mb_ref[...], wob_ref[...]))
  xn = _rms(h1, gple_ref[...]).astype(BF16)
  z = _dot(xn, wpg_ref[...]) + bpg_ref[...]
  gate = 1.0 / (1.0 + jnp.exp(-z))
  pe = _dot(p_ref[...].astype(BF16), wpe_ref[...])
  h2 = h1 + pe * gate
  o_ref[...] = _rms(h2, gfin_ref[...]) if final_norm else h2


def _out_ple(ma, mb, h, p, w, gfin, final_norm):
  n = h.shape[0]
  t = TOKEN_TILE
  tok = lambda width: pl.BlockSpec((t, width), lambda i: (i, 0))
  full = lambda a: pl.BlockSpec(a.shape, lambda i: (0, 0))
  weights = [w["woa"], w["wob"], w["gple"], w["wpg"], w["bpg"], w["wpe"], gfin]
  return pl.pallas_call(
      functools.partial(_out_ple_kernel, final_norm=final_norm),
      grid=(n // t,),
      in_specs=[tok(A_WIDTH), tok(B_WIDTH), tok(D_MODEL), tok(D_PLE)] + [full(a) for a in weights],
      out_specs=tok(D_MODEL),
      out_shape=jax.ShapeDtypeStruct((n, D_MODEL), F32),
      compiler_params=pltpu.CompilerParams(dimension_semantics=("parallel",),
                                           vmem_limit_bytes=VMEM_LIMIT),
      name="out_ple",
  )(ma, mb, h, p, *weights)


def _prep_layer(norm_mix, w_in, g_q, w_uq, g_kv, w_ukv, w_out, norm_ple, w_pe, w_pg, b_pg):
  half = B_ROPE // 2
  c = 0
  cols = {}
  for name, width in (("q", A_WIDTH), ("k", A_WIDTH), ("v", A_WIDTH), ("ga", A_WIDTH),
                      ("cq", Q_LORA), ("ckv", KV_LORA), ("kr", B_ROPE), ("gb", B_WIDTH)):
    cols[name] = w_in[:, c:c + width]
    c += width
  pad = jnp.zeros((D_MODEL, LANES - B_ROPE), F32)
  kr1, kr2 = cols["kr"][:, :half], cols["kr"][:, half:]
  wmla = jnp.concatenate([cols["cq"], cols["ckv"], kr1, kr2, pad, kr2, kr1, pad], axis=1)
  uq = w_uq.reshape(Q_LORA, B_HEADS, B_NOPE + B_ROPE)
  r1, r2 = uq[..., B_NOPE:B_NOPE + half], uq[..., B_NOPE + half:]
  zpad = jnp.zeros((Q_LORA, B_HEADS, LANES - B_ROPE), F32)
  wuq = jnp.concatenate([uq[..., :B_NOPE], r1, r2, zpad, r2, r1, zpad], axis=-1)
  return {
      "gmix": norm_mix[None, :],
      "wq": cols["q"].astype(BF16), "wk": cols["k"].astype(BF16), "wv": cols["v"].astype(BF16),
      "wga": cols["ga"].astype(BF16), "wgb": cols["gb"].astype(BF16),
      "wmla": wmla.astype(BF16),
      "gq": g_q[None, :], "gkv": g_kv[None, :],
      "wuq": wuq.reshape(Q_LORA, B_HEADS * 3 * LANES).astype(BF16),
      "wukv": w_ukv.astype(BF16),
      "woa": w_out[:A_WIDTH].astype(BF16), "wob": w_out[A_WIDTH:].astype(BF16),
      "gple": norm_ple[None, :], "wpg": w_pg.astype(BF16), "bpg": b_pg[None, :],
      "wpe": w_pe.astype(BF16),
  }


def kernel(x, p, positions, norm_mix, w_in, rel_bias, g_q, w_uq, g_kv, w_ukv, w_out,
           norm_ple, w_pe, w_pg, b_pg, norm_final):
  b, s, d = x.shape
  depth = w_in.shape[0]
  n = b * s
  cos, sin = _rope_tables(positions)
  h = x.reshape(n, d)
  gfin = norm_final[None, :]
  for i in range(depth):
    w = _prep_layer(norm_mix[i], w_in[i], g_q[i], w_uq[i], g_kv[i], w_ukv[i], w_out[i],
                    norm_ple[i], w_pe[i], w_pg[i], b_pg[i])
    tab = _bias_tables(rel_bias[i])
    qa, ka, va, ga, gb, qb, kb, vb = _in_proj(h, w, cos, sin)
    seq = lambda a: a.reshape(b, s, a.shape[-1])
    ma = _attn_a(seq(qa), seq(ka), seq(va), seq(ga), tab)
    mb = _attn_b(seq(qb), seq(kb), seq(vb), seq(gb))
    h = _out_ple(ma.reshape(n, A_WIDTH), mb.reshape(n, B_WIDTH), h, p[i].reshape(n, D_PLE),
                 w, gfin, final_norm=(i == depth - 1))
  return h.reshape(b, s, d)
```

```python
import functools
import math

import jax
import jax.numpy as jnp
from jax import lax
from jax.experimental import pallas as pl
from jax.experimental.pallas import tpu as pltpu

D_MODEL = 1024
CHUNK = 64
D_PLE = 256
EPS = 1e-6
A_HEADS = 8
A_HEAD_DIM = 64
A_WIDTH = A_HEADS * A_HEAD_DIM
A_LEFT_CHUNKS = 8
REL_MAX = 128
B_HEADS = 4
B_NOPE = 128
B_ROPE = 64
B_VDIM = 128
B_WIDTH = B_HEADS * B_VDIM
Q_LORA = 256
KV_LORA = 128
ROPE_THETA = 10000.0

LANES = 128
TOKEN_TILE = 512
Q_BLOCK = 256
A_LEFT = A_LEFT_CHUNKS * CHUNK
A_WINDOW = A_LEFT + Q_BLOCK
B_QW = B_NOPE + LANES
NEG_BIG = -1e30
LOG2E = math.log2(math.e)
VMEM_LIMIT = 56 * 1024 * 1024

BF16 = jnp.bfloat16
F32 = jnp.float32


def _dot(a, b):
  return jnp.dot(a, b, preferred_element_type=F32)


def _dot_nt(a, b):
  return lax.dot_general(a, b, (((1,), (1,)), ((), ())), preferred_element_type=F32)


def _rms(x, g):
  y = x * lax.rsqrt(jnp.mean(x * x, axis=-1, keepdims=True) + EPS)
  return y * g


def _silu(g):
  return g * (1.0 / (1.0 + jnp.exp(-g)))


def _rope_kernel(pos_ref, invf_ref, cmask_ref, ssign_ref, c_ref, s_ref):
  ang = pos_ref[...].astype(F32) * invf_ref[...]
  c_ref[...] = jnp.cos(ang) * cmask_ref[...]
  s_ref[...] = jnp.sin(ang) * ssign_ref[...]


def _rope_tables(positions):
  n = positions.size
  half = B_ROPE // 2
  inv_freq = ROPE_THETA ** (-jnp.arange(0, B_ROPE, 2, dtype=F32) / B_ROPE)
  zeros = jnp.zeros((LANES - 2 * half,), F32)
  invf = jnp.concatenate([inv_freq, inv_freq, zeros])[None, :]
  cmask = jnp.concatenate([jnp.ones((2 * half,), F32), zeros])[None, :]
  ssign = jnp.concatenate([-jnp.ones((half,), F32), jnp.ones((half,), F32), zeros])[None, :]
  pos = positions.reshape(n, 1)
  tile = 2048
  row = pl.BlockSpec((1, LANES), lambda i: (0, 0))
  return pl.pallas_call(
      _rope_kernel,
      grid=(n // tile,),
      in_specs=[pl.BlockSpec((tile, 1), lambda i: (i, 0)), row, row, row],
      out_specs=[pl.BlockSpec((tile, LANES), lambda i: (i, 0))] * 2,
      out_shape=[jax.ShapeDtypeStruct((n, LANES), F32)] * 2,
      compiler_params=pltpu.CompilerParams(dimension_semantics=("parallel",)),
      name="rope_tables",
  )(pos, invf, cmask, ssign)


G_LEN = 1024


def _bias_table_kernel(g_ref, tab_ref):
  g = jnp.broadcast_to(g_ref[0], (Q_BLOCK, G_LEN))
  t = pltpu.roll(g, G_LEN - Q_BLOCK, 1, stride=1, stride_axis=0)[:, :A_WINDOW]
  qi = lax.broadcasted_iota(jnp.int32, (Q_BLOCK, A_WINDOW), 0)
  kk = lax.broadcasted_iota(jnp.int32, (Q_BLOCK, A_WINDOW), 1)
  qc = (qi + A_LEFT) // CHUNK
  kc = kk // CHUNK
  valid = (kc <= qc) & (kc >= qc - A_LEFT_CHUNKS)
  tab_ref[0] = jnp.where(valid, t * LOG2E, NEG_BIG)


def _bias_tables(rel_bias):
  flat = A_LEFT + Q_BLOCK - REL_MAX
  rev = rel_bias[:, ::-1]
  g = jnp.concatenate([
      jnp.broadcast_to(rel_bias[:, -1:], (A_HEADS, flat)),
      rev[:, : 2 * REL_MAX],
      jnp.broadcast_to(rel_bias[:, :1], (A_HEADS, G_LEN - flat - 2 * REL_MAX)),
  ], axis=1)[:, None, :]
  return pl.pallas_call(
      _bias_table_kernel,
      grid=(A_HEADS,),
      in_specs=[pl.BlockSpec((1, 1, G_LEN), lambda h: (h, 0, 0))],
      out_specs=pl.BlockSpec((1, Q_BLOCK, A_WINDOW), lambda h: (h, 0, 0)),
      out_shape=jax.ShapeDtypeStruct((A_HEADS, Q_BLOCK, A_WINDOW), F32),
      compiler_params=pltpu.CompilerParams(dimension_semantics=("parallel",)),
      name="bias_table",
  )(g)


def _in_proj_kernel(h_ref, gmix_ref, wq_ref, wk_ref, wv_ref, wga_ref, wgb_ref, wmla_ref,
                    gq_ref, gkv_ref, wuq_ref, wukv_ref, cos_ref, sin_ref,
                    q_out, k_out, v_out, ga_out, gb_out, qb_out, kb_out, vb_out):
  xn = _rms(h_ref[...], gmix_ref[...]).astype(BF16)
  q_out[...] = (_dot(xn, wq_ref[...]) * (A_HEAD_DIM ** -0.5 * LOG2E)).astype(BF16)
  k_out[...] = _dot(xn, wk_ref[...]).astype(BF16)
  v_out[...] = _dot(xn, wv_ref[...]).astype(BF16)
  ga_out[...] = _dot(xn, wga_ref[...])
  gb_out[...] = _dot(xn, wgb_ref[...])

  zm = _dot(xn, wmla_ref[...])
  cos = cos_ref[...]
  sin = sin_ref[...]
  cqn = _rms(zm[:, :Q_LORA], gq_ref[...]).astype(BF16)
  ckvn = _rms(zm[:, Q_LORA:Q_LORA + KV_LORA], gkv_ref[...]).astype(BF16)
  kr0 = Q_LORA + KV_LORA
  k_rope = (zm[:, kr0:kr0 + LANES] * cos + zm[:, kr0 + LANES:kr0 + 2 * LANES] * sin).astype(BF16)

  qall = _dot(cqn, wuq_ref[...])
  kv = _dot(ckvn, wukv_ref[...])
  ones = jnp.ones((h_ref.shape[0], B_QW - B_VDIM), BF16)
  for h in range(B_HEADS):
    qb = 3 * LANES * h
    q_rope = qall[:, qb + LANES:qb + 2 * LANES] * cos + qall[:, qb + 2 * LANES:qb + 3 * LANES] * sin
    qb_out[:, B_QW * h:B_QW * h + B_NOPE] = qall[:, qb:qb + LANES].astype(BF16)
    qb_out[:, B_QW * h + B_NOPE:B_QW * (h + 1)] = q_rope.astype(BF16)
    kb_out[:, B_QW * h:B_QW * h + B_NOPE] = kv[:, 2 * LANES * h:2 * LANES * h + B_NOPE].astype(BF16)
    kb_out[:, B_QW * h + B_NOPE:B_QW * (h + 1)] = k_rope
    vb_out[:, B_QW * h:B_QW * h + B_VDIM] = kv[:, 2 * LANES * h + B_NOPE:2 * LANES * (h + 1)].astype(BF16)
    vb_out[:, B_QW * h + B_VDIM:B_QW * (h + 1)] = ones


def _in_proj(h, w, cos, sin):
  n = h.shape[0]
  t = TOKEN_TILE
  tok = lambda width: pl.BlockSpec((t, width), lambda i: (i, 0))
  full = lambda a: pl.BlockSpec(a.shape, lambda i: (0, 0))
  weights = [w["gmix"], w["wq"], w["wk"], w["wv"], w["wga"], w["wgb"], w["wmla"],
             w["gq"], w["gkv"], w["wuq"], w["wukv"]]
  out_widths = [(A_WIDTH, BF16), (A_WIDTH, BF16), (A_WIDTH, BF16), (A_WIDTH, F32), (B_WIDTH, F32),
                (B_HEADS * B_QW, BF16), (B_HEADS * B_QW, BF16), (B_HEADS * B_QW, BF16)]
  return pl.pallas_call(
      _in_proj_kernel,
      grid=(n // t,),
      in_specs=[tok(D_MODEL)] + [full(a) for a in weights] + [tok(LANES), tok(LANES)],
      out_specs=[tok(wd) for wd, _ in out_widths],
      out_shape=[jax.ShapeDtypeStruct((n, wd), dt) for wd, dt in out_widths],
      compiler_params=pltpu.CompilerParams(dimension_semantics=("parallel",),
                                           vmem_limit_bytes=VMEM_LIMIT),
      name="in_proj",
  )(h, *weights, cos, sin)


def _attn_a_kernel(q_ref, k_ref, v_ref, gate_ref, tab_ref, o_ref):
  seq_len = q_ref.shape[1]
  lane = lax.broadcasted_iota(jnp.int32, (1, LANES), 1)
  head_rows = ((lane < A_HEAD_DIM).astype(BF16), (lane >= A_HEAD_DIM).astype(BF16))
  lo_mask = lax.broadcasted_iota(jnp.int32, (Q_BLOCK, LANES), 1) < A_HEAD_DIM
  ones = jnp.ones((A_WINDOW, LANES), BF16)

  def window(qb):
    k0 = max(0, Q_BLOCK * qb - A_LEFT)
    return k0, Q_BLOCK * (qb + 1) - k0

  def scores(qb, hh):
    k0, nk = window(qb)
    qp = q_ref[0, Q_BLOCK * qb:Q_BLOCK * (qb + 1), :]
    return _dot_nt(qp * head_rows[hh], k_ref[0, k0:k0 + nk, :]) + tab_ref[hh, :, A_WINDOW - nk:]

  def weighted(qb, s):
    k0, nk = window(qb)
    vp = jnp.concatenate([v_ref[0, k0:k0 + nk, :], ones[:nk]], axis=1)
    e = jnp.exp2(s - jnp.max(s, axis=-1, keepdims=True))
    pv = _dot(e.astype(BF16), vp)
    return pv[:, :LANES] / pv[:, LANES:]

  items = [(qb, hh) for qb in reversed(range(seq_len // Q_BLOCK)) for hh in range(2)]
  depth = 2
  pending = [scores(*it) for it in items[:depth]]
  outs = []
  for i, (qb, hh) in enumerate(items):
    if i + depth < len(items):
      pending.append(scores(*items[i + depth]))
    outs.append(weighted(qb, pending.pop(0)))
    if hh == 1:
      rows = slice(Q_BLOCK * qb, Q_BLOCK * (qb + 1))
      o = jnp.where(lo_mask, outs[0], outs[1])
      o_ref[0, rows, :] = (o * _silu(gate_ref[0, rows, :])).astype(BF16)
      outs = []


def _attn_a(q, k, v, gate, tab):
  b, s, _ = q.shape
  pair = pl.BlockSpec((1, s, LANES), lambda p, i: (i, 0, p))
  return pl.pallas_call(
      _attn_a_kernel,
      grid=(A_HEADS // 2, b),
      in_specs=[pair, pair, pair, pair,
                pl.BlockSpec((2, Q_BLOCK, A_WINDOW), lambda p, i: (p, 0, 0))],
      out_specs=pair,
      out_shape=jax.ShapeDtypeStruct((b, s, A_WIDTH), BF16),
      compiler_params=pltpu.CompilerParams(dimension_semantics=("parallel", "parallel"),
                                           vmem_limit_bytes=VMEM_LIMIT),
      name="attn_a",
  )(q, k, v, gate, tab)


def _attn_b_kernel(q_ref, k_ref, v_ref, gate_ref, o_ref):
  seq_len = q_ref.shape[1]
  c = (B_NOPE + B_ROPE) ** -0.5 * LOG2E
  qi = lax.broadcasted_iota(jnp.int32, (Q_BLOCK, Q_BLOCK), 0)
  kj = lax.broadcasted_iota(jnp.int32, (Q_BLOCK, Q_BLOCK), 1)
  diag_ok = (kj // CHUNK) <= (qi // CHUNK)

  def scores(qb):
    rows = slice(Q_BLOCK * qb, Q_BLOCK * (qb + 1))
    qh = q_ref[0, rows, :]
    s_d = jnp.where(diag_ok, _dot_nt(qh, k_ref[0, rows, :]), -jnp.inf)
    s_p = _dot_nt(qh, k_ref[0, :Q_BLOCK * qb, :]) if qb else None
    return s_d, s_p

  def finish(qb, s_d, s_p):
    rows = slice(Q_BLOCK * qb, Q_BLOCK * (qb + 1))
    m = jnp.max(s_d, axis=-1, keepdims=True)
    if qb:
      m = jnp.maximum(m, jnp.max(s_p, axis=-1, keepdims=True))
    pv = _dot(jnp.exp2((s_d - m) * c).astype(BF16), v_ref[0, rows, :])
    if qb:
      pv = pv + _dot(jnp.exp2((s_p - m) * c).astype(BF16), v_ref[0, :Q_BLOCK * qb, :])
    o = pv[:, :B_VDIM] / pv[:, B_VDIM:]
    o_ref[0, rows, :] = (o * _silu(gate_ref[0, rows, :])).astype(BF16)

  order = list(reversed(range(seq_len // Q_BLOCK)))
  depth = 2
  pending = [scores(qb) for qb in order[:depth]]
  for i, qb in enumerate(order):
    if i + depth < len(order):
      pending.append(scores(order[i + depth]))
    finish(qb, *pending.pop(0))


def _attn_b(q, k, v, gate):
  b, s, _ = q.shape
  wide = pl.BlockSpec((1, s, B_QW), lambda i, h: (i, 0, h))
  narrow = pl.BlockSpec((1, s, B_VDIM), lambda i, h: (i, 0, h))
  return pl.pallas_call(
      _attn_b_kernel,
      grid=(b, B_HEADS),
      in_specs=[wide, wide, wide, narrow],
      out_specs=narrow,
      out_shape=jax.ShapeDtypeStruct((b, s, B_WIDTH), BF16),
      compiler_params=pltpu.CompilerParams(dimension_semantics=("parallel", "parallel"),
                                           vmem_limit_bytes=VMEM_LIMIT),
      name="attn_b",
  )(q, k, v, gate)


def _out_ple_kernel(ma_ref, mb_ref, h_ref, p_ref, woa_ref, wob_ref, gple_ref, wpg_ref, bpg_ref,
                    wpe_ref, gfin_ref, o_ref, *, final_norm):
  h1 = h_ref[...] + (_dot(ma_ref[...], woa_ref[...]) + _dot(mb_ref[...], wob_ref[...]))
  xn = _rms(h1, gple_ref[...]).astype(BF16)
  z = _dot(xn, wpg_ref[...]) + bpg_ref[...]
  gate = 1.0 / (1.0 + jnp.exp(-z))
  pe = _dot(p_ref[...].astype(BF16), wpe_ref[...])
  h2 = h1 + pe * gate
  o_ref[...] = _rms(h2, gfin_ref[...]) if final_norm else h2


def _out_ple(ma, mb, h, p, w, gfin, final_norm):
  n = h.shape[0]
  t = TOKEN_TILE
  tok = lambda width: pl.BlockSpec((t, width), lambda i: (i, 0))
  full = lambda a: pl.BlockSpec(a.shape, lambda i: (0, 0))
  weights = [w["woa"], w["wob"], w["gple"], w["wpg"], w["bpg"], w["wpe"], gfin]
  return pl.pallas_call(
      functools.partial(_out_ple_kernel, final_norm=final_norm),
      grid=(n // t,),
      in_specs=[tok(A_WIDTH), tok(B_WIDTH), tok(D_MODEL), tok(D_PLE)] + [full(a) for a in weights],
      out_specs=tok(D_MODEL),
      out_shape=jax.ShapeDtypeStruct((n, D_MODEL), F32),
      compiler_params=pltpu.CompilerParams(dimension_semantics=("parallel",),
                                           vmem_limit_bytes=VMEM_LIMIT),
      name="out_ple",
  )(ma, mb, h, p, *weights)


def _prep_layer(norm_mix, w_in, g_q, w_uq, g_kv, w_ukv, w_out, norm_ple, w_pe, w_pg, b_pg):
  half = B_ROPE // 2
  c = 0
  cols = {}
  for name, width in (("q", A_WIDTH), ("k", A_WIDTH), ("v", A_WIDTH), ("ga", A_WIDTH),
                      ("cq", Q_LORA), ("ckv", KV_LORA), ("kr", B_ROPE), ("gb", B_WIDTH)):
    cols[name] = w_in[:, c:c + width]
    c += width
  pad = jnp.zeros((D_MODEL, LANES - B_ROPE), F32)
  kr1, kr2 = cols["kr"][:, :half], cols["kr"][:, half:]
  wmla = jnp.concatenate([cols["cq"], cols["ckv"], kr1, kr2, pad, kr2, kr1, pad], axis=1)
  uq = w_uq.reshape(Q_LORA, B_HEADS, B_NOPE + B_ROPE)
  r1, r2 = uq[..., B_NOPE:B_NOPE + half], uq[..., B_NOPE + half:]
  zpad = jnp.zeros((Q_LORA, B_HEADS, LANES - B_ROPE), F32)
  wuq = jnp.concatenate([uq[..., :B_NOPE], r1, r2, zpad, r2, r1, zpad], axis=-1)
  return {
      "gmix": norm_mix[None, :],
      "wq": cols["q"].astype(BF16), "wk": cols["k"].astype(BF16), "wv": cols["v"].astype(BF16),
      "wga": cols["ga"].astype(BF16), "wgb": cols["gb"].astype(BF16),
      "wmla": wmla.astype(BF16),
      "gq": g_q[None, :], "gkv": g_kv[None, :],
      "wuq": wuq.reshape(Q_LORA, B_HEADS * 3 * LANES).astype(BF16),
      "wukv": w_ukv.astype(BF16),
      "woa": w_out[:A_WIDTH].astype(BF16), "wob": w_out[A_WIDTH:].astype(BF16),
      "gple": norm_ple[None, :], "wpg": w_pg.astype(BF16), "bpg": b_pg[None, :],
      "wpe": w_pe.astype(BF16),
  }


def kernel(x, p, positions, norm_mix, w_in, rel_bias, g_q, w_uq, g_kv, w_ukv, w_out,
           norm_ple, w_pe, w_pg, b_pg, norm_final):
  b, s, d = x.shape
  depth = w_in.shape[0]
  n = b * s
  cos, sin = _rope_tables(positions)
  h = x.reshape(n, d)
  gfin = norm_final[None, :]
  for i in range(depth):
    w = _prep_layer(norm_mix[i], w_in[i], g_q[i], w_uq[i], g_kv[i], w_ukv[i], w_out[i],
                    norm_ple[i], w_pe[i], w_pg[i], b_pg[i])
    tab = _bias_tables(rel_bias[i])
    qa, ka, va, ga, gb, qb, kb, vb = _in_proj(h, w, cos, sin)
    seq = lambda a: a.reshape(b, s, a.shape[-1])
    ma = _attn_a(seq(qa), seq(ka), seq(va), seq(ga), tab)
    mb = _attn_b(seq(qb), seq(kb), seq(vb), seq(gb))
    h = _out_ple(ma.reshape(n, A_WIDTH), mb.reshape(n, B_WIDTH), h, p[i].reshape(n, D_PLE),
                 w, gfin, final_norm=(i == depth - 1))
  return h.reshape(b, s, d)
```

```python
import functools
import math

import jax
import jax.numpy as jnp
from jax import lax
from jax.experimental import pallas as pl
from jax.experimental.pallas import tpu as pltpu

D_MODEL = 1024
CHUNK = 64
D_PLE = 256
EPS = 1e-6
A_HEADS = 8
A_HEAD_DIM = 64
A_WIDTH = A_HEADS * A_HEAD_DIM
A_LEFT_CHUNKS = 8
REL_MAX = 128
B_HEADS = 4
B_NOPE = 128
B_ROPE = 64
B_VDIM = 128
B_WIDTH = B_HEADS * B_VDIM
Q_LORA = 256
KV_LORA = 128
ROPE_THETA = 10000.0

LANES = 128
TOKEN_TILE = 512
Q_BLOCK = 256
A_LEFT = A_LEFT_CHUNKS * CHUNK
A_WINDOW = A_LEFT + Q_BLOCK
B_QW = B_NOPE + LANES
NEG_BIG = -1e30
LOG2E = math.log2(math.e)
VMEM_LIMIT = 56 * 1024 * 1024

BF16 = jnp.bfloat16
F32 = jnp.float32


def _dot(a, b):
  return jnp.dot(a, b, preferred_element_type=F32)


def _dot_nt(a, b):
  return lax.dot_general(a, b, (((1,), (1,)), ((), ())), preferred_element_type=F32)


def _rms(x, g):
  y = x * lax.rsqrt(jnp.mean(x * x, axis=-1, keepdims=True) + EPS)
  return y * g


def _silu(g):
  return g * (1.0 / (1.0 + jnp.exp(-g)))


def _inv_freq_column():
  inv_freq = ROPE_THETA ** (-jnp.arange(0, B_ROPE, 2, dtype=F32) / B_ROPE)
  return inv_freq[:, None]


def _rope_tile(pos_row, inv_freq_col):
  t = pos_row.shape[1]
  ang = pos_row.astype(F32) * inv_freq_col
  c, s = jnp.cos(ang), jnp.sin(ang)
  zeros = jnp.zeros((LANES - B_ROPE, t), F32)
  return jnp.concatenate([c, c, zeros], axis=0).T, jnp.concatenate([-s, s, zeros], axis=0).T


G_LEN = 1024


def _bias_table_kernel(g_ref, tab_ref):
  g = jnp.broadcast_to(g_ref[0], (Q_BLOCK, G_LEN))
  t = pltpu.roll(g, G_LEN - Q_BLOCK, 1, stride=1, stride_axis=0)[:, :A_WINDOW]
  qi = lax.broadcasted_iota(jnp.int32, (Q_BLOCK, A_WINDOW), 0)
  kk = lax.broadcasted_iota(jnp.int32, (Q_BLOCK, A_WINDOW), 1)
  qc = (qi + A_LEFT) // CHUNK
  kc = kk // CHUNK
  valid = (kc <= qc) & (kc >= qc - A_LEFT_CHUNKS)
  tab_ref[0] = jnp.where(valid, t * LOG2E, NEG_BIG)


def _bias_tables(rel_bias):
  flat = A_LEFT + Q_BLOCK - REL_MAX
  rev = rel_bias[:, ::-1]
  g = jnp.concatenate([
      jnp.broadcast_to(rel_bias[:, -1:], (A_HEADS, flat)),
      rev[:, : 2 * REL_MAX],
      jnp.broadcast_to(rel_bias[:, :1], (A_HEADS, G_LEN - flat - 2 * REL_MAX)),
  ], axis=1)[:, None, :]
  return pl.pallas_call(
      _bias_table_kernel,
      grid=(A_HEADS,),
      in_specs=[pl.BlockSpec((1, 1, G_LEN), lambda h: (h, 0, 0))],
      out_specs=pl.BlockSpec((1, Q_BLOCK, A_WINDOW), lambda h: (h, 0, 0)),
      out_shape=jax.ShapeDtypeStruct((A_HEADS, Q_BLOCK, A_WINDOW), F32),
      compiler_params=pltpu.CompilerParams(dimension_semantics=("parallel",)),
      name="bias_table",
  )(g)


def _in_proj_kernel(h_ref, gmix_ref, wq_ref, wk_ref, wv_ref, wga_ref, wgb_ref, wmla_ref,
                    gq_ref, gkv_ref, wuq_ref, wukv_ref, *refs, make_tables):
  if make_tables:
    (pos_ref, invf_ref, q_out, k_out, v_out, ga_out, gb_out, qb_out, kb_out, vb_out,
     cos_out, sin_out) = refs
  else:
    cos_ref, sin_ref, q_out, k_out, v_out, ga_out, gb_out, qb_out, kb_out, vb_out = refs
  xn = _rms(h_ref[...], gmix_ref[...]).astype(BF16)
  zm = _dot(xn, wmla_ref[...])
  cqn = _rms(zm[:, :Q_LORA], gq_ref[...]).astype(BF16)
  ckvn = _rms(zm[:, Q_LORA:Q_LORA + KV_LORA], gkv_ref[...]).astype(BF16)
  qall = _dot(cqn, wuq_ref[...])
  kv = _dot(ckvn, wukv_ref[...])

  q_out[...] = (_dot(xn, wq_ref[...]) * (A_HEAD_DIM ** -0.5 * LOG2E)).astype(BF16)
  k_out[...] = _dot(xn, wk_ref[...]).astype(BF16)
  v_out[...] = _dot(xn, wv_ref[...]).astype(BF16)
  ga_out[...] = _dot(xn, wga_ref[...])
  gb_out[...] = _dot(xn, wgb_ref[...])

  if make_tables:
    cos, sin = _rope_tile(pos_ref[0], invf_ref[...])
    cos_out[...] = cos
    sin_out[...] = sin
  else:
    cos = cos_ref[...]
    sin = sin_ref[...]
  kr0 = Q_LORA + KV_LORA
  k_rope = (zm[:, kr0:kr0 + LANES] * cos + zm[:, kr0 + LANES:kr0 + 2 * LANES] * sin).astype(BF16)
  ones = jnp.ones((h_ref.shape[0], B_QW - B_VDIM), BF16)
  for h in range(B_HEADS):
    qb = 3 * LANES * h
    q_rope = qall[:, qb + LANES:qb + 2 * LANES] * cos + qall[:, qb + 2 * LANES:qb + 3 * LANES] * sin
    qb_out[:, B_QW * h:B_QW * h + B_NOPE] = qall[:, qb:qb + LANES].astype(BF16)
    qb_out[:, B_QW * h + B_NOPE:B_QW * (h + 1)] = q_rope.astype(BF16)
    kb_out[:, B_QW * h:B_QW * h + B_NOPE] = kv[:, 2 * LANES * h:2 * LANES * h + B_NOPE].astype(BF16)
    kb_out[:, B_QW * h + B_NOPE:B_QW * (h + 1)] = k_rope
    vb_out[:, B_QW * h:B_QW * h + B_VDIM] = kv[:, 2 * LANES * h + B_NOPE:2 * LANES * (h + 1)].astype(BF16)
    vb_out[:, B_QW * h + B_VDIM:B_QW * (h + 1)] = ones


def _layer_spec(a, layer):
  return pl.BlockSpec((None,) + a.shape[1:], lambda i: (layer,) + (0,) * (a.ndim - 1))


def _in_proj(h, w, layer, positions=None, tables=None):
  n = h.shape[0]
  t = TOKEN_TILE
  tok = lambda width: pl.BlockSpec((t, width), lambda i: (i, 0))
  w_in_block = lambda j: pl.BlockSpec((None, D_MODEL, A_WIDTH), lambda i: (layer, 0, j))
  small = [w["gmix"], w["wgb"], w["wmla"], w["gq"], w["gkv"], w["wuq"], w["wukv"]]
  in_specs = ([tok(D_MODEL), _layer_spec(small[0], layer)] + [w_in_block(j) for j in range(4)]
              + [_layer_spec(a, layer) for a in small[1:]])
  operands = [h, small[0]] + [w["w_in"]] * 4 + small[1:]
  out_widths = [(A_WIDTH, BF16), (A_WIDTH, BF16), (A_WIDTH, BF16), (A_WIDTH, F32), (B_WIDTH, F32),
                (B_HEADS * B_QW, BF16), (B_HEADS * B_QW, BF16), (B_HEADS * B_QW, BF16)]
  make_tables = tables is None
  if make_tables:
    invf = _inv_freq_column()
    operands += [positions.reshape(n // t, 1, t), invf]
    in_specs += [pl.BlockSpec((1, 1, t), lambda i: (i, 0, 0)), pl.BlockSpec(invf.shape, lambda i: (0, 0))]
    out_widths += [(LANES, F32), (LANES, F32)]
  else:
    operands += list(tables)
    in_specs += [tok(LANES), tok(LANES)]
  return pl.pallas_call(
      functools.partial(_in_proj_kernel, make_tables=make_tables),
      grid=(n // t,),
      in_specs=in_specs,
      out_specs=[tok(wd) for wd, _ in out_widths],
      out_shape=[jax.ShapeDtypeStruct((n, wd), dt) for wd, dt in out_widths],
      compiler_params=pltpu.CompilerParams(dimension_semantics=("parallel",),
                                           vmem_limit_bytes=VMEM_LIMIT),
      name="in_proj",
  )(*operands)


def _attn_a_kernel(q_ref, k_ref, v_ref, gate_ref, tab_ref, o_ref):
  seq_len = q_ref.shape[1]
  lane = lax.broadcasted_iota(jnp.int32, (1, LANES), 1)
  head_rows = ((lane < A_HEAD_DIM).astype(BF16), (lane >= A_HEAD_DIM).astype(BF16))
  lo_mask = lax.broadcasted_iota(jnp.int32, (Q_BLOCK, LANES), 1) < A_HEAD_DIM
  ones = jnp.ones((A_WINDOW, LANES), BF16)

  def window(qb):
    k0 = max(0, Q_BLOCK * qb - A_LEFT)
    return k0, Q_BLOCK * (qb + 1) - k0

  def scores(qb, hh):
    k0, nk = window(qb)
    qp = q_ref[0, Q_BLOCK * qb:Q_BLOCK * (qb + 1), :]
    return _dot_nt(qp * head_rows[hh], k_ref[0, k0:k0 + nk, :]) + tab_ref[hh, :, A_WINDOW - nk:]

  def weighted(qb, s):
    k0, nk = window(qb)
    vp = jnp.concatenate([v_ref[0, k0:k0 + nk, :], ones[:nk]], axis=1)
    e = jnp.exp2(s - jnp.max(s, axis=-1, keepdims=True))
    pv = _dot(e.astype(BF16), vp)
    return pv[:, :LANES] / pv[:, LANES:]

  items = [(qb, hh) for qb in reversed(range(seq_len // Q_BLOCK)) for hh in range(2)]
  depth = 2
  pending = [scores(*it) for it in items[:depth]]
  outs = []
  for i, (qb, hh) in enumerate(items):
    if i + depth < len(items):
      pending.append(scores(*items[i + depth]))
    outs.append(weighted(qb, pending.pop(0)))
    if hh == 1:
      rows = slice(Q_BLOCK * qb, Q_BLOCK * (qb + 1))
      o = jnp.where(lo_mask, outs[0], outs[1])
      o_ref[0, rows, :] = (o * _silu(gate_ref[0, rows, :])).astype(BF16)
      outs = []


def _attn_a(q, k, v, gate, tab):
  b, s, _ = q.shape
  pair = pl.BlockSpec((1, s, LANES), lambda p, i: (i, 0, p))
  return pl.pallas_call(
      _attn_a_kernel,
      grid=(A_HEADS // 2, b),
      in_specs=[pair, pair, pair, pair,
                pl.BlockSpec((2, Q_BLOCK, A_WINDOW), lambda p, i: (p, 0, 0))],
      out_specs=pair,
      out_shape=jax.ShapeDtypeStruct((b, s, A_WIDTH), BF16),
      compiler_params=pltpu.CompilerParams(dimension_semantics=("parallel", "parallel"),
                                           vmem_limit_bytes=VMEM_LIMIT),
      name="attn_a",
  )(q, k, v, gate, tab)


def _attn_b_kernel(q_ref, k_ref, v_ref, gate_ref, o_ref):
  seq_len = q_ref.shape[1]
  c = (B_NOPE + B_ROPE) ** -0.5 * LOG2E
  qi = lax.broadcasted_iota(jnp.int32, (Q_BLOCK, Q_BLOCK), 0)
  kj = lax.broadcasted_iota(jnp.int32, (Q_BLOCK, Q_BLOCK), 1)
  diag_ok = (kj // CHUNK) <= (qi // CHUNK)

  def scores(qb):
    rows = slice(Q_BLOCK * qb, Q_BLOCK * (qb + 1))
    qh = q_ref[0, rows, :]
    s_d = jnp.where(diag_ok, _dot_nt(qh, k_ref[0, rows, :]), -jnp.inf)
    s_p = _dot_nt(qh, k_ref[0, :Q_BLOCK * qb, :]) if qb else None
    return s_d, s_p

  def finish(qb, s_d, s_p):
    rows = slice(Q_BLOCK * qb, Q_BLOCK * (qb + 1))
    m = jnp.max(s_d, axis=-1, keepdims=True)
    if qb:
      m = jnp.maximum(m, jnp.max(s_p, axis=-1, keepdims=True))
    pv = _dot(jnp.exp2((s_d - m) * c).astype(BF16), v_ref[0, rows, :])
    if qb:
      pv = pv + _dot(jnp.exp2((s_p - m) * c).astype(BF16), v_ref[0, :Q_BLOCK * qb, :])
    o = pv[:, :B_VDIM] / pv[:, B_VDIM:]
    o_ref[0, rows, :] = (o * _silu(gate_ref[0, rows, :])).astype(BF16)

  order = list(reversed(range(seq_len // Q_BLOCK)))
  depth = 2
  pending = [scores(qb) for qb in order[:depth]]
  for i, qb in enumerate(order):
    if i + depth < len(order):
      pending.append(scores(order[i + depth]))
    finish(qb, *pending.pop(0))


def _attn_b(q, k, v, gate):
  b, s, _ = q.shape
  wide = pl.BlockSpec((1, s, B_QW), lambda i, h: (i, 0, h))
  narrow = pl.BlockSpec((1, s, B_VDIM), lambda i, h: (i, 0, h))
  return pl.pallas_call(
      _attn_b_kernel,
      grid=(b, B_HEADS),
      in_specs=[wide, wide, wide, narrow],
      out_specs=narrow,
      out_shape=jax.ShapeDtypeStruct((b, s, B_WIDTH), BF16),
      compiler_params=pltpu.CompilerParams(dimension_semantics=("parallel", "parallel"),
                                           vmem_limit_bytes=VMEM_LIMIT),
      name="attn_b",
  )(q, k, v, gate)


def _out_ple_kernel(ma_ref, mb_ref, h_ref, p_ref, woa_ref, wob_ref, gple_ref, wpg_ref, bpg_ref,
                    wpe_ref, gfin_ref, o_ref, *, final_norm):
  h1 = h_ref[...] + (_dot(ma_ref[...], woa_ref[...]) + _dot(mb_ref[...], wob_ref[...]))
  xn = _rms(h1, gple_ref[...]).astype(BF16)
  z = _dot(xn, wpg_ref[...]) + bpg_ref[...]
  gate = 1.0 / (1.0 + jnp.exp(-z))
  pe = _dot(p_ref[...].astype(BF16), wpe_ref[...])
  h2 = h1 + pe * gate
  o_ref[...] = _rms(h2, gfin_ref[...]) if final_norm else h2


def _out_ple(ma, mb, h, p, w, layer, final_norm):
  n = h.shape[0]
  t = TOKEN_TILE
  tok = lambda width: pl.BlockSpec((t, width), lambda i: (i, 0))
  w_out_half = lambda j: pl.BlockSpec((None, A_WIDTH, D_MODEL), lambda i: (layer, j, 0))
  rest = [w["gple"], w["wpg"], w["bpg"], w["wpe"]]
  return pl.pallas_call(
      functools.partial(_out_ple_kernel, final_norm=final_norm),
      grid=(n // t,),
      in_specs=([tok(A_WIDTH), tok(B_WIDTH), tok(D_MODEL),
                 pl.BlockSpec((None, t, D_PLE), lambda i: (layer, i, 0)),
                 w_out_half(0), w_out_half(1)] + [_layer_spec(a, layer) for a in rest]
                + [pl.BlockSpec(w["gfin"].shape, lambda i: (0, 0))]),
      out_specs=tok(D_MODEL),
      out_shape=jax.ShapeDtypeStruct((n, D_MODEL), F32),
      compiler_params=pltpu.CompilerParams(dimension_semantics=("parallel",),
                                           vmem_limit_bytes=VMEM_LIMIT),
      name="out_ple",
  )(ma, mb, h, p, w["w_out"], w["w_out"], *rest, w["gfin"])


def _prep_params(norm_mix, w_in, g_q, w_uq, g_kv, w_ukv, w_out, norm_ple, w_pe, w_pg, b_pg,
                 norm_final):
  depth = w_in.shape[0]
  half = B_ROPE // 2
  mla0 = 4 * A_WIDTH
  kr0 = mla0 + Q_LORA + KV_LORA
  kr1, kr2 = w_in[..., kr0:kr0 + half], w_in[..., kr0 + half:kr0 + B_ROPE]
  pad = jnp.zeros((depth, D_MODEL, LANES - B_ROPE), F32)
  wmla = jnp.concatenate([w_in[..., mla0:kr0], kr1, kr2, pad, kr2, kr1, pad], axis=-1)
  uq = w_uq.reshape(depth, Q_LORA, B_HEADS, B_NOPE + B_ROPE)
  r1, r2 = uq[..., B_NOPE:B_NOPE + half], uq[..., B_NOPE + half:]
  zpad = jnp.zeros((depth, Q_LORA, B_HEADS, LANES - B_ROPE), F32)
  wuq = jnp.concatenate([uq[..., :B_NOPE], r1, r2, zpad, r2, r1, zpad], axis=-1)
  return {
      "gmix": norm_mix[:, None, :],
      "w_in": w_in.astype(BF16),
      "wgb": w_in[..., kr0 + B_ROPE:].astype(BF16),
      "wmla": wmla.astype(BF16),
      "gq": g_q[:, None, :], "gkv": g_kv[:, None, :],
      "wuq": wuq.reshape(depth, Q_LORA, B_HEADS * 3 * LANES).astype(BF16),
      "wukv": w_ukv.astype(BF16),
      "w_out": w_out.astype(BF16),
      "gple": norm_ple[:, None, :], "wpg": w_pg.astype(BF16), "bpg": b_pg[:, None, :],
      "wpe": w_pe.astype(BF16),
      "gfin": norm_final[None, :],
  }


def kernel(x, p, positions, norm_mix, w_in, rel_bias, g_q, w_uq, g_kv, w_ukv, w_out,
           norm_ple, w_pe, w_pg, b_pg, norm_final):
  b, s, d = x.shape
  depth = w_in.shape[0]
  n = b * s
  w = _prep_params(norm_mix, w_in, g_q, w_uq, g_kv, w_ukv, w_out, norm_ple, w_pe, w_pg, b_pg,
                   norm_final)
  p = p.reshape(depth, n, D_PLE)
  h = x.reshape(n, d)
  seq = lambda a: a.reshape(b, s, a.shape[-1])
  tables = None
  for i in range(depth):
    tab = _bias_tables(rel_bias[i])
    if tables is None:
      *outs, cos, sin = _in_proj(h, w, i, positions=positions)
      tables = (cos, sin)
    else:
      outs = _in_proj(h, w, i, tables=tables)
    qa, ka, va, ga, gb, qb, kb, vb = outs
    ma = _attn_a(seq(qa), seq(ka), seq(va), seq(ga), tab)
    mb = _attn_b(seq(qb), seq(kb), seq(vb), seq(gb))
    h = _out_ple(ma.reshape(n, A_WIDTH), mb.reshape(n, B_WIDTH), h, p, w, i,
                 final_norm=(i == depth - 1))
  return h.reshape(b, s, d)
```

```python
import functools
import math

import jax
import jax.numpy as jnp
from jax import lax
from jax.experimental import pallas as pl
from jax.experimental.pallas import tpu as pltpu

D_MODEL = 1024
CHUNK = 64
D_PLE = 256
EPS = 1e-6
A_HEADS = 8
A_HEAD_DIM = 64
A_WIDTH = A_HEADS * A_HEAD_DIM
A_LEFT_CHUNKS = 8
REL_MAX = 128
B_HEADS = 4
B_NOPE = 128
B_ROPE = 64
B_VDIM = 128
B_WIDTH = B_HEADS * B_VDIM
Q_LORA = 256
KV_LORA = 128
ROPE_THETA = 10000.0

LANES = 128
TOKEN_TILE = 512
OUT_TOKEN_TILE = 1024
Q_BLOCK = 256
A_STEP_PAIRS = 2
B_STEP_HEADS = 2
A_LEFT = A_LEFT_CHUNKS * CHUNK
A_WINDOW = A_LEFT + Q_BLOCK
B_QW = B_NOPE + LANES
NEG_BIG = -1e30
LOG2E = math.log2(math.e)
VMEM_LIMIT = 56 * 1024 * 1024

BF16 = jnp.bfloat16
F32 = jnp.float32


def _dot(a, b):
  return jnp.dot(a, b, preferred_element_type=F32)


def _dot_nt(a, b):
  return lax.dot_general(a, b, (((1,), (1,)), ((), ())), preferred_element_type=F32)


def _rms(x, g):
  y = x * lax.rsqrt(jnp.mean(x * x, axis=-1, keepdims=True) + EPS)
  return y * g


def _silu(g):
  return g * (1.0 / (1.0 + jnp.exp(-g)))


def _inv_freq_column():
  inv_freq = ROPE_THETA ** (-jnp.arange(0, B_ROPE, 2, dtype=F32) / B_ROPE)
  return inv_freq[:, None]


def _rope_tile(pos_row, inv_freq_col):
  t = pos_row.shape[1]
  ang = pos_row.astype(F32) * inv_freq_col
  c, s = jnp.cos(ang), jnp.sin(ang)
  zeros = jnp.zeros((LANES - B_ROPE, t), F32)
  return jnp.concatenate([c, c, zeros], axis=0).T, jnp.concatenate([-s, s, zeros], axis=0).T


G_LEN = 1024


def _bias_table_kernel(g_ref, tab_ref):
  g = jnp.broadcast_to(g_ref[0], (Q_BLOCK, G_LEN))
  t = pltpu.roll(g, G_LEN - Q_BLOCK, 1, stride=1, stride_axis=0)[:, :A_WINDOW]
  qi = lax.broadcasted_iota(jnp.int32, (Q_BLOCK, A_WINDOW), 0)
  kk = lax.broadcasted_iota(jnp.int32, (Q_BLOCK, A_WINDOW), 1)
  qc = (qi + A_LEFT) // CHUNK
  kc = kk // CHUNK
  valid = (kc <= qc) & (kc >= qc - A_LEFT_CHUNKS)
  tab_ref[0] = jnp.where(valid, t * LOG2E, NEG_BIG)


def _bias_tables(rel_bias):
  flat = A_LEFT + Q_BLOCK - REL_MAX
  rev = rel_bias[:, ::-1]
  g = jnp.concatenate([
      jnp.broadcast_to(rel_bias[:, -1:], (A_HEADS, flat)),
      rev[:, : 2 * REL_MAX],
      jnp.broadcast_to(rel_bias[:, :1], (A_HEADS, G_LEN - flat - 2 * REL_MAX)),
  ], axis=1)[:, None, :]
  return pl.pallas_call(
      _bias_table_kernel,
      grid=(A_HEADS,),
      in_specs=[pl.BlockSpec((1, 1, G_LEN), lambda h: (h, 0, 0))],
      out_specs=pl.BlockSpec((1, Q_BLOCK, A_WINDOW), lambda h: (h, 0, 0)),
      out_shape=jax.ShapeDtypeStruct((A_HEADS, Q_BLOCK, A_WINDOW), F32),
      compiler_params=pltpu.CompilerParams(dimension_semantics=("parallel",)),
      name="bias_table",
  )(g)


def _in_proj_kernel(h_ref, gmix_ref, wq_ref, wk_ref, wv_ref, wga_ref, wgb_ref, wmla_ref,
                    gq_ref, gkv_ref, wuq_ref, wukv_ref, *refs, make_tables):
  if make_tables:
    (pos_ref, invf_ref, q_out, k_out, v_out, ga_out, gb_out, qb_out, kb_out, vb_out,
     cos_out, sin_out) = refs
  else:
    cos_ref, sin_ref, q_out, k_out, v_out, ga_out, gb_out, qb_out, kb_out, vb_out = refs
  xn = _rms(h_ref[...], gmix_ref[...]).astype(BF16)
  zm = _dot(xn, wmla_ref[...])
  cqn = _rms(zm[:, :Q_LORA], gq_ref[...]).astype(BF16)
  ckvn = _rms(zm[:, Q_LORA:Q_LORA + KV_LORA], gkv_ref[...]).astype(BF16)
  qall = _dot(cqn, wuq_ref[...])
  kv = _dot(ckvn, wukv_ref[...])

  q_out[...] = (_dot(xn, wq_ref[...]) * (A_HEAD_DIM ** -0.5 * LOG2E)).astype(BF16)
  k_out[...] = _dot(xn, wk_ref[...]).astype(BF16)
  v_out[...] = _dot(xn, wv_ref[...]).astype(BF16)
  ga_out[...] = _dot(xn, wga_ref[...])
  gb_out[...] = _dot(xn, wgb_ref[...])

  if make_tables:
    cos, sin = _rope_tile(pos_ref[0], invf_ref[...])
    cos_out[...] = cos
    sin_out[...] = sin
  else:
    cos = cos_ref[...]
    sin = sin_ref[...]
  kr0 = Q_LORA + KV_LORA
  k_rope = (zm[:, kr0:kr0 + LANES] * cos + zm[:, kr0 + LANES:kr0 + 2 * LANES] * sin).astype(BF16)
  ones = jnp.ones((h_ref.shape[0], B_QW - B_VDIM), BF16)
  for h in range(B_HEADS):
    qb = 3 * LANES * h
    q_rope = qall[:, qb + LANES:qb + 2 * LANES] * cos + qall[:, qb + 2 * LANES:qb + 3 * LANES] * sin
    qb_out[:, B_QW * h:B_QW * h + B_NOPE] = qall[:, qb:qb + LANES].astype(BF16)
    qb_out[:, B_QW * h + B_NOPE:B_QW * (h + 1)] = q_rope.astype(BF16)
    kb_out[:, B_QW * h:B_QW * h + B_NOPE] = kv[:, 2 * LANES * h:2 * LANES * h + B_NOPE].astype(BF16)
    kb_out[:, B_QW * h + B_NOPE:B_QW * (h + 1)] = k_rope
    vb_out[:, B_QW * h:B_QW * h + B_VDIM] = kv[:, 2 * LANES * h + B_NOPE:2 * LANES * (h + 1)].astype(BF16)
    vb_out[:, B_QW * h + B_VDIM:B_QW * (h + 1)] = ones


def _layer_spec(a, layer):
  return pl.BlockSpec((None,) + a.shape[1:], lambda i: (layer,) + (0,) * (a.ndim - 1))


def _in_proj(h, w, layer, positions=None, tables=None):
  n = h.shape[0]
  t = TOKEN_TILE
  tok = lambda width: pl.BlockSpec((t, width), lambda i: (i, 0))
  w_in_block = lambda j: pl.BlockSpec((None, D_MODEL, A_WIDTH), lambda i: (layer, 0, j))
  small = [w["gmix"], w["wgb"], w["wmla"], w["gq"], w["gkv"], w["wuq"], w["wukv"]]
  in_specs = ([tok(D_MODEL), _layer_spec(small[0], layer)] + [w_in_block(j) for j in range(4)]
              + [_layer_spec(a, layer) for a in small[1:]])
  operands = [h, small[0]] + [w["w_qkvg"]] * 4 + small[1:]
  out_widths = [(A_WIDTH, BF16), (A_WIDTH, BF16), (A_WIDTH, BF16), (A_WIDTH, F32), (B_WIDTH, F32),
                (B_HEADS * B_QW, BF16), (B_HEADS * B_QW, BF16), (B_HEADS * B_QW, BF16)]
  make_tables = tables is None
  if make_tables:
    invf = _inv_freq_column()
    operands += [positions.reshape(n // t, 1, t), invf]
    in_specs += [pl.BlockSpec((1, 1, t), lambda i: (i, 0, 0)), pl.BlockSpec(invf.shape, lambda i: (0, 0))]
    out_widths += [(LANES, F32), (LANES, F32)]
  else:
    operands += list(tables)
    in_specs += [tok(LANES), tok(LANES)]
  return pl.pallas_call(
      functools.partial(_in_proj_kernel, make_tables=make_tables),
      grid=(n // t,),
      in_specs=in_specs,
      out_specs=[tok(wd) for wd, _ in out_widths],
      out_shape=[jax.ShapeDtypeStruct((n, wd), dt) for wd, dt in out_widths],
      compiler_params=pltpu.CompilerParams(dimension_semantics=("parallel",),
                                           vmem_limit_bytes=VMEM_LIMIT),
      name="in_proj",
  )(*operands)


def _attn_a_kernel(q_ref, k_ref, v_ref, gate_ref, tab_ref, o_ref):
  seq_len = q_ref.shape[1]
  lane = lax.broadcasted_iota(jnp.int32, (1, LANES), 1)
  head_rows = ((lane < A_HEAD_DIM).astype(BF16), (lane >= A_HEAD_DIM).astype(BF16))
  lo_mask = lax.broadcasted_iota(jnp.int32, (Q_BLOCK, LANES), 1) < A_HEAD_DIM
  ones = jnp.ones((A_WINDOW, LANES), BF16)

  def window(qb):
    k0 = max(0, Q_BLOCK * qb - A_LEFT)
    return k0, Q_BLOCK * (qb + 1) - k0

  def scores(qb, pr, hh):
    k0, nk = window(qb)
    ls = slice(LANES * pr, LANES * (pr + 1))
    qp = q_ref[0, Q_BLOCK * qb:Q_BLOCK * (qb + 1), ls]
    return (_dot_nt(qp * head_rows[hh], k_ref[0, k0:k0 + nk, ls])
            + tab_ref[2 * pr + hh, :, A_WINDOW - nk:])

  def weighted(qb, pr, s):
    k0, nk = window(qb)
    vp = jnp.concatenate([v_ref[0, k0:k0 + nk, LANES * pr:LANES * (pr + 1)], ones[:nk]], axis=1)
    e = jnp.exp2(s - jnp.max(s, axis=-1, keepdims=True))
    pv = _dot(e.astype(BF16), vp)
    return pv[:, :LANES] / pv[:, LANES:]

  items = [(qb, pr, hh) for qb in reversed(range(seq_len // Q_BLOCK))
           for pr in range(A_STEP_PAIRS) for hh in range(2)]
  depth = 2
  pending = [scores(*it) for it in items[:depth]]
  outs = []
  for i, (qb, pr, hh) in enumerate(items):
    if i + depth < len(items):
      pending.append(scores(*items[i + depth]))
    outs.append(weighted(qb, pr, pending.pop(0)))
    if hh == 1:
      rows = slice(Q_BLOCK * qb, Q_BLOCK * (qb + 1))
      ls = slice(LANES * pr, LANES * (pr + 1))
      o = jnp.where(lo_mask, outs[0], outs[1])
      o_ref[0, rows, ls] = (o * _silu(gate_ref[0, rows, ls])).astype(BF16)
      outs = []


def _attn_a(q, k, v, gate, tab):
  b, s, _ = q.shape
  pair = pl.BlockSpec((1, s, A_STEP_PAIRS * LANES), lambda p, i: (i, 0, p))
  return pl.pallas_call(
      _attn_a_kernel,
      grid=(A_HEADS // (2 * A_STEP_PAIRS), b),
      in_specs=[pair, pair, pair, pair,
                pl.BlockSpec((2 * A_STEP_PAIRS, Q_BLOCK, A_WINDOW), lambda p, i: (p, 0, 0))],
      out_specs=pair,
      out_shape=jax.ShapeDtypeStruct((b, s, A_WIDTH), BF16),
      compiler_params=pltpu.CompilerParams(dimension_semantics=("parallel", "parallel"),
                                           vmem_limit_bytes=VMEM_LIMIT),
      name="attn_a",
  )(q, k, v, gate, tab)


def _attn_b_kernel(q_ref, k_ref, v_ref, gate_ref, o_ref):
  seq_len = q_ref.shape[1]
  c = (B_NOPE + B_ROPE) ** -0.5 * LOG2E
  qi = lax.broadcasted_iota(jnp.int32, (Q_BLOCK, Q_BLOCK), 0)
  kj = lax.broadcasted_iota(jnp.int32, (Q_BLOCK, Q_BLOCK), 1)
  diag_ok = (kj // CHUNK) <= (qi // CHUNK)

  def scores(qb, h):
    rows = slice(Q_BLOCK * qb, Q_BLOCK * (qb + 1))
    wide = slice(B_QW * h, B_QW * (h + 1))
    qh = q_ref[0, rows, wide]
    s_d = jnp.where(diag_ok, _dot_nt(qh, k_ref[0, rows, wide]), -jnp.inf)
    s_p = _dot_nt(qh, k_ref[0, :Q_BLOCK * qb, wide]) if qb else None
    return s_d, s_p

  def finish(qb, h, s_d, s_p):
    rows = slice(Q_BLOCK * qb, Q_BLOCK * (qb + 1))
    wide = slice(B_QW * h, B_QW * (h + 1))
    narrow = slice(B_VDIM * h, B_VDIM * (h + 1))
    m = jnp.max(s_d, axis=-1, keepdims=True)
    if qb:
      m = jnp.maximum(m, jnp.max(s_p, axis=-1, keepdims=True))
    pv = _dot(jnp.exp2((s_d - m) * c).astype(BF16), v_ref[0, rows, wide])
    if qb:
      pv = pv + _dot(jnp.exp2((s_p - m) * c).astype(BF16), v_ref[0, :Q_BLOCK * qb, wide])
    o = pv[:, :B_VDIM] / pv[:, B_VDIM:]
    o_ref[0, rows, narrow] = (o * _silu(gate_ref[0, rows, narrow])).astype(BF16)

  items = [(qb, h) for qb in reversed(range(seq_len // Q_BLOCK)) for h in range(B_STEP_HEADS)]
  depth = 2
  pending = [scores(*it) for it in items[:depth]]
  for i, it in enumerate(items):
    if i + depth < len(items):
      pending.append(scores(*items[i + depth]))
    finish(*it, *pending.pop(0))


def _attn_b(q, k, v, gate):
  b, s, _ = q.shape
  wide = pl.BlockSpec((1, s, B_STEP_HEADS * B_QW), lambda i, h: (i, 0, h))
  narrow = pl.BlockSpec((1, s, B_STEP_HEADS * B_VDIM), lambda i, h: (i, 0, h))
  return pl.pallas_call(
      _attn_b_kernel,
      grid=(b, B_HEADS // B_STEP_HEADS),
      in_specs=[wide, wide, wide, narrow],
      out_specs=narrow,
      out_shape=jax.ShapeDtypeStruct((b, s, B_WIDTH), BF16),
      compiler_params=pltpu.CompilerParams(dimension_semantics=("parallel", "parallel"),
                                           vmem_limit_bytes=VMEM_LIMIT),
      name="attn_b",
  )(q, k, v, gate)


def _out_ple_kernel(ma_ref, mb_ref, h_ref, p_ref, woa_ref, wob_ref, gple_ref, wpg_ref, bpg_ref,
                    wpe_ref, gfin_ref, o_ref, *, final_norm):
  n_sub = 2
  sub = h_ref.shape[0] // n_sub

  def stage1(j):
    rs = slice(sub * j, sub * (j + 1))
    h1 = h_ref[rs, :] + (_dot(ma_ref[rs, :], woa_ref[...]) + _dot(mb_ref[rs, :], wob_ref[...]))
    pe = _dot(p_ref[rs, :].astype(BF16), wpe_ref[...])
    return h1, pe

  def stage2(j, h1, pe):
    rs = slice(sub * j, sub * (j + 1))
    xn = _rms(h1, gple_ref[...]).astype(BF16)
    z = _dot(xn, wpg_ref[...]) + bpg_ref[...]
    h2 = h1 + pe * (1.0 / (1.0 + jnp.exp(-z)))
    o_ref[rs, :] = _rms(h2, gfin_ref[...]) if final_norm else h2

  pending = stage1(0)
  for j in range(n_sub):
    nxt = stage1(j + 1) if j + 1 < n_sub else None
    stage2(j, *pending)
    pending = nxt


def _out_ple(ma, mb, h, p, w, layer, final_norm):
  n = h.shape[0]
  t = OUT_TOKEN_TILE
  tok = lambda width: pl.BlockSpec((t, width), lambda i: (i, 0))
  w_out_half = lambda j: pl.BlockSpec((None, A_WIDTH, D_MODEL), lambda i: (layer, j, 0))
  rest = [w["gple"], w["wpg"], w["bpg"], w["wpe"]]
  return pl.pallas_call(
      functools.partial(_out_ple_kernel, final_norm=final_norm),
      grid=(n // t,),
      in_specs=([tok(A_WIDTH), tok(B_WIDTH), tok(D_MODEL),
                 pl.BlockSpec((None, t, D_PLE), lambda i: (layer, i, 0)),
                 w_out_half(0), w_out_half(1)] + [_layer_spec(a, layer) for a in rest]
                + [pl.BlockSpec(w["gfin"].shape, lambda i: (0, 0))]),
      out_specs=tok(D_MODEL),
      out_shape=jax.ShapeDtypeStruct((n, D_MODEL), F32),
      compiler_params=pltpu.CompilerParams(dimension_semantics=("parallel",),
                                           vmem_limit_bytes=VMEM_LIMIT),
      name="out_ple",
  )(ma, mb, h, p, w["w_out"], w["w_out"], *rest, w["gfin"])


def _prep_params(norm_mix, w_in, g_q, w_uq, g_kv, w_ukv, w_out, norm_ple, w_pe, w_pg, b_pg,
                 norm_final):
  depth = w_in.shape[0]
  half = B_ROPE // 2
  mla0 = 4 * A_WIDTH
  kr0 = mla0 + Q_LORA + KV_LORA
  kr1, kr2 = w_in[..., kr0:kr0 + half], w_in[..., kr0 + half:kr0 + B_ROPE]
  pad = jnp.zeros((depth, D_MODEL, LANES - B_ROPE), F32)
  wmla = jnp.concatenate([w_in[..., mla0:kr0], kr1, kr2, pad, kr2, kr1, pad], axis=-1)
  uq = w_uq.reshape(depth, Q_LORA, B_HEADS, B_NOPE + B_ROPE)
  r1, r2 = uq[..., B_NOPE:B_NOPE + half], uq[..., B_NOPE + half:]
  zpad = jnp.zeros((depth, Q_LORA, B_HEADS, LANES - B_ROPE), F32)
  wuq = jnp.concatenate([uq[..., :B_NOPE], r1, r2, zpad, r2, r1, zpad], axis=-1)
  return {
      "gmix": norm_mix[:, None, :],
      "w_qkvg": w_in[..., :mla0].astype(BF16),
      "wgb": w_in[..., kr0 + B_ROPE:].astype(BF16),
      "wmla": wmla.astype(BF16),
      "gq": g_q[:, None, :], "gkv": g_kv[:, None, :],
      "wuq": wuq.reshape(depth, Q_LORA, B_HEADS * 3 * LANES).astype(BF16),
      "wukv": w_ukv.astype(BF16),
      "w_out": w_out.astype(BF16),
      "gple": norm_ple[:, None, :], "wpg": w_pg.astype(BF16), "bpg": b_pg[:, None, :],
      "wpe": w_pe.astype(BF16),
      "gfin": norm_final[None, :],
  }


def kernel(x, p, positions, norm_mix, w_in, rel_bias, g_q, w_uq, g_kv, w_ukv, w_out,
           norm_ple, w_pe, w_pg, b_pg, norm_final):
  b, s, d = x.shape
  depth = w_in.shape[0]
  n = b * s
  w = _prep_params(norm_mix, w_in, g_q, w_uq, g_kv, w_ukv, w_out, norm_ple, w_pe, w_pg, b_pg,
                   norm_final)
  p = p.reshape(depth, n, D_PLE)
  h = x.reshape(n, d)
  seq = lambda a: a.reshape(b, s, a.shape[-1])
  tables = None
  for i in range(depth):
    tab = _bias_tables(rel_bias[i])
    if tables is None:
      *outs, cos, sin = _in_proj(h, w, i, positions=positions)
      tables = (cos, sin)
    else:
      outs = _in_proj(h, w, i, tables=tables)
    qa, ka, va, ga, gb, qb, kb, vb = outs
    ma = _attn_a(seq(qa), seq(ka), seq(va), seq(ga), tab)
    mb = _attn_b(seq(qb), seq(kb), seq(vb), seq(gb))
    h = _out_ple(ma.reshape(n, A_WIDTH), mb.reshape(n, B_WIDTH), h, p, w, i,
                 final_norm=(i == depth - 1))
  return h.reshape(b, s, d)
```

```python
import functools
import math

import jax
import jax.numpy as jnp
from jax import lax
from jax.experimental import pallas as pl
from jax.experimental.pallas import tpu as pltpu

D_MODEL = 1024
CHUNK = 64
D_PLE = 256
EPS = 1e-6
A_HEADS = 8
A_HEAD_DIM = 64
A_WIDTH = A_HEADS * A_HEAD_DIM
A_LEFT_CHUNKS = 8
REL_MAX = 128
B_HEADS = 4
B_NOPE = 128
B_ROPE = 64
B_VDIM = 128
B_WIDTH = B_HEADS * B_VDIM
Q_LORA = 256
KV_LORA = 128
ROPE_THETA = 10000.0

LANES = 128
TOKEN_TILE = 1024
OUT_TOKEN_TILE = 1024
Q_BLOCK = 256
A_STEP_PAIRS = 2
B_STEP_HEADS = 2
A_LEFT = A_LEFT_CHUNKS * CHUNK
A_WINDOW = A_LEFT + Q_BLOCK
B_QW = B_NOPE + LANES
NEG_BIG = -1e30
LOG2E = math.log2(math.e)
VMEM_LIMIT = 56 * 1024 * 1024

BF16 = jnp.bfloat16
F32 = jnp.float32


def _dot(a, b):
  return jnp.dot(a, b, preferred_element_type=F32)


def _dot_nt(a, b):
  return lax.dot_general(a, b, (((1,), (1,)), ((), ())), preferred_element_type=F32)


def _rms(x, g):
  y = x * lax.rsqrt(jnp.mean(x * x, axis=-1, keepdims=True) + EPS)
  return y * g


def _silu(g):
  return g * (1.0 / (1.0 + jnp.exp(-g)))


def _inv_freq_column():
  inv_freq = ROPE_THETA ** (-jnp.arange(0, B_ROPE, 2, dtype=F32) / B_ROPE)
  return inv_freq[:, None]


def _rope_tile(pos_row, inv_freq_col):
  t = pos_row.shape[1]
  ang = pos_row.astype(F32) * inv_freq_col
  c, s = jnp.cos(ang), jnp.sin(ang)
  zeros = jnp.zeros((LANES - B_ROPE, t), F32)
  return jnp.concatenate([c, c, zeros], axis=0).T, jnp.concatenate([-s, s, zeros], axis=0).T


G_LEN = 1024


def _bias_table_kernel(g_ref, tab_ref):
  g = jnp.broadcast_to(g_ref[0], (Q_BLOCK, G_LEN))
  t = pltpu.roll(g, G_LEN - Q_BLOCK, 1, stride=1, stride_axis=0)[:, :A_WINDOW]
  qi = lax.broadcasted_iota(jnp.int32, (Q_BLOCK, A_WINDOW), 0)
  kk = lax.broadcasted_iota(jnp.int32, (Q_BLOCK, A_WINDOW), 1)
  qc = (qi + A_LEFT) // CHUNK
  kc = kk // CHUNK
  valid = (kc <= qc) & (kc >= qc - A_LEFT_CHUNKS)
  tab_ref[0] = jnp.where(valid, t * LOG2E, NEG_BIG)


def _bias_tables(rel_bias):
  flat = A_LEFT + Q_BLOCK - REL_MAX
  rev = rel_bias[:, ::-1]
  g = jnp.concatenate([
      jnp.broadcast_to(rel_bias[:, -1:], (A_HEADS, flat)),
      rev[:, : 2 * REL_MAX],
      jnp.broadcast_to(rel_bias[:, :1], (A_HEADS, G_LEN - flat - 2 * REL_MAX)),
  ], axis=1)[:, None, :]
  return pl.pallas_call(
      _bias_table_kernel,
      grid=(A_HEADS,),
      in_specs=[pl.BlockSpec((1, 1, G_LEN), lambda h: (h, 0, 0))],
      out_specs=pl.BlockSpec((1, Q_BLOCK, A_WINDOW), lambda h: (h, 0, 0)),
      out_shape=jax.ShapeDtypeStruct((A_HEADS, Q_BLOCK, A_WINDOW), F32),
      compiler_params=pltpu.CompilerParams(dimension_semantics=("parallel",)),
      name="bias_table",
  )(g)


def _in_proj_kernel(h_ref, gmix_ref, wq_ref, wk_ref, wv_ref, wga_ref, wgb_ref, wmla_ref,
                    gq_ref, gkv_ref, wuq_ref, wukv_ref, *refs, make_tables):
  if make_tables:
    (pos_ref, invf_ref, q_out, k_out, v_out, ga_out, gb_out, qb_out, kb_out, vb_out,
     cos_out, sin_out) = refs
  else:
    cos_ref, sin_ref, q_out, k_out, v_out, ga_out, gb_out, qb_out, kb_out, vb_out = refs
  xn = _rms(h_ref[...], gmix_ref[...]).astype(BF16)
  zm = _dot(xn, wmla_ref[...])
  cqn = _rms(zm[:, :Q_LORA], gq_ref[...]).astype(BF16)
  ckvn = _rms(zm[:, Q_LORA:Q_LORA + KV_LORA], gkv_ref[...]).astype(BF16)
  qall = _dot(cqn, wuq_ref[...])
  kv = _dot(ckvn, wukv_ref[...])

  q_out[...] = (_dot(xn, wq_ref[...]) * (A_HEAD_DIM ** -0.5 * LOG2E)).astype(BF16)
  k_out[...] = _dot(xn, wk_ref[...]).astype(BF16)
  v_out[...] = _dot(xn, wv_ref[...]).astype(BF16)
  ga_out[...] = _dot(xn, wga_ref[...])
  gb_out[...] = _dot(xn, wgb_ref[...])

  if make_tables:
    cos, sin = _rope_tile(pos_ref[0], invf_ref[...])
    cos_out[...] = cos
    sin_out[...] = sin
  else:
    cos = cos_ref[...]
    sin = sin_ref[...]
  first_half = lax.broadcasted_iota(jnp.int32, (h_ref.shape[0], LANES), 1) < B_ROPE // 2

  def rope(g):
    swapped = jnp.where(first_half, pltpu.roll(g, LANES - B_ROPE // 2, 1), pltpu.roll(g, B_ROPE // 2, 1))
    return g * cos + swapped * sin

  k_rope = rope(zm[:, Q_LORA + KV_LORA:]).astype(BF16)
  ones = jnp.ones((h_ref.shape[0], B_QW - B_VDIM), BF16)
  for h in range(B_HEADS):
    qb = B_QW * h
    qb_out[:, qb:qb + B_NOPE] = qall[:, qb:qb + B_NOPE].astype(BF16)
    qb_out[:, qb + B_NOPE:qb + B_QW] = rope(qall[:, qb + B_NOPE:qb + B_QW]).astype(BF16)
    kb_out[:, B_QW * h:B_QW * h + B_NOPE] = kv[:, 2 * LANES * h:2 * LANES * h + B_NOPE].astype(BF16)
    kb_out[:, B_QW * h + B_NOPE:B_QW * (h + 1)] = k_rope
    vb_out[:, B_QW * h:B_QW * h + B_VDIM] = kv[:, 2 * LANES * h + B_NOPE:2 * LANES * (h + 1)].astype(BF16)
    vb_out[:, B_QW * h + B_VDIM:B_QW * (h + 1)] = ones


def _layer_spec(a, layer):
  return pl.BlockSpec((None,) + a.shape[1:], lambda i: (layer,) + (0,) * (a.ndim - 1),
                      pipeline_mode=pl.Buffered(1))


def _in_proj(h, w, layer, positions=None, tables=None):
  n = h.shape[0]
  t = TOKEN_TILE
  tok = lambda width: pl.BlockSpec((t, width), lambda i: (i, 0))
  w_in_block = lambda j: pl.BlockSpec((None, D_MODEL, A_WIDTH), lambda i: (layer, 0, j),
                                      pipeline_mode=pl.Buffered(1))
  small = [w["gmix"], w["wgb"], w["wmla"], w["gq"], w["gkv"], w["wuq"], w["wukv"]]
  in_specs = ([tok(D_MODEL), _layer_spec(small[0], layer)] + [w_in_block(j) for j in range(4)]
              + [_layer_spec(a, layer) for a in small[1:]])
  operands = [h, small[0]] + [w["w_qkvg"]] * 4 + small[1:]
  out_widths = [(A_WIDTH, BF16), (A_WIDTH, BF16), (A_WIDTH, BF16), (A_WIDTH, F32), (B_WIDTH, F32),
                (B_HEADS * B_QW, BF16), (B_HEADS * B_QW, BF16), (B_HEADS * B_QW, BF16)]
  make_tables = tables is None
  if make_tables:
    invf = _inv_freq_column()
    operands += [positions.reshape(n // t, 1, t), invf]
    in_specs += [pl.BlockSpec((1, 1, t), lambda i: (i, 0, 0)), pl.BlockSpec(invf.shape, lambda i: (0, 0))]
    out_widths += [(LANES, F32), (LANES, F32)]
  else:
    operands += list(tables)
    in_specs += [tok(LANES), tok(LANES)]
  return pl.pallas_call(
      functools.partial(_in_proj_kernel, make_tables=make_tables),
      grid=(n // t,),
      in_specs=in_specs,
      out_specs=[tok(wd) for wd, _ in out_widths],
      out_shape=[jax.ShapeDtypeStruct((n, wd), dt) for wd, dt in out_widths],
      compiler_params=pltpu.CompilerParams(dimension_semantics=("parallel",),
                                           vmem_limit_bytes=VMEM_LIMIT),
      name="in_proj",
  )(*operands)


def _attn_a_kernel(q_ref, k_ref, v_ref, gate_ref, tab_ref, o_ref):
  seq_len = q_ref.shape[1]
  lane = lax.broadcasted_iota(jnp.int32, (1, LANES), 1)
  head_rows = ((lane < A_HEAD_DIM).astype(BF16), (lane >= A_HEAD_DIM).astype(BF16))
  lo_mask = lax.broadcasted_iota(jnp.int32, (Q_BLOCK, LANES), 1) < A_HEAD_DIM
  ones = jnp.ones((A_WINDOW, LANES), BF16)

  def window(qb):
    k0 = max(0, Q_BLOCK * qb - A_LEFT)
    return k0, Q_BLOCK * (qb + 1) - k0

  def scores(qb, pr, hh):
    k0, nk = window(qb)
    ls = slice(LANES * pr, LANES * (pr + 1))
    qp = q_ref[0, Q_BLOCK * qb:Q_BLOCK * (qb + 1), ls]
    return (_dot_nt(qp * head_rows[hh], k_ref[0, k0:k0 + nk, ls])
            + tab_ref[2 * pr + hh, :, A_WINDOW - nk:])

  def weighted(qb, pr, s):
    k0, nk = window(qb)
    vp = jnp.concatenate([v_ref[0, k0:k0 + nk, LANES * pr:LANES * (pr + 1)], ones[:nk]], axis=1)
    e = jnp.exp2(s - jnp.max(s, axis=-1, keepdims=True))
    pv = _dot(e.astype(BF16), vp)
    return pv[:, :LANES] / pv[:, LANES:]

  items = [(qb, pr, hh) for qb in reversed(range(seq_len // Q_BLOCK))
           for pr in range(A_STEP_PAIRS) for hh in range(2)]
  depth = 2
  pending = [scores(*it) for it in items[:depth]]
  outs = []
  for i, (qb, pr, hh) in enumerate(items):
    if i + depth < len(items):
      pending.append(scores(*items[i + depth]))
    outs.append(weighted(qb, pr, pending.pop(0)))
    if hh == 1:
      rows = slice(Q_BLOCK * qb, Q_BLOCK * (qb + 1))
      ls = slice(LANES * pr, LANES * (pr + 1))
      o = jnp.where(lo_mask, outs[0], outs[1])
      o_ref[0, rows, ls] = (o * _silu(gate_ref[0, rows, ls])).astype(BF16)
      outs = []


def _attn_a(q, k, v, gate, tab):
  b, s, _ = q.shape
  pair = pl.BlockSpec((1, s, A_STEP_PAIRS * LANES), lambda p, i: (i, 0, p))
  return pl.pallas_call(
      _attn_a_kernel,
      grid=(A_HEADS // (2 * A_STEP_PAIRS), b),
      in_specs=[pair, pair, pair, pair,
                pl.BlockSpec((2 * A_STEP_PAIRS, Q_BLOCK, A_WINDOW), lambda p, i: (p, 0, 0))],
      out_specs=pair,
      out_shape=jax.ShapeDtypeStruct((b, s, A_WIDTH), BF16),
      compiler_params=pltpu.CompilerParams(dimension_semantics=("parallel", "parallel"),
                                           vmem_limit_bytes=VMEM_LIMIT),
      name="attn_a",
  )(q, k, v, gate, tab)


def _attn_b_kernel(q_ref, k_ref, v_ref, gate_ref, o_ref):
  seq_len = q_ref.shape[1]
  c = (B_NOPE + B_ROPE) ** -0.5 * LOG2E
  qi = lax.broadcasted_iota(jnp.int32, (Q_BLOCK, Q_BLOCK), 0)
  kj = lax.broadcasted_iota(jnp.int32, (Q_BLOCK, Q_BLOCK), 1)
  diag_ok = (kj // CHUNK) <= (qi // CHUNK)

  def scores(qb, h):
    rows = slice(Q_BLOCK * qb, Q_BLOCK * (qb + 1))
    wide = slice(B_QW * h, B_QW * (h + 1))
    qh = q_ref[0, rows, wide]
    s_d = jnp.where(diag_ok, _dot_nt(qh, k_ref[0, rows, wide]), -jnp.inf)
    s_p = _dot_nt(qh, k_ref[0, :Q_BLOCK * qb, wide]) if qb else None
    return s_d, s_p

  def finish(qb, h, s_d, s_p):
    rows = slice(Q_BLOCK * qb, Q_BLOCK * (qb + 1))
    wide = slice(B_QW * h, B_QW * (h + 1))
    narrow = slice(B_VDIM * h, B_VDIM * (h + 1))
    m = jnp.max(s_d, axis=-1, keepdims=True)
    if qb:
      m = jnp.maximum(m, jnp.max(s_p, axis=-1, keepdims=True))
    pv = _dot(jnp.exp2((s_d - m) * c).astype(BF16), v_ref[0, rows, wide])
    if qb:
      pv = pv + _dot(jnp.exp2((s_p - m) * c).astype(BF16), v_ref[0, :Q_BLOCK * qb, wide])
    o = pv[:, :B_VDIM] / pv[:, B_VDIM:]
    o_ref[0, rows, narrow] = (o * _silu(gate_ref[0, rows, narrow])).astype(BF16)

  items = [(qb, h) for qb in reversed(range(seq_len // Q_BLOCK)) for h in range(B_STEP_HEADS)]
  depth = 2
  pending = [scores(*it) for it in items[:depth]]
  for i, it in enumerate(items):
    if i + depth < len(items):
      pending.append(scores(*items[i + depth]))
    finish(*it, *pending.pop(0))


def _attn_b(q, k, v, gate):
  b, s, _ = q.shape
  wide = pl.BlockSpec((1, s, B_STEP_HEADS * B_QW), lambda i, h: (i, 0, h))
  narrow = pl.BlockSpec((1, s, B_STEP_HEADS * B_VDIM), lambda i, h: (i, 0, h))
  return pl.pallas_call(
      _attn_b_kernel,
      grid=(b, B_HEADS // B_STEP_HEADS),
      in_specs=[wide, wide, wide, narrow],
      out_specs=narrow,
      out_shape=jax.ShapeDtypeStruct((b, s, B_WIDTH), BF16),
      compiler_params=pltpu.CompilerParams(dimension_semantics=("parallel", "parallel"),
                                           vmem_limit_bytes=VMEM_LIMIT),
      name="attn_b",
  )(q, k, v, gate)


def _out_ple_kernel(ma_ref, mb_ref, h_ref, p_ref, woa_ref, wob_ref, gple_ref, wpg_ref, bpg_ref,
                    wpe_ref, gfin_ref, o_ref, *, final_norm):
  n_sub = 2
  sub = h_ref.shape[0] // n_sub

  def stage1(j):
    rs = slice(sub * j, sub * (j + 1))
    h1 = h_ref[rs, :] + (_dot(ma_ref[rs, :], woa_ref[...]) + _dot(mb_ref[rs, :], wob_ref[...]))
    pe = _dot(p_ref[rs, :].astype(BF16), wpe_ref[...])
    return h1, pe

  def stage2(j, h1, pe):
    rs = slice(sub * j, sub * (j + 1))
    xn = _rms(h1, gple_ref[...]).astype(BF16)
    z = _dot(xn, wpg_ref[...]) + bpg_ref[...]
    h2 = h1 + pe * (1.0 / (1.0 + jnp.exp(-z)))
    o_ref[rs, :] = _rms(h2, gfin_ref[...]) if final_norm else h2

  pending = stage1(0)
  for j in range(n_sub):
    nxt = stage1(j + 1) if j + 1 < n_sub else None
    stage2(j, *pending)
    pending = nxt


def _out_ple(ma, mb, h, p, w, layer, final_norm):
  n = h.shape[0]
  t = OUT_TOKEN_TILE
  tok = lambda width: pl.BlockSpec((t, width), lambda i: (i, 0))
  w_out_half = lambda j: pl.BlockSpec((None, A_WIDTH, D_MODEL), lambda i: (layer, j, 0))
  rest = [w["gple"], w["wpg"], w["bpg"], w["wpe"]]
  return pl.pallas_call(
      functools.partial(_out_ple_kernel, final_norm=final_norm),
      grid=(n // t,),
      in_specs=([tok(A_WIDTH), tok(B_WIDTH), tok(D_MODEL),
                 pl.BlockSpec((None, t, D_PLE), lambda i: (layer, i, 0)),
                 w_out_half(0), w_out_half(1)] + [_layer_spec(a, layer) for a in rest]
                + [pl.BlockSpec(w["gfin"].shape, lambda i: (0, 0))]),
      out_specs=tok(D_MODEL),
      out_shape=jax.ShapeDtypeStruct((n, D_MODEL), F32),
      compiler_params=pltpu.CompilerParams(dimension_semantics=("parallel",),
                                           vmem_limit_bytes=VMEM_LIMIT),
      name="out_ple",
  )(ma, mb, h, p, w["w_out"], w["w_out"], *rest, w["gfin"])


def _prep_params(norm_mix, w_in, g_q, w_uq, g_kv, w_ukv, w_out, norm_ple, w_pe, w_pg, b_pg,
                 norm_final):
  depth = w_in.shape[0]
  mla0 = 4 * A_WIDTH
  kr0 = mla0 + Q_LORA + KV_LORA
  pad = jnp.zeros((depth, D_MODEL, LANES - B_ROPE), F32)
  wmla = jnp.concatenate([w_in[..., mla0:kr0 + B_ROPE], pad], axis=-1)
  uq = w_uq.reshape(depth, Q_LORA, B_HEADS, B_NOPE + B_ROPE)
  zpad = jnp.zeros((depth, Q_LORA, B_HEADS, LANES - B_ROPE), F32)
  wuq = jnp.concatenate([uq, zpad], axis=-1)
  return {
      "gmix": norm_mix[:, None, :],
      "w_qkvg": w_in[..., :mla0].astype(BF16),
      "wgb": w_in[..., kr0 + B_ROPE:].astype(BF16),
      "wmla": wmla.astype(BF16),
      "gq": g_q[:, None, :], "gkv": g_kv[:, None, :],
      "wuq": wuq.reshape(depth, Q_LORA, B_HEADS * B_QW).astype(BF16),
      "wukv": w_ukv.astype(BF16),
      "w_out": w_out.astype(BF16),
      "gple": norm_ple[:, None, :], "wpg": w_pg.astype(BF16), "bpg": b_pg[:, None, :],
      "wpe": w_pe.astype(BF16),
      "gfin": norm_final[None, :],
  }


def kernel(x, p, positions, norm_mix, w_in, rel_bias, g_q, w_uq, g_kv, w_ukv, w_out,
           norm_ple, w_pe, w_pg, b_pg, norm_final):
  b, s, d = x.shape
  depth = w_in.shape[0]
  n = b * s
  w = _prep_params(norm_mix, w_in, g_q, w_uq, g_kv, w_ukv, w_out, norm_ple, w_pe, w_pg, b_pg,
                   norm_final)
  p = p.reshape(depth, n, D_PLE)
  h = x.reshape(n, d)
  seq = lambda a: a.reshape(b, s, a.shape[-1])
  tables = None
  for i in range(depth):
    tab = _bias_tables(rel_bias[i])
    if tables is None:
      *outs, cos, sin = _in_proj(h, w, i, positions=positions)
      tables = (cos, sin)
    else:
      outs = _in_proj(h, w, i, tables=tables)
    qa, ka, va, ga, gb, qb, kb, vb = outs
    ma = _attn_a(seq(qa), seq(ka), seq(va), seq(ga), tab)
    mb = _attn_b(seq(qb), seq(kb), seq(vb), seq(gb))
    h = _out_ple(ma.reshape(n, A_WIDTH), mb.reshape(n, B_WIDTH), h, p, w, i,
                 final_norm=(i == depth - 1))
  return h.reshape(b, s, d)
```

```python
import functools
import math

import jax
import jax.numpy as jnp
from jax import lax
from jax.experimental import pallas as pl
from jax.experimental.pallas import tpu as pltpu

D_MODEL = 1024
CHUNK = 64
D_PLE = 256
EPS = 1e-6
A_HEADS = 8
A_HEAD_DIM = 64
A_WIDTH = A_HEADS * A_HEAD_DIM
A_LEFT_CHUNKS = 8
REL_MAX = 128
B_HEADS = 4
B_NOPE = 128
B_ROPE = 64
B_VDIM = 128
B_WIDTH = B_HEADS * B_VDIM
Q_LORA = 256
KV_LORA = 128
ROPE_THETA = 10000.0

LANES = 128
TOKEN_TILE = 1024
OUT_TOKEN_TILE = 1024
Q_BLOCK = 256
A_STEP_PAIRS = 2
B_STEP_HEADS = 2
A_LEFT = A_LEFT_CHUNKS * CHUNK
A_WINDOW = A_LEFT + Q_BLOCK
B_QW = B_NOPE + LANES
NEG_BIG = -1e30
LOG2E = math.log2(math.e)
B_QSCALE = (B_NOPE + B_ROPE) ** -0.5 * LOG2E
VMEM_LIMIT = 56 * 1024 * 1024

BF16 = jnp.bfloat16
F32 = jnp.float32


def _dot(a, b):
  return jnp.dot(a, b, preferred_element_type=F32)


def _dot_nt(a, b):
  return lax.dot_general(a, b, (((1,), (1,)), ((), ())), preferred_element_type=F32)


def _rms(x, g):
  y = x * lax.rsqrt(jnp.mean(x * x, axis=-1, keepdims=True) + EPS)
  return y * g


def _silu(g):
  return g * (1.0 / (1.0 + jnp.exp(-g)))


def _inv_freq_column():
  inv_freq = ROPE_THETA ** (-jnp.arange(0, B_ROPE, 2, dtype=F32) / B_ROPE)
  return inv_freq[:, None]


def _rope_tile(pos_row, inv_freq_col):
  t = pos_row.shape[1]
  ang = pos_row.astype(F32) * inv_freq_col
  c, s = jnp.cos(ang), jnp.sin(ang)
  zeros = jnp.zeros((LANES - B_ROPE, t), F32)
  return jnp.concatenate([c, c, zeros], axis=0).T, jnp.concatenate([-s, s, zeros], axis=0).T


G_LEN = 1024


def _bias_table_kernel(g_ref, tab_ref):
  g = jnp.broadcast_to(g_ref[0], (Q_BLOCK, G_LEN))
  t = pltpu.roll(g, G_LEN - Q_BLOCK, 1, stride=1, stride_axis=0)[:, :A_WINDOW]
  qi = lax.broadcasted_iota(jnp.int32, (Q_BLOCK, A_WINDOW), 0)
  kk = lax.broadcasted_iota(jnp.int32, (Q_BLOCK, A_WINDOW), 1)
  qc = (qi + A_LEFT) // CHUNK
  kc = kk // CHUNK
  valid = (kc <= qc) & (kc >= qc - A_LEFT_CHUNKS)
  tab_ref[0] = jnp.where(valid, t * LOG2E, NEG_BIG)


def _bias_tables(rel_bias):
  flat = A_LEFT + Q_BLOCK - REL_MAX
  rev = rel_bias[:, ::-1]
  g = jnp.concatenate([
      jnp.broadcast_to(rel_bias[:, -1:], (A_HEADS, flat)),
      rev[:, : 2 * REL_MAX],
      jnp.broadcast_to(rel_bias[:, :1], (A_HEADS, G_LEN - flat - 2 * REL_MAX)),
  ], axis=1)[:, None, :]
  return pl.pallas_call(
      _bias_table_kernel,
      grid=(A_HEADS,),
      in_specs=[pl.BlockSpec((1, 1, G_LEN), lambda h: (h, 0, 0))],
      out_specs=pl.BlockSpec((1, Q_BLOCK, A_WINDOW), lambda h: (h, 0, 0)),
      out_shape=jax.ShapeDtypeStruct((A_HEADS, Q_BLOCK, A_WINDOW), F32),
      compiler_params=pltpu.CompilerParams(dimension_semantics=("parallel",)),
      name="bias_table",
  )(g)


def _in_proj_kernel(h_ref, gmix_ref, wq_ref, wk_ref, wv_ref, wga_ref, wgb_ref, wmla_ref,
                    gq_ref, gkv_ref, wuq_ref, wukv_ref, *refs, make_tables):
  if make_tables:
    (pos_ref, invf_ref, q_out, k_out, v_out, ga_out, gb_out, qb_out, kb_out, vb_out,
     cos_out, sin_out) = refs
  else:
    cos_ref, sin_ref, q_out, k_out, v_out, ga_out, gb_out, qb_out, kb_out, vb_out = refs
  xn = _rms(h_ref[...], gmix_ref[...]).astype(BF16)
  zm = _dot(xn, wmla_ref[...])
  cqn = _rms(zm[:, :Q_LORA], gq_ref[...]).astype(BF16)
  ckvn = _rms(zm[:, Q_LORA:Q_LORA + KV_LORA], gkv_ref[...]).astype(BF16)
  qall = _dot(cqn, wuq_ref[...])
  kv = _dot(ckvn, wukv_ref[...])

  q_out[...] = (_dot(xn, wq_ref[...]) * (A_HEAD_DIM ** -0.5 * LOG2E)).astype(BF16)
  k_out[...] = _dot(xn, wk_ref[...]).astype(BF16)
  v_out[...] = _dot(xn, wv_ref[...]).astype(BF16)
  ga_out[...] = _dot(xn, wga_ref[...])
  gb_out[...] = _dot(xn, wgb_ref[...])

  if make_tables:
    cos, sin = _rope_tile(pos_ref[0], invf_ref[...])
    cos_out[...] = cos
    sin_out[...] = sin
  else:
    cos = cos_ref[...]
    sin = sin_ref[...]
  first_half = lax.broadcasted_iota(jnp.int32, (h_ref.shape[0], LANES), 1) < B_ROPE // 2

  def rope(g):
    swapped = jnp.where(first_half, pltpu.roll(g, LANES - B_ROPE // 2, 1), pltpu.roll(g, B_ROPE // 2, 1))
    return g * cos + swapped * sin

  k_rope = rope(zm[:, Q_LORA + KV_LORA:]).astype(BF16)
  ones = jnp.ones((h_ref.shape[0], B_QW - B_VDIM), BF16)
  for h in range(B_HEADS):
    qb = B_QW * h
    qb_out[:, qb:qb + B_NOPE] = (qall[:, qb:qb + B_NOPE] * B_QSCALE).astype(BF16)
    qb_out[:, qb + B_NOPE:qb + B_QW] = (rope(qall[:, qb + B_NOPE:qb + B_QW]) * B_QSCALE).astype(BF16)
    kb_out[:, B_QW * h:B_QW * h + B_NOPE] = kv[:, 2 * LANES * h:2 * LANES * h + B_NOPE].astype(BF16)
    kb_out[:, B_QW * h + B_NOPE:B_QW * (h + 1)] = k_rope
    vb_out[:, B_QW * h:B_QW * h + B_VDIM] = kv[:, 2 * LANES * h + B_NOPE:2 * LANES * (h + 1)].astype(BF16)
    vb_out[:, B_QW * h + B_VDIM:B_QW * (h + 1)] = ones


def _layer_spec(a, layer):
  return pl.BlockSpec((None,) + a.shape[1:], lambda i: (layer,) + (0,) * (a.ndim - 1),
                      pipeline_mode=pl.Buffered(1))


def _in_proj(h, w, layer, positions=None, tables=None):
  n = h.shape[0]
  t = TOKEN_TILE
  tok = lambda width: pl.BlockSpec((t, width), lambda i: (i, 0))
  w_in_block = lambda j: pl.BlockSpec((None, D_MODEL, A_WIDTH), lambda i: (layer, 0, j),
                                      pipeline_mode=pl.Buffered(1))
  small = [w["gmix"], w["wgb"], w["wmla"], w["gq"], w["gkv"], w["wuq"], w["wukv"]]
  in_specs = ([tok(D_MODEL), _layer_spec(small[0], layer)] + [w_in_block(j) for j in range(4)]
              + [_layer_spec(a, layer) for a in small[1:]])
  operands = [h, small[0]] + [w["w_qkvg"]] * 4 + small[1:]
  out_widths = [(A_WIDTH, BF16), (A_WIDTH, BF16), (A_WIDTH, BF16), (A_WIDTH, F32), (B_WIDTH, F32),
                (B_HEADS * B_QW, BF16), (B_HEADS * B_QW, BF16), (B_HEADS * B_QW, BF16)]
  make_tables = tables is None
  if make_tables:
    invf = _inv_freq_column()
    operands += [positions.reshape(n // t, 1, t), invf]
    in_specs += [pl.BlockSpec((1, 1, t), lambda i: (i, 0, 0)), pl.BlockSpec(invf.shape, lambda i: (0, 0))]
    out_widths += [(LANES, F32), (LANES, F32)]
  else:
    operands += list(tables)
    in_specs += [tok(LANES), tok(LANES)]
  return pl.pallas_call(
      functools.partial(_in_proj_kernel, make_tables=make_tables),
      grid=(n // t,),
      in_specs=in_specs,
      out_specs=[tok(wd) for wd, _ in out_widths],
      out_shape=[jax.ShapeDtypeStruct((n, wd), dt) for wd, dt in out_widths],
      compiler_params=pltpu.CompilerParams(dimension_semantics=("parallel",),
                                           vmem_limit_bytes=VMEM_LIMIT),
      name="in_proj",
  )(*operands)


def _attn_a_kernel(q_ref, k_ref, v_ref, gate_ref, tab_ref, o_ref):
  seq_len = q_ref.shape[1]
  lane = lax.broadcasted_iota(jnp.int32, (1, LANES), 1)
  head_rows = ((lane < A_HEAD_DIM).astype(BF16), (lane >= A_HEAD_DIM).astype(BF16))
  lo_mask = lax.broadcasted_iota(jnp.int32, (Q_BLOCK, LANES), 1) < A_HEAD_DIM
  ones = jnp.ones((A_WINDOW, LANES), BF16)

  def window(qb):
    k0 = max(0, Q_BLOCK * qb - A_LEFT)
    return k0, Q_BLOCK * (qb + 1) - k0

  def scores(qb, pr, hh):
    k0, nk = window(qb)
    ls = slice(LANES * pr, LANES * (pr + 1))
    qp = q_ref[0, Q_BLOCK * qb:Q_BLOCK * (qb + 1), ls]
    return (_dot_nt(qp * head_rows[hh], k_ref[0, k0:k0 + nk, ls])
            + tab_ref[2 * pr + hh, :, A_WINDOW - nk:])

  def weighted(qb, pr, s):
    k0, nk = window(qb)
    vp = jnp.concatenate([v_ref[0, k0:k0 + nk, LANES * pr:LANES * (pr + 1)], ones[:nk]], axis=1)
    e = jnp.exp2(s - jnp.max(s, axis=-1, keepdims=True))
    pv = _dot(e.astype(BF16), vp)
    return pv[:, :LANES] / pv[:, LANES:]

  items = [(qb, pr, hh) for qb in reversed(range(seq_len // Q_BLOCK))
           for pr in range(A_STEP_PAIRS) for hh in range(2)]
  depth = 2
  pending = [scores(*it) for it in items[:depth]]
  outs = []
  for i, (qb, pr, hh) in enumerate(items):
    if i + depth < len(items):
      pending.append(scores(*items[i + depth]))
    outs.append(weighted(qb, pr, pending.pop(0)))
    if hh == 1:
      rows = slice(Q_BLOCK * qb, Q_BLOCK * (qb + 1))
      ls = slice(LANES * pr, LANES * (pr + 1))
      o = jnp.where(lo_mask, outs[0], outs[1])
      o_ref[0, rows, ls] = (o * _silu(gate_ref[0, rows, ls])).astype(BF16)
      outs = []


def _attn_a(q, k, v, gate, tab):
  b, s, _ = q.shape
  pair = pl.BlockSpec((1, s, A_STEP_PAIRS * LANES), lambda p, i: (i, 0, p))
  return pl.pallas_call(
      _attn_a_kernel,
      grid=(A_HEADS // (2 * A_STEP_PAIRS), b),
      in_specs=[pair, pair, pair, pair,
                pl.BlockSpec((2 * A_STEP_PAIRS, Q_BLOCK, A_WINDOW), lambda p, i: (p, 0, 0))],
      out_specs=pair,
      out_shape=jax.ShapeDtypeStruct((b, s, A_WIDTH), BF16),
      compiler_params=pltpu.CompilerParams(dimension_semantics=("parallel", "parallel"),
                                           vmem_limit_bytes=VMEM_LIMIT),
      name="attn_a",
  )(q, k, v, gate, tab)


def _attn_b_kernel(q_ref, k_ref, v_ref, gate_ref, o_ref):
  seq_len = q_ref.shape[1]
  qi = lax.broadcasted_iota(jnp.int32, (Q_BLOCK, Q_BLOCK), 0)
  kj = lax.broadcasted_iota(jnp.int32, (Q_BLOCK, Q_BLOCK), 1)
  diag_ok = (kj // CHUNK) <= (qi // CHUNK)

  def scores(qb, h):
    rows = slice(Q_BLOCK * qb, Q_BLOCK * (qb + 1))
    wide = slice(B_QW * h, B_QW * (h + 1))
    qh = q_ref[0, rows, wide]
    s_d = jnp.where(diag_ok, _dot_nt(qh, k_ref[0, rows, wide]), -jnp.inf)
    s_p = _dot_nt(qh, k_ref[0, :Q_BLOCK * qb, wide]) if qb else None
    return s_d, s_p

  def finish(qb, h, s_d, s_p):
    rows = slice(Q_BLOCK * qb, Q_BLOCK * (qb + 1))
    wide = slice(B_QW * h, B_QW * (h + 1))
    narrow = slice(B_VDIM * h, B_VDIM * (h + 1))
    m = jnp.max(s_d, axis=-1, keepdims=True)
    if qb:
      m = jnp.maximum(m, jnp.max(s_p, axis=-1, keepdims=True))
    pv = _dot(jnp.exp2(s_d - m).astype(BF16), v_ref[0, rows, wide])
    if qb:
      pv = pv + _dot(jnp.exp2(s_p - m).astype(BF16), v_ref[0, :Q_BLOCK * qb, wide])
    o = pv[:, :B_VDIM] / pv[:, B_VDIM:]
    o_ref[0, rows, narrow] = (o * _silu(gate_ref[0, rows, narrow])).astype(BF16)

  items = [(qb, h) for qb in reversed(range(seq_len // Q_BLOCK)) for h in range(B_STEP_HEADS)]
  depth = 2
  pending = [scores(*it) for it in items[:depth]]
  for i, it in enumerate(items):
    if i + depth < len(items):
      pending.append(scores(*items[i + depth]))
    finish(*it, *pending.pop(0))


def _attn_b(q, k, v, gate):
  b, s, _ = q.shape
  wide = pl.BlockSpec((1, s, B_STEP_HEADS * B_QW), lambda i, h: (i, 0, h))
  narrow = pl.BlockSpec((1, s, B_STEP_HEADS * B_VDIM), lambda i, h: (i, 0, h))
  return pl.pallas_call(
      _attn_b_kernel,
      grid=(b, B_HEADS // B_STEP_HEADS),
      in_specs=[wide, wide, wide, narrow],
      out_specs=narrow,
      out_shape=jax.ShapeDtypeStruct((b, s, B_WIDTH), BF16),
      compiler_params=pltpu.CompilerParams(dimension_semantics=("parallel", "parallel"),
                                           vmem_limit_bytes=VMEM_LIMIT),
      name="attn_b",
  )(q, k, v, gate)


def _out_ple_kernel(ma_ref, mb_ref, h_ref, p_ref, woa_ref, wob_ref, gple_ref, wpg_ref, bpg_ref,
                    wpe_ref, gfin_ref, o_ref, *, final_norm):
  n_sub = 2
  sub = h_ref.shape[0] // n_sub

  def stage1(j):
    rs = slice(sub * j, sub * (j + 1))
    h1 = h_ref[rs, :] + (_dot(ma_ref[rs, :], woa_ref[...]) + _dot(mb_ref[rs, :], wob_ref[...]))
    pe = _dot(p_ref[rs, :].astype(BF16), wpe_ref[...])
    return h1, pe

  def stage2(j, h1, pe):
    rs = slice(sub * j, sub * (j + 1))
    xn = _rms(h1, gple_ref[...]).astype(BF16)
    z = _dot(xn, wpg_ref[...]) + bpg_ref[...]
    h2 = h1 + pe * (1.0 / (1.0 + jnp.exp(-z)))
    o_ref[rs, :] = _rms(h2, gfin_ref[...]) if final_norm else h2

  pending = stage1(0)
  for j in range(n_sub):
    nxt = stage1(j + 1) if j + 1 < n_sub else None
    stage2(j, *pending)
    pending = nxt


def _out_ple(ma, mb, h, p, w, layer, final_norm):
  n = h.shape[0]
  t = OUT_TOKEN_TILE
  tok = lambda width: pl.BlockSpec((t, width), lambda i: (i, 0))
  w_out_half = lambda j: pl.BlockSpec((None, A_WIDTH, D_MODEL), lambda i: (layer, j, 0))
  rest = [w["gple"], w["wpg"], w["bpg"], w["wpe"]]
  return pl.pallas_call(
      functools.partial(_out_ple_kernel, final_norm=final_norm),
      grid=(n // t,),
      in_specs=([tok(A_WIDTH), tok(B_WIDTH), tok(D_MODEL),
                 pl.BlockSpec((None, t, D_PLE), lambda i: (layer, i, 0)),
                 w_out_half(0), w_out_half(1)] + [_layer_spec(a, layer) for a in rest]
                + [pl.BlockSpec(w["gfin"].shape, lambda i: (0, 0))]),
      out_specs=tok(D_MODEL),
      out_shape=jax.ShapeDtypeStruct((n, D_MODEL), F32),
      compiler_params=pltpu.CompilerParams(dimension_semantics=("parallel",),
                                           vmem_limit_bytes=VMEM_LIMIT),
      name="out_ple",
  )(ma, mb, h, p, w["w_out"], w["w_out"], *rest, w["gfin"])


def _prep_params(norm_mix, w_in, g_q, w_uq, g_kv, w_ukv, w_out, norm_ple, w_pe, w_pg, b_pg,
                 norm_final):
  depth = w_in.shape[0]
  mla0 = 4 * A_WIDTH
  kr0 = mla0 + Q_LORA + KV_LORA
  pad = jnp.zeros((depth, D_MODEL, LANES - B_ROPE), F32)
  wmla = jnp.concatenate([w_in[..., mla0:kr0 + B_ROPE], pad], axis=-1)
  uq = w_uq.reshape(depth, Q_LORA, B_HEADS, B_NOPE + B_ROPE)
  zpad = jnp.zeros((depth, Q_LORA, B_HEADS, LANES - B_ROPE), F32)
  wuq = jnp.concatenate([uq, zpad], axis=-1)
  return {
      "gmix": norm_mix[:, None, :],
      "w_qkvg": w_in[..., :mla0].astype(BF16),
      "wgb": w_in[..., kr0 + B_ROPE:].astype(BF16),
      "wmla": wmla.astype(BF16),
      "gq": g_q[:, None, :], "gkv": g_kv[:, None, :],
      "wuq": wuq.reshape(depth, Q_LORA, B_HEADS * B_QW).astype(BF16),
      "wukv": w_ukv.astype(BF16),
      "w_out": w_out.astype(BF16),
      "gple": norm_ple[:, None, :], "wpg": w_pg.astype(BF16), "bpg": b_pg[:, None, :],
      "wpe": w_pe.astype(BF16),
      "gfin": norm_final[None, :],
  }


def kernel(x, p, positions, norm_mix, w_in, rel_bias, g_q, w_uq, g_kv, w_ukv, w_out,
           norm_ple, w_pe, w_pg, b_pg, norm_final):
  b, s, d = x.shape
  depth = w_in.shape[0]
  n = b * s
  w = _prep_params(norm_mix, w_in, g_q, w_uq, g_kv, w_ukv, w_out, norm_ple, w_pe, w_pg, b_pg,
                   norm_final)
  p = p.reshape(depth, n, D_PLE)
  h = x.reshape(n, d)
  seq = lambda a: a.reshape(b, s, a.shape[-1])
  tables = None
  for i in range(depth):
    tab = _bias_tables(rel_bias[i])
    if tables is None:
      *outs, cos, sin = _in_proj(h, w, i, positions=positions)
      tables = (cos, sin)
    else:
      outs = _in_proj(h, w, i, tables=tables)
    qa, ka, va, ga, gb, qb, kb, vb = outs
    ma = _attn_a(seq(qa), seq(ka), seq(va), seq(ga), tab)
    mb = _attn_b(seq(qb), seq(kb), seq(vb), seq(gb))
    h = _out_ple(ma.reshape(n, A_WIDTH), mb.reshape(n, B_WIDTH), h, p, w, i,
                 final_norm=(i == depth - 1))
  return h.reshape(b, s, d)
```

```python
import functools
import math

import jax
import jax.numpy as jnp
from jax import lax
from jax.experimental import pallas as pl
from jax.experimental.pallas import tpu as pltpu

D_MODEL = 1024
CHUNK = 64
D_PLE = 256
EPS = 1e-6
A_HEADS = 8
A_HEAD_DIM = 64
A_WIDTH = A_HEADS * A_HEAD_DIM
A_LEFT_CHUNKS = 8
REL_MAX = 128
B_HEADS = 4
B_NOPE = 128
B_ROPE = 64
B_VDIM = 128
B_WIDTH = B_HEADS * B_VDIM
Q_LORA = 256
KV_LORA = 128
ROPE_THETA = 10000.0

LANES = 128
TOKEN_TILE = 1024
OUT_TOKEN_TILE = 1024
Q_BLOCK = 256
A_STEP_PAIRS = 2
B_STEP_HEADS = 2
A_LEFT = A_LEFT_CHUNKS * CHUNK
A_WINDOW = A_LEFT + Q_BLOCK
B_QW = B_NOPE + LANES
NEG_BIG = -1e30
LOG2E = math.log2(math.e)
B_QSCALE = (B_NOPE + B_ROPE) ** -0.5 * LOG2E
VMEM_LIMIT = 56 * 1024 * 1024

BF16 = jnp.bfloat16
F32 = jnp.float32


def _dot(a, b):
  return jnp.dot(a, b, preferred_element_type=F32)


def _dot_nt(a, b):
  return lax.dot_general(a, b, (((1,), (1,)), ((), ())), preferred_element_type=F32)


def _rms(x, g):
  y = x * lax.rsqrt(jnp.mean(x * x, axis=-1, keepdims=True) + EPS)
  return y * g


def _silu(g):
  return g * (1.0 / (1.0 + jnp.exp(-g)))


def _inv_freq_column():
  inv_freq = ROPE_THETA ** (-jnp.arange(0, B_ROPE, 2, dtype=F32) / B_ROPE)
  return inv_freq[:, None]


def _rope_tile(pos_row, inv_freq_col):
  t = pos_row.shape[1]
  ang = pos_row.astype(F32) * inv_freq_col
  c, s = jnp.cos(ang), jnp.sin(ang)
  zeros = jnp.zeros((LANES - B_ROPE, t), F32)
  return jnp.concatenate([c, c, zeros], axis=0).T, jnp.concatenate([-s, s, zeros], axis=0).T


G_LEN = 1024


def _bias_table_kernel(g_ref, tab_ref):
  g = jnp.broadcast_to(g_ref[0], (Q_BLOCK, G_LEN))
  t = pltpu.roll(g, G_LEN - Q_BLOCK, 1, stride=1, stride_axis=0)[:, :A_WINDOW]
  qi = lax.broadcasted_iota(jnp.int32, (Q_BLOCK, A_WINDOW), 0)
  kk = lax.broadcasted_iota(jnp.int32, (Q_BLOCK, A_WINDOW), 1)
  qc = (qi + A_LEFT) // CHUNK
  kc = kk // CHUNK
  valid = (kc <= qc) & (kc >= qc - A_LEFT_CHUNKS)
  tab_ref[0] = jnp.where(valid, t * LOG2E, NEG_BIG)


def _bias_tables(rel_bias):
  flat = A_LEFT + Q_BLOCK - REL_MAX
  rev = rel_bias[:, ::-1]
  g = jnp.concatenate([
      jnp.broadcast_to(rel_bias[:, -1:], (A_HEADS, flat)),
      rev[:, : 2 * REL_MAX],
      jnp.broadcast_to(rel_bias[:, :1], (A_HEADS, G_LEN - flat - 2 * REL_MAX)),
  ], axis=1)[:, None, :]
  return pl.pallas_call(
      _bias_table_kernel,
      grid=(A_HEADS,),
      in_specs=[pl.BlockSpec((1, 1, G_LEN), lambda h: (h, 0, 0))],
      out_specs=pl.BlockSpec((1, Q_BLOCK, A_WINDOW), lambda h: (h, 0, 0)),
      out_shape=jax.ShapeDtypeStruct((A_HEADS, Q_BLOCK, A_WINDOW), F32),
      compiler_params=pltpu.CompilerParams(dimension_semantics=("parallel",)),
      name="bias_table",
  )(g)


def _in_proj_kernel(h_ref, gmix_ref, wq_ref, wk_ref, wv_ref, wga_ref, wgb_ref, wmla_ref,
                    gq_ref, gkv_ref, wuq_ref, wukv_ref, *refs, make_tables):
  if make_tables:
    (pos_ref, invf_ref, q_out, k_out, v_out, ga_out, gb_out, qb_out, kb_out, vb_out,
     cos_out, sin_out) = refs
  else:
    cos_ref, sin_ref, q_out, k_out, v_out, ga_out, gb_out, qb_out, kb_out, vb_out = refs
  xn = _rms(h_ref[...], gmix_ref[...]).astype(BF16)
  zm = _dot(xn, wmla_ref[...])
  cqn = _rms(zm[:, :Q_LORA], gq_ref[...]).astype(BF16)
  ckvn = _rms(zm[:, Q_LORA:Q_LORA + KV_LORA], gkv_ref[...]).astype(BF16)
  qall = _dot(cqn, wuq_ref[...])
  kv = _dot(ckvn, wukv_ref[...])

  q_out[...] = (_dot(xn, wq_ref[...]) * (A_HEAD_DIM ** -0.5 * LOG2E)).astype(BF16)
  k_out[...] = _dot(xn, wk_ref[...]).astype(BF16)
  v_out[...] = _dot(xn, wv_ref[...]).astype(BF16)
  ga_out[...] = _dot(xn, wga_ref[...])
  gb_out[...] = _dot(xn, wgb_ref[...])

  if make_tables:
    cos, sin = _rope_tile(pos_ref[0], invf_ref[...])
    cos_out[...] = cos
    sin_out[...] = sin
  else:
    cos = cos_ref[...]
    sin = sin_ref[...]
  first_half = lax.broadcasted_iota(jnp.int32, (h_ref.shape[0], LANES), 1) < B_ROPE // 2

  def rope(g):
    swapped = jnp.where(first_half, pltpu.roll(g, LANES - B_ROPE // 2, 1), pltpu.roll(g, B_ROPE // 2, 1))
    return g * cos + swapped * sin

  k_rope = rope(zm[:, Q_LORA + KV_LORA:]).astype(BF16)
  ones = jnp.ones((h_ref.shape[0], B_QW - B_VDIM), BF16)
  for h in range(B_HEADS):
    qb = B_QW * h
    qb_out[:, qb:qb + B_NOPE] = (qall[:, qb:qb + B_NOPE] * B_QSCALE).astype(BF16)
    qb_out[:, qb + B_NOPE:qb + B_QW] = (rope(qall[:, qb + B_NOPE:qb + B_QW]) * B_QSCALE).astype(BF16)
    kb_out[:, B_QW * h:B_QW * h + B_NOPE] = kv[:, 2 * LANES * h:2 * LANES * h + B_NOPE].astype(BF16)
    kb_out[:, B_QW * h + B_NOPE:B_QW * (h + 1)] = k_rope
    vb_out[:, B_QW * h:B_QW * h + B_VDIM] = kv[:, 2 * LANES * h + B_NOPE:2 * LANES * (h + 1)].astype(BF16)
    vb_out[:, B_QW * h + B_VDIM:B_QW * (h + 1)] = ones


def _layer_spec(a, layer):
  return pl.BlockSpec((None,) + a.shape[1:], lambda i: (layer,) + (0,) * (a.ndim - 1),
                      pipeline_mode=pl.Buffered(1))


def _in_proj(h, w, layer, positions=None, tables=None):
  n = h.shape[0]
  t = TOKEN_TILE
  tok = lambda width: pl.BlockSpec((t, width), lambda i: (i, 0))
  w_in_block = lambda j: pl.BlockSpec((None, D_MODEL, A_WIDTH), lambda i: (layer, 0, j),
                                      pipeline_mode=pl.Buffered(1))
  small = [w["gmix"], w["wgb"], w["wmla"], w["gq"], w["gkv"], w["wuq"], w["wukv"]]
  in_specs = ([tok(D_MODEL), _layer_spec(small[0], layer)] + [w_in_block(j) for j in range(4)]
              + [_layer_spec(a, layer) for a in small[1:]])
  operands = [h, small[0]] + [w["w_qkvg"]] * 4 + small[1:]
  out_widths = [(A_WIDTH, BF16), (A_WIDTH, BF16), (A_WIDTH, BF16), (A_WIDTH, F32), (B_WIDTH, F32),
                (B_HEADS * B_QW, BF16), (B_HEADS * B_QW, BF16), (B_HEADS * B_QW, BF16)]
  make_tables = tables is None
  if make_tables:
    invf = _inv_freq_column()
    operands += [positions.reshape(n // t, 1, t), invf]
    in_specs += [pl.BlockSpec((1, 1, t), lambda i: (i, 0, 0)), pl.BlockSpec(invf.shape, lambda i: (0, 0))]
    out_widths += [(LANES, F32), (LANES, F32)]
  else:
    operands += list(tables)
    in_specs += [tok(LANES), tok(LANES)]
  return pl.pallas_call(
      functools.partial(_in_proj_kernel, make_tables=make_tables),
      grid=(n // t,),
      in_specs=in_specs,
      out_specs=[tok(wd) for wd, _ in out_widths],
      out_shape=[jax.ShapeDtypeStruct((n, wd), dt) for wd, dt in out_widths],
      compiler_params=pltpu.CompilerParams(dimension_semantics=("parallel",),
                                           vmem_limit_bytes=VMEM_LIMIT),
      name="in_proj",
  )(*operands)


def _attn_a_kernel(q_ref, k_ref, v_ref, gate_ref, tab_ref, o_ref):
  seq_len = q_ref.shape[1]
  lane = lax.broadcasted_iota(jnp.int32, (1, LANES), 1)
  head_rows = ((lane < A_HEAD_DIM).astype(BF16), (lane >= A_HEAD_DIM).astype(BF16))
  lo_mask = lax.broadcasted_iota(jnp.int32, (Q_BLOCK, LANES), 1) < A_HEAD_DIM
  ones = jnp.ones((A_WINDOW, LANES), BF16)

  def window(qb):
    k0 = max(0, Q_BLOCK * qb - A_LEFT)
    return k0, Q_BLOCK * (qb + 1) - k0

  half = Q_BLOCK // 2

  def panels(qb):
    _, nk = window(qb)
    first_live = half if nk == A_WINDOW else 0
    return (slice(0, half), slice(0, nk - half)), (slice(half, Q_BLOCK), slice(first_live, nk))

  def scores(qb, pr, hh):
    k0, nk = window(qb)
    ls = slice(LANES * pr, LANES * (pr + 1))
    qp = q_ref[0, Q_BLOCK * qb:Q_BLOCK * (qb + 1), ls]
    s = _dot_nt(qp * head_rows[hh], k_ref[0, k0:k0 + nk, ls])
    col0 = A_WINDOW - nk
    return [s[rs, cs] + tab_ref[2 * pr + hh, rs, col0 + cs.start:col0 + cs.stop] for rs, cs in panels(qb)]

  def weighted(qb, pr, s_panels):
    k0, nk = window(qb)
    vp = jnp.concatenate([v_ref[0, k0:k0 + nk, LANES * pr:LANES * (pr + 1)], ones[:nk]], axis=1)
    rows = []
    for s, (_, cs) in zip(s_panels, panels(qb)):
      e = jnp.exp2(s - jnp.max(s, axis=-1, keepdims=True)).astype(BF16)
      dead = [jnp.zeros((half, w), BF16) for w in (cs.start, nk - cs.stop)]
      rows.append(jnp.concatenate([x for x in (dead[0], e, dead[1]) if x.shape[1]], axis=1))
    pv = _dot(jnp.concatenate(rows, axis=0), vp)
    return pv[:, :LANES] / pv[:, LANES:]

  items = [(qb, pr, hh) for qb in reversed(range(seq_len // Q_BLOCK))
           for pr in range(A_STEP_PAIRS) for hh in range(2)]
  depth = 2
  pending = [scores(*it) for it in items[:depth]]
  outs = []
  for i, (qb, pr, hh) in enumerate(items):
    if i + depth < len(items):
      pending.append(scores(*items[i + depth]))
    outs.append(weighted(qb, pr, pending.pop(0)))
    if hh == 1:
      rows = slice(Q_BLOCK * qb, Q_BLOCK * (qb + 1))
      ls = slice(LANES * pr, LANES * (pr + 1))
      o = jnp.where(lo_mask, outs[0], outs[1])
      o_ref[0, rows, ls] = (o * _silu(gate_ref[0, rows, ls])).astype(BF16)
      outs = []


def _attn_a(q, k, v, gate, tab):
  b, s, _ = q.shape
  pair = pl.BlockSpec((1, s, A_STEP_PAIRS * LANES), lambda p, i: (i, 0, p))
  return pl.pallas_call(
      _attn_a_kernel,
      grid=(A_HEADS // (2 * A_STEP_PAIRS), b),
      in_specs=[pair, pair, pair, pair,
                pl.BlockSpec((2 * A_STEP_PAIRS, Q_BLOCK, A_WINDOW), lambda p, i: (p, 0, 0))],
      out_specs=pair,
      out_shape=jax.ShapeDtypeStruct((b, s, A_WIDTH), BF16),
      compiler_params=pltpu.CompilerParams(dimension_semantics=("parallel", "parallel"),
                                           vmem_limit_bytes=VMEM_LIMIT),
      name="attn_a",
  )(q, k, v, gate, tab)


def _attn_b_kernel(q_ref, k_ref, v_ref, gate_ref, o_ref):
  seq_len = q_ref.shape[1]
  qi = lax.broadcasted_iota(jnp.int32, (Q_BLOCK, Q_BLOCK), 0)
  kj = lax.broadcasted_iota(jnp.int32, (Q_BLOCK, Q_BLOCK), 1)
  diag_ok = (kj // CHUNK) <= (qi // CHUNK)

  def scores(qb, h):
    rows = slice(Q_BLOCK * qb, Q_BLOCK * (qb + 1))
    wide = slice(B_QW * h, B_QW * (h + 1))
    qh = q_ref[0, rows, wide]
    s_d = jnp.where(diag_ok, _dot_nt(qh, k_ref[0, rows, wide]), -jnp.inf)
    s_p = _dot_nt(qh, k_ref[0, :Q_BLOCK * qb, wide]) if qb else None
    return s_d, s_p

  def finish(qb, h, s_d, s_p):
    rows = slice(Q_BLOCK * qb, Q_BLOCK * (qb + 1))
    wide = slice(B_QW * h, B_QW * (h + 1))
    narrow = slice(B_VDIM * h, B_VDIM * (h + 1))
    m = jnp.max(s_d, axis=-1, keepdims=True)
    if qb:
      m = jnp.maximum(m, jnp.max(s_p, axis=-1, keepdims=True))
    pv = _dot(jnp.exp2(s_d - m).astype(BF16), v_ref[0, rows, wide])
    if qb:
      pv = pv + _dot(jnp.exp2(s_p - m).astype(BF16), v_ref[0, :Q_BLOCK * qb, wide])
    o = pv[:, :B_VDIM] / pv[:, B_VDIM:]
    o_ref[0, rows, narrow] = (o * _silu(gate_ref[0, rows, narrow])).astype(BF16)

  items = [(qb, h) for qb in reversed(range(seq_len // Q_BLOCK)) for h in range(B_STEP_HEADS)]
  depth = 2
  pending = [scores(*it) for it in items[:depth]]
  for i, it in enumerate(items):
    if i + depth < len(items):
      pending.append(scores(*items[i + depth]))
    finish(*it, *pending.pop(0))


def _attn_b(q, k, v, gate):
  b, s, _ = q.shape
  wide = pl.BlockSpec((1, s, B_STEP_HEADS * B_QW), lambda i, h: (i, 0, h))
  narrow = pl.BlockSpec((1, s, B_STEP_HEADS * B_VDIM), lambda i, h: (i, 0, h))
  return pl.pallas_call(
      _attn_b_kernel,
      grid=(b, B_HEADS // B_STEP_HEADS),
      in_specs=[wide, wide, wide, narrow],
      out_specs=narrow,
      out_shape=jax.ShapeDtypeStruct((b, s, B_WIDTH), BF16),
      compiler_params=pltpu.CompilerParams(dimension_semantics=("parallel", "parallel"),
                                           vmem_limit_bytes=VMEM_LIMIT),
      name="attn_b",
  )(q, k, v, gate)


def _out_ple_kernel(ma_ref, mb_ref, h_ref, p_ref, woa_ref, wob_ref, gple_ref, wpg_ref, bpg_ref,
                    wpe_ref, gfin_ref, o_ref, *, final_norm):
  n_sub = 4
  sub = h_ref.shape[0] // n_sub

  def stage1(j):
    rs = slice(sub * j, sub * (j + 1))
    h1 = h_ref[rs, :] + (_dot(ma_ref[rs, :], woa_ref[...]) + _dot(mb_ref[rs, :], wob_ref[...]))
    pe = _dot(p_ref[rs, :].astype(BF16), wpe_ref[...])
    return h1, pe

  def stage2(j, h1, pe):
    rs = slice(sub * j, sub * (j + 1))
    xn = _rms(h1, gple_ref[...]).astype(BF16)
    z = _dot(xn, wpg_ref[...]) + bpg_ref[...]
    h2 = h1 + pe * (1.0 / (1.0 + jnp.exp(-z)))
    o_ref[rs, :] = _rms(h2, gfin_ref[...]) if final_norm else h2

  pending = stage1(0)
  for j in range(n_sub):
    nxt = stage1(j + 1) if j + 1 < n_sub else None
    stage2(j, *pending)
    pending = nxt


def _out_ple(ma, mb, h, p, w, layer, final_norm):
  n = h.shape[0]
  t = OUT_TOKEN_TILE
  tok = lambda width: pl.BlockSpec((t, width), lambda i: (i, 0))
  w_out_half = lambda j: pl.BlockSpec((None, A_WIDTH, D_MODEL), lambda i: (layer, j, 0))
  rest = [w["gple"], w["wpg"], w["bpg"], w["wpe"]]
  return pl.pallas_call(
      functools.partial(_out_ple_kernel, final_norm=final_norm),
      grid=(n // t,),
      in_specs=([tok(A_WIDTH), tok(B_WIDTH), tok(D_MODEL),
                 pl.BlockSpec((None, t, D_PLE), lambda i: (layer, i, 0)),
                 w_out_half(0), w_out_half(1)] + [_layer_spec(a, layer) for a in rest]
                + [pl.BlockSpec(w["gfin"].shape, lambda i: (0, 0))]),
      out_specs=tok(D_MODEL),
      out_shape=jax.ShapeDtypeStruct((n, D_MODEL), F32),
      compiler_params=pltpu.CompilerParams(dimension_semantics=("parallel",),
                                           vmem_limit_bytes=VMEM_LIMIT),
      name="out_ple",
  )(ma, mb, h, p, w["w_out"], w["w_out"], *rest, w["gfin"])


def _prep_params(norm_mix, w_in, g_q, w_uq, g_kv, w_ukv, w_out, norm_ple, w_pe, w_pg, b_pg,
                 norm_final):
  depth = w_in.shape[0]
  mla0 = 4 * A_WIDTH
  kr0 = mla0 + Q_LORA + KV_LORA
  pad = jnp.zeros((depth, D_MODEL, LANES - B_ROPE), F32)
  wmla = jnp.concatenate([w_in[..., mla0:kr0 + B_ROPE], pad], axis=-1)
  uq = w_uq.reshape(depth, Q_LORA, B_HEADS, B_NOPE + B_ROPE)
  zpad = jnp.zeros((depth, Q_LORA, B_HEADS, LANES - B_ROPE), F32)
  wuq = jnp.concatenate([uq, zpad], axis=-1)
  return {
      "gmix": norm_mix[:, None, :],
      "w_qkvg": w_in[..., :mla0].astype(BF16),
      "wgb": w_in[..., kr0 + B_ROPE:].astype(BF16),
      "wmla": wmla.astype(BF16),
      "gq": g_q[:, None, :], "gkv": g_kv[:, None, :],
      "wuq": wuq.reshape(depth, Q_LORA, B_HEADS * B_QW).astype(BF16),
      "wukv": w_ukv.astype(BF16),
      "w_out": w_out.astype(BF16),
      "gple": norm_ple[:, None, :], "wpg": w_pg.astype(BF16), "bpg": b_pg[:, None, :],
      "wpe": w_pe.astype(BF16),
      "gfin": norm_final[None, :],
  }


def kernel(x, p, positions, norm_mix, w_in, rel_bias, g_q, w_uq, g_kv, w_ukv, w_out,
           norm_ple, w_pe, w_pg, b_pg, norm_final):
  b, s, d = x.shape
  depth = w_in.shape[0]
  n = b * s
  w = _prep_params(norm_mix, w_in, g_q, w_uq, g_kv, w_ukv, w_out, norm_ple, w_pe, w_pg, b_pg,
                   norm_final)
  p = p.reshape(depth, n, D_PLE)
  h = x.reshape(n, d)
  seq = lambda a: a.reshape(b, s, a.shape[-1])
  tables = None
  for i in range(depth):
    tab = _bias_tables(rel_bias[i])
    if tables is None:
      *outs, cos, sin = _in_proj(h, w, i, positions=positions)
      tables = (cos, sin)
    else:
      outs = _in_proj(h, w, i, tables=tables)
    qa, ka, va, ga, gb, qb, kb, vb = outs
    ma = _attn_a(seq(qa), seq(ka), seq(va), seq(ga), tab)
    mb = _attn_b(seq(qb), seq(kb), seq(vb), seq(gb))
    h = _out_ple(ma.reshape(n, A_WIDTH), mb.reshape(n, B_WIDTH), h, p, w, i,
                 final_norm=(i == depth - 1))
  return h.reshape(b, s, d)
```

```python
import functools
import math

import jax
import jax.numpy as jnp
from jax import lax
from jax.experimental import pallas as pl
from jax.experimental.pallas import tpu as pltpu

D_MODEL = 1024
CHUNK = 64
D_PLE = 256
EPS = 1e-6
A_HEADS = 8
A_HEAD_DIM = 64
A_WIDTH = A_HEADS * A_HEAD_DIM
A_LEFT_CHUNKS = 8
REL_MAX = 128
B_HEADS = 4
B_NOPE = 128
B_ROPE = 64
B_VDIM = 128
B_WIDTH = B_HEADS * B_VDIM
Q_LORA = 256
KV_LORA = 128
ROPE_THETA = 10000.0

LANES = 128
TOKEN_TILE = 1024
OUT_TOKEN_TILE = 1024
Q_BLOCK = 256
A_STEP_PAIRS = 2
B_STEP_HEADS = 2
A_LEFT = A_LEFT_CHUNKS * CHUNK
A_WINDOW = A_LEFT + Q_BLOCK
B_QW = B_NOPE + LANES
NEG_BIG = -1e30
LOG2E = math.log2(math.e)
B_QSCALE = (B_NOPE + B_ROPE) ** -0.5 * LOG2E
VMEM_LIMIT = 56 * 1024 * 1024

BF16 = jnp.bfloat16
F32 = jnp.float32


def _dot(a, b):
  return jnp.dot(a, b, preferred_element_type=F32)


def _dot_nt(a, b):
  return lax.dot_general(a, b, (((1,), (1,)), ((), ())), preferred_element_type=F32)


def _rms(x, g):
  y = x * lax.rsqrt(jnp.mean(x * x, axis=-1, keepdims=True) + EPS)
  return y * g


def _silu(g):
  return g * (1.0 / (1.0 + jnp.exp(-g)))


def _inv_freq_column():
  inv_freq = ROPE_THETA ** (-jnp.arange(0, B_ROPE, 2, dtype=F32) / B_ROPE)
  return inv_freq[:, None]


def _rope_tile(pos_row, inv_freq_col):
  t = pos_row.shape[1]
  ang = pos_row.astype(F32) * inv_freq_col
  c, s = jnp.cos(ang), jnp.sin(ang)
  zeros = jnp.zeros((LANES - B_ROPE, t), F32)
  return jnp.concatenate([c, c, zeros], axis=0).T, jnp.concatenate([-s, s, zeros], axis=0).T


G_LEN = 1024


def _bias_table_kernel(g_ref, tab_ref):
  g = jnp.broadcast_to(g_ref[0], (Q_BLOCK, G_LEN))
  t = pltpu.roll(g, G_LEN - Q_BLOCK, 1, stride=1, stride_axis=0)[:, :A_WINDOW]
  qi = lax.broadcasted_iota(jnp.int32, (Q_BLOCK, A_WINDOW), 0)
  kk = lax.broadcasted_iota(jnp.int32, (Q_BLOCK, A_WINDOW), 1)
  qc = (qi + A_LEFT) // CHUNK
  kc = kk // CHUNK
  valid = (kc <= qc) & (kc >= qc - A_LEFT_CHUNKS)
  tab_ref[0] = jnp.where(valid, t * LOG2E, NEG_BIG)


def _bias_tables(rel_bias):
  flat = A_LEFT + Q_BLOCK - REL_MAX
  rev = rel_bias[:, ::-1]
  g = jnp.concatenate([
      jnp.broadcast_to(rel_bias[:, -1:], (A_HEADS, flat)),
      rev[:, : 2 * REL_MAX],
      jnp.broadcast_to(rel_bias[:, :1], (A_HEADS, G_LEN - flat - 2 * REL_MAX)),
  ], axis=1)[:, None, :]
  return pl.pallas_call(
      _bias_table_kernel,
      grid=(A_HEADS,),
      in_specs=[pl.BlockSpec((1, 1, G_LEN), lambda h: (h, 0, 0))],
      out_specs=pl.BlockSpec((1, Q_BLOCK, A_WINDOW), lambda h: (h, 0, 0)),
      out_shape=jax.ShapeDtypeStruct((A_HEADS, Q_BLOCK, A_WINDOW), F32),
      compiler_params=pltpu.CompilerParams(dimension_semantics=("parallel",)),
      name="bias_table",
  )(g)


def _in_proj_kernel(h_ref, gmix_ref, wq_ref, wk_ref, wv_ref, wga_ref, wgb_ref, wmla_ref,
                    gq_ref, gkv_ref, wuq_ref, wukv_ref, *refs, make_tables):
  if make_tables:
    (pos_ref, invf_ref, q_out, k_out, v_out, ga_out, gb_out, qb_out, kb_out, vb_out,
     cos_out, sin_out) = refs
  else:
    cos_ref, sin_ref, q_out, k_out, v_out, ga_out, gb_out, qb_out, kb_out, vb_out = refs
  xn = _rms(h_ref[...], gmix_ref[...]).astype(BF16)
  zm = _dot(xn, wmla_ref[...])
  cqn = _rms(zm[:, :Q_LORA], gq_ref[...]).astype(BF16)
  ckvn = _rms(zm[:, Q_LORA:Q_LORA + KV_LORA], gkv_ref[...]).astype(BF16)
  qall = _dot(cqn, wuq_ref[...])
  kv = _dot(ckvn, wukv_ref[...])

  q_out[...] = (_dot(xn, wq_ref[...]) * (A_HEAD_DIM ** -0.5 * LOG2E)).astype(BF16)
  k_out[...] = _dot(xn, wk_ref[...]).astype(BF16)
  v_out[...] = _dot(xn, wv_ref[...]).astype(BF16)
  ga_out[...] = _dot(xn, wga_ref[...])
  gb_out[...] = _dot(xn, wgb_ref[...])

  if make_tables:
    cos, sin = _rope_tile(pos_ref[0], invf_ref[...])
    cos_out[...] = cos
    sin_out[...] = sin
  else:
    cos = cos_ref[...]
    sin = sin_ref[...]
  first_half = lax.broadcasted_iota(jnp.int32, (h_ref.shape[0], LANES), 1) < B_ROPE // 2

  def rope(g):
    swapped = jnp.where(first_half, pltpu.roll(g, LANES - B_ROPE // 2, 1), pltpu.roll(g, B_ROPE // 2, 1))
    return g * cos + swapped * sin

  k_rope = rope(zm[:, Q_LORA + KV_LORA:]).astype(BF16)
  ones = jnp.ones((h_ref.shape[0], B_QW - B_VDIM), BF16)
  for h in range(B_HEADS):
    qb = B_QW * h
    qb_out[:, qb:qb + B_NOPE] = (qall[:, qb:qb + B_NOPE] * B_QSCALE).astype(BF16)
    qb_out[:, qb + B_NOPE:qb + B_QW] = (rope(qall[:, qb + B_NOPE:qb + B_QW]) * B_QSCALE).astype(BF16)
    kb_out[:, B_QW * h:B_QW * h + B_NOPE] = kv[:, 2 * LANES * h:2 * LANES * h + B_NOPE].astype(BF16)
    kb_out[:, B_QW * h + B_NOPE:B_QW * (h + 1)] = k_rope
    vb_out[:, B_QW * h:B_QW * h + B_VDIM] = kv[:, 2 * LANES * h + B_NOPE:2 * LANES * (h + 1)].astype(BF16)
    vb_out[:, B_QW * h + B_VDIM:B_QW * (h + 1)] = ones


def _layer_spec(a, layer):
  return pl.BlockSpec((None,) + a.shape[1:], lambda i: (layer,) + (0,) * (a.ndim - 1),
                      pipeline_mode=pl.Buffered(1))


def _in_proj(h, w, layer, positions=None, tables=None):
  n = h.shape[0]
  t = TOKEN_TILE
  tok = lambda width: pl.BlockSpec((t, width), lambda i: (i, 0))
  w_in_block = lambda j: pl.BlockSpec((None, D_MODEL, A_WIDTH), lambda i: (layer, 0, j),
                                      pipeline_mode=pl.Buffered(1))
  small = [w["gmix"], w["wgb"], w["wmla"], w["gq"], w["gkv"], w["wuq"], w["wukv"]]
  in_specs = ([tok(D_MODEL), _layer_spec(small[0], layer)] + [w_in_block(j) for j in range(4)]
              + [_layer_spec(a, layer) for a in small[1:]])
  operands = [h, small[0]] + [w["w_qkvg"]] * 4 + small[1:]
  out_widths = [(A_WIDTH, BF16), (A_WIDTH, BF16), (A_WIDTH, BF16), (A_WIDTH, F32), (B_WIDTH, F32),
                (B_HEADS * B_QW, BF16), (B_HEADS * B_QW, BF16), (B_HEADS * B_QW, BF16)]
  make_tables = tables is None
  if make_tables:
    invf = _inv_freq_column()
    operands += [positions.reshape(n // t, 1, t), invf]
    in_specs += [pl.BlockSpec((1, 1, t), lambda i: (i, 0, 0)), pl.BlockSpec(invf.shape, lambda i: (0, 0))]
    out_widths += [(LANES, F32), (LANES, F32)]
  else:
    operands += list(tables)
    in_specs += [tok(LANES), tok(LANES)]
  return pl.pallas_call(
      functools.partial(_in_proj_kernel, make_tables=make_tables),
      grid=(n // t,),
      in_specs=in_specs,
      out_specs=[tok(wd) for wd, _ in out_widths],
      out_shape=[jax.ShapeDtypeStruct((n, wd), dt) for wd, dt in out_widths],
      compiler_params=pltpu.CompilerParams(dimension_semantics=("parallel",),
                                           vmem_limit_bytes=VMEM_LIMIT),
      name="in_proj",
  )(*operands)


def _attn_a_kernel(q_ref, k_ref, v_ref, gate_ref, tab_ref, o_ref):
  seq_len = q_ref.shape[1]
  lane = lax.broadcasted_iota(jnp.int32, (1, LANES), 1)
  head_rows = ((lane < A_HEAD_DIM).astype(BF16), (lane >= A_HEAD_DIM).astype(BF16))
  lo_mask = lax.broadcasted_iota(jnp.int32, (Q_BLOCK, LANES), 1) < A_HEAD_DIM
  ones = jnp.ones((A_WINDOW, LANES), BF16)

  def window(qb):
    k0 = max(0, Q_BLOCK * qb - A_LEFT)
    return k0, Q_BLOCK * (qb + 1) - k0

  half = Q_BLOCK // 2

  def panels(qb):
    _, nk = window(qb)
    first_live = half if nk == A_WINDOW else 0
    return (slice(0, half), slice(0, nk - half)), (slice(half, Q_BLOCK), slice(first_live, nk))

  def scores(qb, pr, hh):
    k0, nk = window(qb)
    ls = slice(LANES * pr, LANES * (pr + 1))
    qp = q_ref[0, Q_BLOCK * qb:Q_BLOCK * (qb + 1), ls]
    s = _dot_nt(qp * head_rows[hh], k_ref[0, k0:k0 + nk, ls])
    col0 = A_WINDOW - nk
    return [s[rs, cs] + tab_ref[2 * pr + hh, rs, col0 + cs.start:col0 + cs.stop] for rs, cs in panels(qb)]

  def weighted(qb, pr, s_panels):
    k0, nk = window(qb)
    vp = jnp.concatenate([v_ref[0, k0:k0 + nk, LANES * pr:LANES * (pr + 1)], ones[:nk]], axis=1)
    rows = []
    for s, (_, cs) in zip(s_panels, panels(qb)):
      e = jnp.exp2(s - jnp.max(s, axis=-1, keepdims=True)).astype(BF16)
      dead = [jnp.zeros((half, w), BF16) for w in (cs.start, nk - cs.stop)]
      rows.append(jnp.concatenate([x for x in (dead[0], e, dead[1]) if x.shape[1]], axis=1))
    pv = _dot(jnp.concatenate(rows, axis=0), vp)
    return pv[:, :LANES] / pv[:, LANES:]

  items = [(qb, pr, hh) for qb in reversed(range(seq_len // Q_BLOCK))
           for pr in range(A_STEP_PAIRS) for hh in range(2)]
  depth = 2
  pending = [scores(*it) for it in items[:depth]]
  outs = []
  for i, (qb, pr, hh) in enumerate(items):
    if i + depth < len(items):
      pending.append(scores(*items[i + depth]))
    outs.append(weighted(qb, pr, pending.pop(0)))
    if hh == 1:
      rows = slice(Q_BLOCK * qb, Q_BLOCK * (qb + 1))
      ls = slice(LANES * pr, LANES * (pr + 1))
      o = jnp.where(lo_mask, outs[0], outs[1])
      o_ref[0, rows, ls] = (o * _silu(gate_ref[0, rows, ls])).astype(BF16)
      outs = []


def _attn_a(q, k, v, gate, tab):
  b, s, _ = q.shape
  pair = pl.BlockSpec((1, s, A_STEP_PAIRS * LANES), lambda p, i: (i, 0, p))
  return pl.pallas_call(
      _attn_a_kernel,
      grid=(A_HEADS // (2 * A_STEP_PAIRS), b),
      in_specs=[pair, pair, pair, pair,
                pl.BlockSpec((2 * A_STEP_PAIRS, Q_BLOCK, A_WINDOW), lambda p, i: (p, 0, 0))],
      out_specs=pair,
      out_shape=jax.ShapeDtypeStruct((b, s, A_WIDTH), BF16),
      compiler_params=pltpu.CompilerParams(dimension_semantics=("parallel", "parallel"),
                                           vmem_limit_bytes=VMEM_LIMIT),
      name="attn_a",
  )(q, k, v, gate, tab)


def _attn_b_kernel(q_ref, k_ref, v_ref, gate_ref, o_ref):
  seq_len = q_ref.shape[1]
  qi = lax.broadcasted_iota(jnp.int32, (Q_BLOCK, Q_BLOCK), 0)
  kj = lax.broadcasted_iota(jnp.int32, (Q_BLOCK, Q_BLOCK), 1)
  diag_ok = (kj // CHUNK) <= (qi // CHUNK)

  def scores(qb, h):
    rows = slice(Q_BLOCK * qb, Q_BLOCK * (qb + 1))
    wide = slice(B_QW * h, B_QW * (h + 1))
    qh = q_ref[0, rows, wide]
    s_d = jnp.where(diag_ok, _dot_nt(qh, k_ref[0, rows, wide]), -jnp.inf)
    s_p = _dot_nt(qh, k_ref[0, :Q_BLOCK * qb, wide]) if qb else None
    return s_d, s_p

  def finish(qb, h, s_d, s_p):
    rows = slice(Q_BLOCK * qb, Q_BLOCK * (qb + 1))
    wide = slice(B_QW * h, B_QW * (h + 1))
    narrow = slice(B_VDIM * h, B_VDIM * (h + 1))
    m = jnp.max(s_d, axis=-1, keepdims=True)
    if qb:
      m = jnp.maximum(m, jnp.max(s_p, axis=-1, keepdims=True))
    pv = _dot(jnp.exp2(s_d - m).astype(BF16), v_ref[0, rows, wide])
    if qb:
      pv = pv + _dot(jnp.exp2(s_p - m).astype(BF16), v_ref[0, :Q_BLOCK * qb, wide])
    o = pv[:, :B_VDIM] / pv[:, B_VDIM:]
    o_ref[0, rows, narrow] = (o * _silu(gate_ref[0, rows, narrow])).astype(BF16)

  items = [(qb, h) for qb in reversed(range(seq_len // Q_BLOCK)) for h in range(B_STEP_HEADS)]
  depth = 2
  pending = [scores(*it) for it in items[:depth]]
  for i, it in enumerate(items):
    if i + depth < len(items):
      pending.append(scores(*items[i + depth]))
    finish(*it, *pending.pop(0))


def _attn_b(q, k, v, gate):
  b, s, _ = q.shape
  wide = pl.BlockSpec((1, s, B_STEP_HEADS * B_QW), lambda i, h: (i, 0, h))
  narrow = pl.BlockSpec((1, s, B_STEP_HEADS * B_VDIM), lambda i, h: (i, 0, h))
  return pl.pallas_call(
      _attn_b_kernel,
      grid=(b, B_HEADS // B_STEP_HEADS),
      in_specs=[wide, wide, wide, narrow],
      out_specs=narrow,
      out_shape=jax.ShapeDtypeStruct((b, s, B_WIDTH), BF16),
      compiler_params=pltpu.CompilerParams(dimension_semantics=("parallel", "parallel"),
                                           vmem_limit_bytes=VMEM_LIMIT),
      name="attn_b",
  )(q, k, v, gate)


def _out_ple_kernel(ma_ref, mb_ref, h_ref, p_ref, woa_ref, wob_ref, gple_ref, wpg_ref, bpg_ref,
                    wpe_ref, gfin_ref, o_ref, *, final_norm):
  n_sub = 4
  sub = h_ref.shape[0] // n_sub

  def stage1(j):
    rs = slice(sub * j, sub * (j + 1))
    h1 = h_ref[rs, :] + (_dot(ma_ref[rs, :], woa_ref[...]) + _dot(mb_ref[rs, :], wob_ref[...]))
    pe = _dot(p_ref[rs, :].astype(BF16), wpe_ref[...])
    return h1, pe

  def stage2(j, h1, pe):
    rs = slice(sub * j, sub * (j + 1))
    xn = _rms(h1, gple_ref[...]).astype(BF16)
    z = _dot(xn, wpg_ref[...]) + bpg_ref[...]
    h2 = h1 + pe * (1.0 / (1.0 + jnp.exp(-z)))
    o_ref[rs, :] = _rms(h2, gfin_ref[...]) if final_norm else h2

  pending = stage1(0)
  for j in range(n_sub):
    nxt = stage1(j + 1) if j + 1 < n_sub else None
    stage2(j, *pending)
    pending = nxt


def _out_ple(ma, mb, h, p, w, layer, final_norm):
  n = h.shape[0]
  t = OUT_TOKEN_TILE
  tok = lambda width: pl.BlockSpec((t, width), lambda i: (i, 0))
  w_out_half = lambda j: pl.BlockSpec((None, A_WIDTH, D_MODEL), lambda i: (layer, j, 0))
  rest = [w["gple"], w["wpg"], w["bpg"], w["wpe"]]
  return pl.pallas_call(
      functools.partial(_out_ple_kernel, final_norm=final_norm),
      grid=(n // t,),
      in_specs=([tok(A_WIDTH), tok(B_WIDTH), tok(D_MODEL),
                 pl.BlockSpec((None, t, D_PLE), lambda i: (layer, i, 0)),
                 w_out_half(0), w_out_half(1)] + [_layer_spec(a, layer) for a in rest]
                + [pl.BlockSpec(w["gfin"].shape, lambda i: (0, 0))]),
      out_specs=tok(D_MODEL),
      out_shape=jax.ShapeDtypeStruct((n, D_MODEL), F32),
      compiler_params=pltpu.CompilerParams(dimension_semantics=("parallel",),
                                           vmem_limit_bytes=VMEM_LIMIT),
      name="out_ple",
  )(ma, mb, h, p, w["w_out"], w["w_out"], *rest, w["gfin"])


W_MLA0 = 4 * A_WIDTH
W_GB0 = W_MLA0 + Q_LORA + KV_LORA + B_ROPE
W_MLA_COLS = Q_LORA + KV_LORA + LANES


def _split_w_in_kernel(w_ref, qkvg_ref, gb_ref, mla_ref):
  x = w_ref[0]
  qkvg_ref[0] = x[:, :W_MLA0].astype(BF16)
  gb_ref[0] = x[:, W_GB0:].astype(BF16)
  mla = x[:, W_MLA0:W_MLA0 + W_MLA_COLS]
  keep = lax.broadcasted_iota(jnp.int32, mla.shape, 1) < W_GB0 - W_MLA0
  mla_ref[0] = jnp.where(keep, mla, 0.0).astype(BF16)


def _split_w_in(w_in):
  depth, rows, cols = w_in.shape
  r = 256
  spec = lambda width: pl.BlockSpec((1, r, width), lambda l, i: (l, i, 0))
  widths = (W_MLA0, B_WIDTH, W_MLA_COLS)
  return pl.pallas_call(
      _split_w_in_kernel,
      grid=(depth, rows // r),
      in_specs=[spec(cols)],
      out_specs=[spec(wd) for wd in widths],
      out_shape=[jax.ShapeDtypeStruct((depth, rows, wd), BF16) for wd in widths],
      compiler_params=pltpu.CompilerParams(dimension_semantics=("parallel", "parallel")),
      name="split_w_in",
  )(w_in)


def _prep_params(norm_mix, w_in, g_q, w_uq, g_kv, w_ukv, w_out, norm_ple, w_pe, w_pg, b_pg,
                 norm_final):
  depth = w_in.shape[0]
  w_qkvg, wgb, wmla = _split_w_in(w_in)
  uq = w_uq.reshape(depth, Q_LORA, B_HEADS, B_NOPE + B_ROPE)
  zpad = jnp.zeros((depth, Q_LORA, B_HEADS, LANES - B_ROPE), F32)
  wuq = jnp.concatenate([uq, zpad], axis=-1)
  return {
      "gmix": norm_mix[:, None, :],
      "w_qkvg": w_qkvg, "wgb": wgb, "wmla": wmla,
      "gq": g_q[:, None, :], "gkv": g_kv[:, None, :],
      "wuq": wuq.reshape(depth, Q_LORA, B_HEADS * B_QW).astype(BF16),
      "wukv": w_ukv.astype(BF16),
      "w_out": w_out.astype(BF16),
      "gple": norm_ple[:, None, :], "wpg": w_pg.astype(BF16), "bpg": b_pg[:, None, :],
      "wpe": w_pe.astype(BF16),
      "gfin": norm_final[None, :],
  }


def kernel(x, p, positions, norm_mix, w_in, rel_bias, g_q, w_uq, g_kv, w_ukv, w_out,
           norm_ple, w_pe, w_pg, b_pg, norm_final):
  b, s, d = x.shape
  depth = w_in.shape[0]
  n = b * s
  w = _prep_params(norm_mix, w_in, g_q, w_uq, g_kv, w_ukv, w_out, norm_ple, w_pe, w_pg, b_pg,
                   norm_final)
  p = p.reshape(depth, n, D_PLE)
  h = x.reshape(n, d)
  seq = lambda a: a.reshape(b, s, a.shape[-1])
  tables = None
  for i in range(depth):
    tab = _bias_tables(rel_bias[i])
    if tables is None:
      *outs, cos, sin = _in_proj(h, w, i, positions=positions)
      tables = (cos, sin)
    else:
      outs = _in_proj(h, w, i, tables=tables)
    qa, ka, va, ga, gb, qb, kb, vb = outs
    ma = _attn_a(seq(qa), seq(ka), seq(va), seq(ga), tab)
    mb = _attn_b(seq(qb), seq(kb), seq(vb), seq(gb))
    h = _out_ple(ma.reshape(n, A_WIDTH), mb.reshape(n, B_WIDTH), h, p, w, i,
                 final_norm=(i == depth - 1))
  return h.reshape(b, s, d)
```

```python
import functools
import math

import jax
import jax.numpy as jnp
from jax import lax
from jax.experimental import pallas as pl
from jax.experimental.pallas import tpu as pltpu

D_MODEL = 1024
CHUNK = 64
D_PLE = 256
EPS = 1e-6
A_HEADS = 8
A_HEAD_DIM = 64
A_WIDTH = A_HEADS * A_HEAD_DIM
A_LEFT_CHUNKS = 8
REL_MAX = 128
B_HEADS = 4
B_NOPE = 128
B_ROPE = 64
B_VDIM = 128
B_WIDTH = B_HEADS * B_VDIM
Q_LORA = 256
KV_LORA = 128
ROPE_THETA = 10000.0

LANES = 128
TOKEN_TILE = 1024
OUT_TOKEN_TILE = 1024
Q_BLOCK = 256
A_STEP_PAIRS = 2
B_STEP_HEADS = 2
A_LEFT = A_LEFT_CHUNKS * CHUNK
A_WINDOW = A_LEFT + Q_BLOCK
B_QW = B_NOPE + LANES
NEG_BIG = -1e30
LOG2E = math.log2(math.e)
B_QSCALE = (B_NOPE + B_ROPE) ** -0.5 * LOG2E
VMEM_LIMIT = 56 * 1024 * 1024

BF16 = jnp.bfloat16
F32 = jnp.float32


def _dot(a, b):
  return jnp.dot(a, b, preferred_element_type=F32)


def _dot_nt(a, b):
  return lax.dot_general(a, b, (((1,), (1,)), ((), ())), preferred_element_type=F32)


def _rms(x, g):
  y = x * lax.rsqrt(jnp.mean(x * x, axis=-1, keepdims=True) + EPS)
  return y * g


def _silu(g):
  return g * (1.0 / (1.0 + jnp.exp(-g)))


def _inv_freq_column():
  inv_freq = ROPE_THETA ** (-jnp.arange(0, B_ROPE, 2, dtype=F32) / B_ROPE)
  return inv_freq[:, None]


def _rope_tile(pos_row, inv_freq_col):
  t = pos_row.shape[1]
  ang = pos_row.astype(F32) * inv_freq_col
  c, s = jnp.cos(ang), jnp.sin(ang)
  zeros = jnp.zeros((LANES - B_ROPE, t), F32)
  return jnp.concatenate([c, c, zeros], axis=0).T, jnp.concatenate([-s, s, zeros], axis=0).T


G_LEN = 1024


def _bias_table_kernel(g_ref, tab_ref):
  g = jnp.broadcast_to(g_ref[0], (Q_BLOCK, G_LEN))
  t = pltpu.roll(g, G_LEN - Q_BLOCK, 1, stride=1, stride_axis=0)[:, :A_WINDOW]
  qi = lax.broadcasted_iota(jnp.int32, (Q_BLOCK, A_WINDOW), 0)
  kk = lax.broadcasted_iota(jnp.int32, (Q_BLOCK, A_WINDOW), 1)
  qc = (qi + A_LEFT) // CHUNK
  kc = kk // CHUNK
  valid = (kc <= qc) & (kc >= qc - A_LEFT_CHUNKS)
  tab_ref[0] = jnp.where(valid, t * LOG2E, NEG_BIG)


def _bias_tables(rel_bias):
  flat = A_LEFT + Q_BLOCK - REL_MAX
  rev = rel_bias[:, ::-1]
  g = jnp.concatenate([
      jnp.broadcast_to(rel_bias[:, -1:], (A_HEADS, flat)),
      rev[:, : 2 * REL_MAX],
      jnp.broadcast_to(rel_bias[:, :1], (A_HEADS, G_LEN - flat - 2 * REL_MAX)),
  ], axis=1)[:, None, :]
  return pl.pallas_call(
      _bias_table_kernel,
      grid=(A_HEADS,),
      in_specs=[pl.BlockSpec((1, 1, G_LEN), lambda h: (h, 0, 0))],
      out_specs=pl.BlockSpec((1, Q_BLOCK, A_WINDOW), lambda h: (h, 0, 0)),
      out_shape=jax.ShapeDtypeStruct((A_HEADS, Q_BLOCK, A_WINDOW), F32),
      compiler_params=pltpu.CompilerParams(dimension_semantics=("parallel",)),
      name="bias_table",
  )(g)


def _in_proj_kernel(h_ref, gmix_ref, wq_ref, wk_ref, wv_ref, wga_ref, wgb_ref, wmla_ref,
                    gq_ref, gkv_ref, wuq_ref, wukv_ref, *refs, make_tables):
  if make_tables:
    (pos_ref, invf_ref, q_out, k_out, v_out, ga_out, gb_out, qb_out, kb_out, vb_out,
     cos_out, sin_out) = refs
  else:
    cos_ref, sin_ref, q_out, k_out, v_out, ga_out, gb_out, qb_out, kb_out, vb_out = refs
  xn = _rms(h_ref[...], gmix_ref[...]).astype(BF16)
  zm = _dot_nt(xn, wmla_ref[...])
  cqn = _rms(zm[:, :Q_LORA], gq_ref[...]).astype(BF16)
  ckvn = _rms(zm[:, Q_LORA:Q_LORA + KV_LORA], gkv_ref[...]).astype(BF16)
  qall = _dot(cqn, wuq_ref[...])
  kv = _dot(ckvn, wukv_ref[...])

  q_out[...] = (_dot_nt(xn, wq_ref[...]) * (A_HEAD_DIM ** -0.5 * LOG2E)).astype(BF16)
  k_out[...] = _dot_nt(xn, wk_ref[...]).astype(BF16)
  v_out[...] = _dot_nt(xn, wv_ref[...]).astype(BF16)
  ga_out[...] = _dot_nt(xn, wga_ref[...])
  gb_out[...] = _dot_nt(xn, wgb_ref[...])

  if make_tables:
    cos, sin = _rope_tile(pos_ref[0], invf_ref[...])
    cos_out[...] = cos
    sin_out[...] = sin
  else:
    cos = cos_ref[...]
    sin = sin_ref[...]
  first_half = lax.broadcasted_iota(jnp.int32, (h_ref.shape[0], LANES), 1) < B_ROPE // 2

  def rope(g):
    swapped = jnp.where(first_half, pltpu.roll(g, LANES - B_ROPE // 2, 1), pltpu.roll(g, B_ROPE // 2, 1))
    return g * cos + swapped * sin

  k_rope = rope(zm[:, Q_LORA + KV_LORA:]).astype(BF16)
  ones = jnp.ones((h_ref.shape[0], B_QW - B_VDIM), BF16)
  for h in range(B_HEADS):
    qb = B_QW * h
    qb_out[:, qb:qb + B_NOPE] = (qall[:, qb:qb + B_NOPE] * B_QSCALE).astype(BF16)
    qb_out[:, qb + B_NOPE:qb + B_QW] = (rope(qall[:, qb + B_NOPE:qb + B_QW]) * B_QSCALE).astype(BF16)
    kb_out[:, B_QW * h:B_QW * h + B_NOPE] = kv[:, 2 * LANES * h:2 * LANES * h + B_NOPE].astype(BF16)
    kb_out[:, B_QW * h + B_NOPE:B_QW * (h + 1)] = k_rope
    vb_out[:, B_QW * h:B_QW * h + B_VDIM] = kv[:, 2 * LANES * h + B_NOPE:2 * LANES * (h + 1)].astype(BF16)
    vb_out[:, B_QW * h + B_VDIM:B_QW * (h + 1)] = ones


def _layer_spec(a, layer):
  return pl.BlockSpec((None,) + a.shape[1:], lambda i: (layer,) + (0,) * (a.ndim - 1),
                      pipeline_mode=pl.Buffered(1))


def _in_proj(h, w, layer, positions=None, tables=None):
  n = h.shape[0]
  t = TOKEN_TILE
  tok = lambda width: pl.BlockSpec((t, width), lambda i: (i, 0))
  w_in_block = lambda j: pl.BlockSpec((None, A_WIDTH, D_MODEL), lambda i: (layer, j, 0),
                                      pipeline_mode=pl.Buffered(1))
  small = [w["gmix"], w["wgb"], w["wmla"], w["gq"], w["gkv"], w["wuq"], w["wukv"]]
  in_specs = ([tok(D_MODEL), _layer_spec(small[0], layer)] + [w_in_block(j) for j in range(4)]
              + [_layer_spec(a, layer) for a in small[1:]])
  operands = [h, small[0]] + [w["w_t"]] * 4 + small[1:]
  out_widths = [(A_WIDTH, BF16), (A_WIDTH, BF16), (A_WIDTH, BF16), (A_WIDTH, F32), (B_WIDTH, F32),
                (B_HEADS * B_QW, BF16), (B_HEADS * B_QW, BF16), (B_HEADS * B_QW, BF16)]
  make_tables = tables is None
  if make_tables:
    invf = _inv_freq_column()
    operands += [positions.reshape(n // t, 1, t), invf]
    in_specs += [pl.BlockSpec((1, 1, t), lambda i: (i, 0, 0)), pl.BlockSpec(invf.shape, lambda i: (0, 0))]
    out_widths += [(LANES, F32), (LANES, F32)]
  else:
    operands += list(tables)
    in_specs += [tok(LANES), tok(LANES)]
  return pl.pallas_call(
      functools.partial(_in_proj_kernel, make_tables=make_tables),
      grid=(n // t,),
      in_specs=in_specs,
      out_specs=[tok(wd) for wd, _ in out_widths],
      out_shape=[jax.ShapeDtypeStruct((n, wd), dt) for wd, dt in out_widths],
      compiler_params=pltpu.CompilerParams(dimension_semantics=("parallel",),
                                           vmem_limit_bytes=VMEM_LIMIT),
      name="in_proj",
  )(*operands)


def _attn_a_kernel(q_ref, k_ref, v_ref, gate_ref, tab_ref, o_ref):
  seq_len = q_ref.shape[1]
  lane = lax.broadcasted_iota(jnp.int32, (1, LANES), 1)
  head_rows = ((lane < A_HEAD_DIM).astype(BF16), (lane >= A_HEAD_DIM).astype(BF16))
  lo_mask = lax.broadcasted_iota(jnp.int32, (Q_BLOCK, LANES), 1) < A_HEAD_DIM
  ones = jnp.ones((A_WINDOW, LANES), BF16)

  def window(qb):
    k0 = max(0, Q_BLOCK * qb - A_LEFT)
    return k0, Q_BLOCK * (qb + 1) - k0

  half = Q_BLOCK // 2

  def panels(qb):
    _, nk = window(qb)
    first_live = half if nk == A_WINDOW else 0
    return (slice(0, half), slice(0, nk - half)), (slice(half, Q_BLOCK), slice(first_live, nk))

  def scores(qb, pr, hh):
    k0, nk = window(qb)
    ls = slice(LANES * pr, LANES * (pr + 1))
    qp = q_ref[0, Q_BLOCK * qb:Q_BLOCK * (qb + 1), ls]
    s = _dot_nt(qp * head_rows[hh], k_ref[0, k0:k0 + nk, ls])
    col0 = A_WINDOW - nk
    return [s[rs, cs] + tab_ref[2 * pr + hh, rs, col0 + cs.start:col0 + cs.stop] for rs, cs in panels(qb)]

  def weighted(qb, pr, s_panels):
    k0, nk = window(qb)
    vp = jnp.concatenate([v_ref[0, k0:k0 + nk, LANES * pr:LANES * (pr + 1)], ones[:nk]], axis=1)
    rows = []
    for s, (_, cs) in zip(s_panels, panels(qb)):
      e = jnp.exp2(s - jnp.max(s, axis=-1, keepdims=True)).astype(BF16)
      dead = [jnp.zeros((half, w), BF16) for w in (cs.start, nk - cs.stop)]
      rows.append(jnp.concatenate([x for x in (dead[0], e, dead[1]) if x.shape[1]], axis=1))
    pv = _dot(jnp.concatenate(rows, axis=0), vp)
    return pv[:, :LANES] / pv[:, LANES:]

  items = [(qb, pr, hh) for qb in reversed(range(seq_len // Q_BLOCK))
           for pr in range(A_STEP_PAIRS) for hh in range(2)]
  depth = 2
  pending = [scores(*it) for it in items[:depth]]
  outs = []
  for i, (qb, pr, hh) in enumerate(items):
    if i + depth < len(items):
      pending.append(scores(*items[i + depth]))
    outs.append(weighted(qb, pr, pending.pop(0)))
    if hh == 1:
      rows = slice(Q_BLOCK * qb, Q_BLOCK * (qb + 1))
      ls = slice(LANES * pr, LANES * (pr + 1))
      o = jnp.where(lo_mask, outs[0], outs[1])
      o_ref[0, rows, ls] = (o * _silu(gate_ref[0, rows, ls])).astype(BF16)
      outs = []


def _attn_a(q, k, v, gate, tab):
  b, s, _ = q.shape
  pair = pl.BlockSpec((1, s, A_STEP_PAIRS * LANES), lambda p, i: (i, 0, p))
  return pl.pallas_call(
      _attn_a_kernel,
      grid=(A_HEADS // (2 * A_STEP_PAIRS), b),
      in_specs=[pair, pair, pair, pair,
                pl.BlockSpec((2 * A_STEP_PAIRS, Q_BLOCK, A_WINDOW), lambda p, i: (p, 0, 0))],
      out_specs=pair,
      out_shape=jax.ShapeDtypeStruct((b, s, A_WIDTH), BF16),
      compiler_params=pltpu.CompilerParams(dimension_semantics=("parallel", "parallel"),
                                           vmem_limit_bytes=VMEM_LIMIT),
      name="attn_a",
  )(q, k, v, gate, tab)


def _attn_b_kernel(q_ref, k_ref, v_ref, gate_ref, o_ref):
  seq_len = q_ref.shape[1]
  qi = lax.broadcasted_iota(jnp.int32, (Q_BLOCK, Q_BLOCK), 0)
  kj = lax.broadcasted_iota(jnp.int32, (Q_BLOCK, Q_BLOCK), 1)
  diag_ok = (kj // CHUNK) <= (qi // CHUNK)

  def scores(qb, h):
    rows = slice(Q_BLOCK * qb, Q_BLOCK * (qb + 1))
    wide = slice(B_QW * h, B_QW * (h + 1))
    qh = q_ref[0, rows, wide]
    s_d = jnp.where(diag_ok, _dot_nt(qh, k_ref[0, rows, wide]), -jnp.inf)
    s_p = _dot_nt(qh, k_ref[0, :Q_BLOCK * qb, wide]) if qb else None
    return s_d, s_p

  def finish(qb, h, s_d, s_p):
    rows = slice(Q_BLOCK * qb, Q_BLOCK * (qb + 1))
    wide = slice(B_QW * h, B_QW * (h + 1))
    narrow = slice(B_VDIM * h, B_VDIM * (h + 1))
    m = jnp.max(s_d, axis=-1, keepdims=True)
    if qb:
      m = jnp.maximum(m, jnp.max(s_p, axis=-1, keepdims=True))
    pv = _dot(jnp.exp2(s_d - m).astype(BF16), v_ref[0, rows, wide])
    if qb:
      pv = pv + _dot(jnp.exp2(s_p - m).astype(BF16), v_ref[0, :Q_BLOCK * qb, wide])
    o = pv[:, :B_VDIM] / pv[:, B_VDIM:]
    o_ref[0, rows, narrow] = (o * _silu(gate_ref[0, rows, narrow])).astype(BF16)

  items = [(qb, h) for qb in reversed(range(seq_len // Q_BLOCK)) for h in range(B_STEP_HEADS)]
  depth = 2
  pending = [scores(*it) for it in items[:depth]]
  for i, it in enumerate(items):
    if i + depth < len(items):
      pending.append(scores(*items[i + depth]))
    finish(*it, *pending.pop(0))


def _attn_b(q, k, v, gate):
  b, s, _ = q.shape
  wide = pl.BlockSpec((1, s, B_STEP_HEADS * B_QW), lambda i, h: (i, 0, h))
  narrow = pl.BlockSpec((1, s, B_STEP_HEADS * B_VDIM), lambda i, h: (i, 0, h))
  return pl.pallas_call(
      _attn_b_kernel,
      grid=(b, B_HEADS // B_STEP_HEADS),
      in_specs=[wide, wide, wide, narrow],
      out_specs=narrow,
      out_shape=jax.ShapeDtypeStruct((b, s, B_WIDTH), BF16),
      compiler_params=pltpu.CompilerParams(dimension_semantics=("parallel", "parallel"),
                                           vmem_limit_bytes=VMEM_LIMIT),
      name="attn_b",
  )(q, k, v, gate)


def _out_ple_kernel(ma_ref, mb_ref, h_ref, p_ref, woa_ref, wob_ref, gple_ref, wpg_ref, bpg_ref,
                    wpe_ref, gfin_ref, o_ref, *, final_norm):
  n_sub = 4
  sub = h_ref.shape[0] // n_sub

  def stage1(j):
    rs = slice(sub * j, sub * (j + 1))
    h1 = h_ref[rs, :] + (_dot(ma_ref[rs, :], woa_ref[...]) + _dot(mb_ref[rs, :], wob_ref[...]))
    pe = _dot(p_ref[rs, :].astype(BF16), wpe_ref[...])
    return h1, pe

  def stage2(j, h1, pe):
    rs = slice(sub * j, sub * (j + 1))
    xn = _rms(h1, gple_ref[...]).astype(BF16)
    z = _dot(xn, wpg_ref[...]) + bpg_ref[...]
    h2 = h1 + pe * (1.0 / (1.0 + jnp.exp(-z)))
    o_ref[rs, :] = _rms(h2, gfin_ref[...]) if final_norm else h2

  pending = stage1(0)
  for j in range(n_sub):
    nxt = stage1(j + 1) if j + 1 < n_sub else None
    stage2(j, *pending)
    pending = nxt


def _out_ple(ma, mb, h, p, w, layer, final_norm):
  n = h.shape[0]
  t = OUT_TOKEN_TILE
  tok = lambda width: pl.BlockSpec((t, width), lambda i: (i, 0))
  w_out_half = lambda j: pl.BlockSpec((None, A_WIDTH, D_MODEL), lambda i: (layer, j, 0))
  rest = [w["gple"], w["wpg"], w["bpg"], w["wpe"]]
  return pl.pallas_call(
      functools.partial(_out_ple_kernel, final_norm=final_norm),
      grid=(n // t,),
      in_specs=([tok(A_WIDTH), tok(B_WIDTH), tok(D_MODEL),
                 pl.BlockSpec((None, t, D_PLE), lambda i: (layer, i, 0)),
                 w_out_half(0), w_out_half(1)] + [_layer_spec(a, layer) for a in rest]
                + [pl.BlockSpec(w["gfin"].shape, lambda i: (0, 0))]),
      out_specs=tok(D_MODEL),
      out_shape=jax.ShapeDtypeStruct((n, D_MODEL), F32),
      compiler_params=pltpu.CompilerParams(dimension_semantics=("parallel",),
                                           vmem_limit_bytes=VMEM_LIMIT),
      name="out_ple",
  )(ma, mb, h, p, w["w_out"], w["w_out"], *rest, w["gfin"])


def _prep_params(norm_mix, w_in, g_q, w_uq, g_kv, w_ukv, w_out, norm_ple, w_pe, w_pg, b_pg,
                 norm_final):
  depth = w_in.shape[0]
  w_t = jnp.swapaxes(w_in, 1, 2).astype(BF16)
  mla0 = 4 * A_WIDTH
  gb0 = mla0 + Q_LORA + KV_LORA + B_ROPE
  wmla = jnp.pad(w_t[:, mla0:gb0], ((0, 0), (0, LANES - B_ROPE), (0, 0)))
  uq =w_uq.reshape(depth, Q_LORA, B_HEADS, B_NOPE + B_ROPE)
  zpad = jnp.zeros((depth, Q_LORA, B_HEADS, LANES - B_ROPE), F32)
  wuq = jnp.concatenate([uq, zpad], axis=-1)
  return {
      "gmix": norm_mix[:, None, :],
      "w_t": w_t, "wgb": w_t[:, gb0:], "wmla": wmla,
      "gq": g_q[:, None, :], "gkv": g_kv[:, None, :],
      "wuq": wuq.reshape(depth, Q_LORA, B_HEADS * B_QW).astype(BF16),
      "wukv": w_ukv.astype(BF16),
      "w_out": w_out.astype(BF16),
      "gple": norm_ple[:, None, :], "wpg": w_pg.astype(BF16), "bpg": b_pg[:, None, :],
      "wpe": w_pe.astype(BF16),
      "gfin": norm_final[None, :],
  }


def kernel(x, p, positions, norm_mix, w_in, rel_bias, g_q, w_uq, g_kv, w_ukv, w_out,
           norm_ple, w_pe, w_pg, b_pg, norm_final):
  b, s, d = x.shape
  depth = w_in.shape[0]
  n = b * s
  w = _prep_params(norm_mix, w_in, g_q, w_uq, g_kv, w_ukv, w_out, norm_ple, w_pe, w_pg, b_pg,
                   norm_final)
  p = p.reshape(depth, n, D_PLE)
  h = x.reshape(n, d)
  seq = lambda a: a.reshape(b, s, a.shape[-1])
  tables = None
  for i in range(depth):
    tab = _bias_tables(rel_bias[i])
    if tables is None:
      *outs, cos, sin = _in_proj(h, w, i, positions=positions)
      tables = (cos, sin)
    else:
      outs = _in_proj(h, w, i, tables=tables)
    qa, ka, va, ga, gb, qb, kb, vb = outs
    ma = _attn_a(seq(qa), seq(ka), seq(va), seq(ga), tab)
    mb = _attn_b(seq(qb), seq(kb), seq(vb), seq(gb))
    h = _out_ple(ma.reshape(n, A_WIDTH), mb.reshape(n, B_WIDTH), h, p, w, i,
                 final_norm=(i == depth - 1))
  return h.reshape(b, s, d)
```

```python
import functools
import math

import jax
import jax.numpy as jnp
from jax import lax
from jax.experimental import pallas as pl
from jax.experimental.pallas import tpu as pltpu

D_MODEL = 1024
CHUNK = 64
D_PLE = 256
EPS = 1e-6
A_HEADS = 8
A_HEAD_DIM = 64
A_WIDTH = A_HEADS * A_HEAD_DIM
A_LEFT_CHUNKS = 8
REL_MAX = 128
B_HEADS = 4
B_NOPE = 128
B_ROPE = 64
B_VDIM = 128
B_WIDTH = B_HEADS * B_VDIM
Q_LORA = 256
KV_LORA = 128
ROPE_THETA = 10000.0

LANES = 128
TOKEN_TILE = 1024
OUT_TOKEN_TILE = 1024
Q_BLOCK = 256
A_STEP_PAIRS = 2
B_STEP_HEADS = 2
A_LEFT = A_LEFT_CHUNKS * CHUNK
A_WINDOW = A_LEFT + Q_BLOCK
B_QW = B_NOPE + LANES
NEG_BIG = -1e30
LOG2E = math.log2(math.e)
B_QSCALE = (B_NOPE + B_ROPE) ** -0.5 * LOG2E
VMEM_LIMIT = 56 * 1024 * 1024

BF16 = jnp.bfloat16
F32 = jnp.float32


def _dot(a, b):
  return jnp.dot(a, b, preferred_element_type=F32)


def _dot_nt(a, b):
  return lax.dot_general(a, b, (((1,), (1,)), ((), ())), preferred_element_type=F32)


def _rms(x, g):
  y = x * lax.rsqrt(jnp.mean(x * x, axis=-1, keepdims=True) + EPS)
  return y * g


def _silu(g):
  return g * (1.0 / (1.0 + jnp.exp(-g)))


def _inv_freq_column():
  inv_freq = ROPE_THETA ** (-jnp.arange(0, B_ROPE, 2, dtype=F32) / B_ROPE)
  return inv_freq[:, None]


def _rope_tile(pos_row, inv_freq_col):
  t = pos_row.shape[1]
  ang = pos_row.astype(F32) * inv_freq_col
  c, s = jnp.cos(ang), jnp.sin(ang)
  zeros = jnp.zeros((LANES - B_ROPE, t), F32)
  return jnp.concatenate([c, c, zeros], axis=0).T, jnp.concatenate([-s, s, zeros], axis=0).T


G_LEN = 1024


def _bias_table_kernel(g_ref, tab_ref):
  qi = lax.broadcasted_iota(jnp.int32, (Q_BLOCK, A_WINDOW), 0)
  kk = lax.broadcasted_iota(jnp.int32, (Q_BLOCK, A_WINDOW), 1)
  qc = (qi + A_LEFT) // CHUNK
  kc = kk // CHUNK
  valid = (kc <= qc) & (kc >= qc - A_LEFT_CHUNKS)
  for h in range(A_HEADS):
    g = jnp.broadcast_to(g_ref[0, h], (Q_BLOCK, G_LEN))
    t = pltpu.roll(g, G_LEN - Q_BLOCK, 1, stride=1, stride_axis=0)[:, :A_WINDOW]
    tab_ref[0, h] = jnp.where(valid, t * LOG2E, NEG_BIG)


def _bias_tables(rel_bias):
  depth = rel_bias.shape[0]
  flat = A_LEFT + Q_BLOCK - REL_MAX
  rev = rel_bias[..., ::-1]
  g = jnp.concatenate([
      jnp.broadcast_to(rel_bias[..., -1:], (depth, A_HEADS, flat)),
      rev[..., : 2 * REL_MAX],
      jnp.broadcast_to(rel_bias[..., :1], (depth, A_HEADS, G_LEN - flat - 2 * REL_MAX)),
  ], axis=-1)[:, :, None, :]
  return pl.pallas_call(
      _bias_table_kernel,
      grid=(depth,),
      in_specs=[pl.BlockSpec((1, A_HEADS, 1, G_LEN), lambda l: (l, 0, 0, 0))],
      out_specs=pl.BlockSpec((1, A_HEADS, Q_BLOCK, A_WINDOW), lambda l: (l, 0, 0, 0)),
      out_shape=jax.ShapeDtypeStruct((depth, A_HEADS, Q_BLOCK, A_WINDOW), F32),
      compiler_params=pltpu.CompilerParams(dimension_semantics=("parallel",)),
      name="bias_table",
  )(g)


def _in_proj_kernel(h_ref, gmix_ref, wq_ref, wk_ref, wv_ref, wga_ref, wgb_ref, wmla_ref,
                    gq_ref, gkv_ref, wuq_ref, wukv_ref, *refs, make_tables):
  if make_tables:
    (pos_ref, invf_ref, q_out, k_out, v_out, ga_out, gb_out, qb_out, kb_out, vb_out,
     cos_out, sin_out) = refs
  else:
    cos_ref, sin_ref, q_out, k_out, v_out, ga_out, gb_out, qb_out, kb_out, vb_out = refs
  xn = _rms(h_ref[...], gmix_ref[...]).astype(BF16)
  zm = _dot_nt(xn, wmla_ref[...])
  cqn = _rms(zm[:, :Q_LORA], gq_ref[...]).astype(BF16)
  ckvn = _rms(zm[:, Q_LORA:Q_LORA + KV_LORA], gkv_ref[...]).astype(BF16)
  qall = _dot(cqn, wuq_ref[...])
  kv = _dot(ckvn, wukv_ref[...])

  q_out[...] = (_dot_nt(xn, wq_ref[...]) * (A_HEAD_DIM ** -0.5 * LOG2E)).astype(BF16)
  k_out[...] = _dot_nt(xn, wk_ref[...]).astype(BF16)
  v_out[...] = _dot_nt(xn, wv_ref[...]).astype(BF16)
  ga_out[...] = _dot_nt(xn, wga_ref[...])
  gb_out[...] = _dot_nt(xn, wgb_ref[...])

  if make_tables:
    cos, sin = _rope_tile(pos_ref[0], invf_ref[...])
    cos_out[...] = cos
    sin_out[...] = sin
  else:
    cos = cos_ref[...]
    sin = sin_ref[...]
  first_half = lax.broadcasted_iota(jnp.int32, (h_ref.shape[0], LANES), 1) < B_ROPE // 2

  def rope(g):
    swapped = jnp.where(first_half, pltpu.roll(g, LANES - B_ROPE // 2, 1), pltpu.roll(g, B_ROPE // 2, 1))
    return g * cos + swapped * sin

  k_rope = rope(zm[:, Q_LORA + KV_LORA:]).astype(BF16)
  ones = jnp.ones((h_ref.shape[0], B_QW - B_VDIM), BF16)
  for h in range(B_HEADS):
    qb = B_QW * h
    qb_out[:, qb:qb + B_NOPE] = (qall[:, qb:qb + B_NOPE] * B_QSCALE).astype(BF16)
    qb_out[:, qb + B_NOPE:qb + B_QW] = (rope(qall[:, qb + B_NOPE:qb + B_QW]) * B_QSCALE).astype(BF16)
    kb_out[:, B_QW * h:B_QW * h + B_NOPE] = kv[:, 2 * LANES * h:2 * LANES * h + B_NOPE].astype(BF16)
    kb_out[:, B_QW * h + B_NOPE:B_QW * (h + 1)] = k_rope
    vb_out[:, B_QW * h:B_QW * h + B_VDIM] = kv[:, 2 * LANES * h + B_NOPE:2 * LANES * (h + 1)].astype(BF16)
    vb_out[:, B_QW * h + B_VDIM:B_QW * (h + 1)] = ones


def _layer_spec(a, layer):
  return pl.BlockSpec((None,) + a.shape[1:], lambda i: (layer,) + (0,) * (a.ndim - 1),
                      pipeline_mode=pl.Buffered(1))


def _in_proj(h, w, layer, positions=None, tables=None):
  n = h.shape[0]
  t = TOKEN_TILE
  tok = lambda width: pl.BlockSpec((t, width), lambda i: (i, 0))
  w_in_block = lambda j: pl.BlockSpec((None, A_WIDTH, D_MODEL), lambda i: (layer, j, 0),
                                      pipeline_mode=pl.Buffered(1))
  small = [w["gmix"], w["wgb"], w["wmla"], w["gq"], w["gkv"], w["wuq"], w["wukv"]]
  in_specs = ([tok(D_MODEL), _layer_spec(small[0], layer)] + [w_in_block(j) for j in range(4)]
              + [_layer_spec(a, layer) for a in small[1:]])
  operands = [h, small[0]] + [w["w_t"]] * 4 + small[1:]
  out_widths = [(A_WIDTH, BF16), (A_WIDTH, BF16), (A_WIDTH, BF16), (A_WIDTH, F32), (B_WIDTH, F32),
                (B_HEADS * B_QW, BF16), (B_HEADS * B_QW, BF16), (B_HEADS * B_QW, BF16)]
  make_tables = tables is None
  if make_tables:
    invf = _inv_freq_column()
    operands += [positions.reshape(n // t, 1, t), invf]
    in_specs += [pl.BlockSpec((1, 1, t), lambda i: (i, 0, 0)), pl.BlockSpec(invf.shape, lambda i: (0, 0))]
    out_widths += [(LANES, F32), (LANES, F32)]
  else:
    operands += list(tables)
    in_specs += [tok(LANES), tok(LANES)]
  return pl.pallas_call(
      functools.partial(_in_proj_kernel, make_tables=make_tables),
      grid=(n // t,),
      in_specs=in_specs,
      out_specs=[tok(wd) for wd, _ in out_widths],
      out_shape=[jax.ShapeDtypeStruct((n, wd), dt) for wd, dt in out_widths],
      compiler_params=pltpu.CompilerParams(dimension_semantics=("parallel",),
                                           vmem_limit_bytes=VMEM_LIMIT),
      name="in_proj",
  )(*operands)


def _attn_a_kernel(q_ref, k_ref, v_ref, gate_ref, tab_ref, o_ref):
  seq_len = q_ref.shape[1]
  lane = lax.broadcasted_iota(jnp.int32, (1, LANES), 1)
  head_rows = ((lane < A_HEAD_DIM).astype(BF16), (lane >= A_HEAD_DIM).astype(BF16))
  lo_mask = lax.broadcasted_iota(jnp.int32, (Q_BLOCK, LANES), 1) < A_HEAD_DIM
  ones = jnp.ones((A_WINDOW, LANES), BF16)

  def window(qb):
    k0 = max(0, Q_BLOCK * qb - A_LEFT)
    return k0, Q_BLOCK * (qb + 1) - k0

  half = Q_BLOCK // 2

  def panels(qb):
    _, nk = window(qb)
    first_live = half if nk == A_WINDOW else 0
    return (slice(0, half), slice(0, nk - half)), (slice(half, Q_BLOCK), slice(first_live, nk))

  def scores(qb, pr, hh):
    k0, nk = window(qb)
    ls = slice(LANES * pr, LANES * (pr + 1))
    qp = q_ref[0, Q_BLOCK * qb:Q_BLOCK * (qb + 1), ls]
    s = _dot_nt(qp * head_rows[hh], k_ref[0, k0:k0 + nk, ls])
    col0 = A_WINDOW - nk
    return [s[rs, cs] + tab_ref[2 * pr + hh, rs, col0 + cs.start:col0 + cs.stop] for rs, cs in panels(qb)]

  def weighted(qb, pr, s_panels):
    k0, nk = window(qb)
    vp = jnp.concatenate([v_ref[0, k0:k0 + nk, LANES * pr:LANES * (pr + 1)], ones[:nk]], axis=1)
    rows = []
    for s, (_, cs) in zip(s_panels, panels(qb)):
      e = jnp.exp2(s - jnp.max(s, axis=-1, keepdims=True)).astype(BF16)
      dead = [jnp.zeros((half, w), BF16) for w in (cs.start, nk - cs.stop)]
      rows.append(jnp.concatenate([x for x in (dead[0], e, dead[1]) if x.shape[1]], axis=1))
    pv = _dot(jnp.concatenate(rows, axis=0), vp)
    return pv[:, :LANES] / pv[:, LANES:]

  items = [(qb, pr, hh) for qb in reversed(range(seq_len // Q_BLOCK))
           for pr in range(A_STEP_PAIRS) for hh in range(2)]
  depth = 2
  pending = [scores(*it) for it in items[:depth]]
  outs = []
  for i, (qb, pr, hh) in enumerate(items):
    if i + depth < len(items):
      pending.append(scores(*items[i + depth]))
    outs.append(weighted(qb, pr, pending.pop(0)))
    if hh == 1:
      rows = slice(Q_BLOCK * qb, Q_BLOCK * (qb + 1))
      ls = slice(LANES * pr, LANES * (pr + 1))
      o = jnp.where(lo_mask, outs[0], outs[1])
      o_ref[0, rows, ls] = (o * _silu(gate_ref[0, rows, ls])).astype(BF16)
      outs = []


def _attn_a(q, k, v, gate, tab, layer):
  b, s, _ = q.shape
  pair = pl.BlockSpec((1, s, A_STEP_PAIRS * LANES), lambda p, i: (i, 0, p))
  return pl.pallas_call(
      _attn_a_kernel,
      grid=(A_HEADS // (2 * A_STEP_PAIRS), b),
      in_specs=[pair, pair, pair, pair,
                pl.BlockSpec((None, 2 * A_STEP_PAIRS, Q_BLOCK, A_WINDOW),
                             lambda p, i: (layer, p, 0, 0))],
      out_specs=pair,
      out_shape=jax.ShapeDtypeStruct((b, s, A_WIDTH), BF16),
      compiler_params=pltpu.CompilerParams(dimension_semantics=("parallel", "parallel"),
                                           vmem_limit_bytes=VMEM_LIMIT),
      name="attn_a",
  )(q, k, v, gate, tab)


def _attn_b_kernel(q_ref, k_ref, v_ref, gate_ref, o_ref):
  seq_len = q_ref.shape[1]
  qi = lax.broadcasted_iota(jnp.int32, (Q_BLOCK, Q_BLOCK), 0)
  kj = lax.broadcasted_iota(jnp.int32, (Q_BLOCK, Q_BLOCK), 1)
  diag_ok = (kj // CHUNK) <= (qi // CHUNK)

  def scores(qb, h):
    rows = slice(Q_BLOCK * qb, Q_BLOCK * (qb + 1))
    wide = slice(B_QW * h, B_QW * (h + 1))
    qh = q_ref[0, rows, wide]
    s_d = jnp.where(diag_ok, _dot_nt(qh, k_ref[0, rows, wide]), -jnp.inf)
    s_p = _dot_nt(qh, k_ref[0, :Q_BLOCK * qb, wide]) if qb else None
    return s_d, s_p

  def finish(qb, h, s_d, s_p):
    rows = slice(Q_BLOCK * qb, Q_BLOCK * (qb + 1))
    wide = slice(B_QW * h, B_QW * (h + 1))
    narrow = slice(B_VDIM * h, B_VDIM * (h + 1))
    m = jnp.max(s_d, axis=-1, keepdims=True)
    if qb:
      m = jnp.maximum(m, jnp.max(s_p, axis=-1, keepdims=True))
    pv = _dot(jnp.exp2(s_d - m).astype(BF16), v_ref[0, rows, wide])
    if qb:
      pv = pv + _dot(jnp.exp2(s_p - m).astype(BF16), v_ref[0, :Q_BLOCK * qb, wide])
    o = pv[:, :B_VDIM] / pv[:, B_VDIM:]
    o_ref[0, rows, narrow] = (o * _silu(gate_ref[0, rows, narrow])).astype(BF16)

  items = [(qb, h) for qb in reversed(range(seq_len // Q_BLOCK)) for h in range(B_STEP_HEADS)]
  depth = 2
  pending = [scores(*it) for it in items[:depth]]
  for i, it in enumerate(items):
    if i + depth < len(items):
      pending.append(scores(*items[i + depth]))
    finish(*it, *pending.pop(0))


def _attn_b(q, k, v, gate):
  b, s, _ = q.shape
  wide = pl.BlockSpec((1, s, B_STEP_HEADS * B_QW), lambda i, h: (i, 0, h))
  narrow = pl.BlockSpec((1, s, B_STEP_HEADS * B_VDIM), lambda i, h: (i, 0, h))
  return pl.pallas_call(
      _attn_b_kernel,
      grid=(b, B_HEADS // B_STEP_HEADS),
      in_specs=[wide, wide, wide, narrow],
      out_specs=narrow,
      out_shape=jax.ShapeDtypeStruct((b, s, B_WIDTH), BF16),
      compiler_params=pltpu.CompilerParams(dimension_semantics=("parallel", "parallel"),
                                           vmem_limit_bytes=VMEM_LIMIT),
      name="attn_b",
  )(q, k, v, gate)


def _out_ple_kernel(ma_ref, mb_ref, h_ref, p_ref, woa_ref, wob_ref, gple_ref, wpg_ref, bpg_ref,
                    wpe_ref, gfin_ref, o_ref, *, final_norm):
  n_sub = 4
  sub = h_ref.shape[0] // n_sub

  def stage1(j):
    rs = slice(sub * j, sub * (j + 1))
    h1 = h_ref[rs, :] + (_dot(ma_ref[rs, :], woa_ref[...]) + _dot(mb_ref[rs, :], wob_ref[...]))
    pe = _dot(p_ref[rs, :].astype(BF16), wpe_ref[...])
    return h1, pe

  def stage2(j, h1, pe):
    rs = slice(sub * j, sub * (j + 1))
    xn = _rms(h1, gple_ref[...]).astype(BF16)
    z = _dot(xn, wpg_ref[...]) + bpg_ref[...]
    h2 = h1 + pe * (1.0 / (1.0 + jnp.exp(-z)))
    o_ref[rs, :] = _rms(h2, gfin_ref[...]) if final_norm else h2

  pending = stage1(0)
  for j in range(n_sub):
    nxt = stage1(j + 1) if j + 1 < n_sub else None
    stage2(j, *pending)
    pending = nxt


def _out_ple(ma, mb, h, p, w, layer, final_norm):
  n = h.shape[0]
  t = OUT_TOKEN_TILE
  tok = lambda width: pl.BlockSpec((t, width), lambda i: (i, 0))
  w_out_half = lambda j: pl.BlockSpec((None, A_WIDTH, D_MODEL), lambda i: (layer, j, 0))
  rest = [w["gple"], w["wpg"], w["bpg"], w["wpe"]]
  return pl.pallas_call(
      functools.partial(_out_ple_kernel, final_norm=final_norm),
      grid=(n // t,),
      in_specs=([tok(A_WIDTH), tok(B_WIDTH), tok(D_MODEL),
                 pl.BlockSpec((None, t, D_PLE), lambda i: (layer, i, 0)),
                 w_out_half(0), w_out_half(1)] + [_layer_spec(a, layer) for a in rest]
                + [pl.BlockSpec(w["gfin"].shape, lambda i: (0, 0))]),
      out_specs=tok(D_MODEL),
      out_shape=jax.ShapeDtypeStruct((n, D_MODEL), F32),
      compiler_params=pltpu.CompilerParams(dimension_semantics=("parallel",),
                                           vmem_limit_bytes=VMEM_LIMIT),
      name="out_ple",
  )(ma, mb, h, p, w["w_out"], w["w_out"], *rest, w["gfin"])


def _prep_params(norm_mix, w_in, g_q, w_uq, g_kv, w_ukv, w_out, norm_ple, w_pe, w_pg, b_pg,
                 norm_final):
  depth = w_in.shape[0]
  w_t = jnp.swapaxes(w_in, 1, 2).astype(BF16)
  mla0 = 4 * A_WIDTH
  gb0 = mla0 + Q_LORA + KV_LORA + B_ROPE
  wmla = jnp.pad(w_t[:, mla0:gb0], ((0, 0), (0, LANES - B_ROPE), (0, 0)))
  uq =w_uq.reshape(depth, Q_LORA, B_HEADS, B_NOPE + B_ROPE)
  zpad = jnp.zeros((depth, Q_LORA, B_HEADS, LANES - B_ROPE), F32)
  wuq = jnp.concatenate([uq, zpad], axis=-1)
  return {
      "gmix": norm_mix[:, None, :],
      "w_t": w_t, "wgb": w_t[:, gb0:], "wmla": wmla,
      "gq": g_q[:, None, :], "gkv": g_kv[:, None, :],
      "wuq": wuq.reshape(depth, Q_LORA, B_HEADS * B_QW).astype(BF16),
      "wukv": w_ukv.astype(BF16),
      "w_out": w_out.astype(BF16),
      "gple": norm_ple[:, None, :], "wpg": w_pg.astype(BF16), "bpg": b_pg[:, None, :],
      "wpe": w_pe.astype(BF16),
      "gfin": norm_final[None, :],
  }


def kernel(x, p, positions, norm_mix, w_in, rel_bias, g_q, w_uq, g_kv, w_ukv, w_out,
           norm_ple, w_pe, w_pg, b_pg, norm_final):
  b, s, d = x.shape
  depth = w_in.shape[0]
  n = b * s
  w = _prep_params(norm_mix, w_in, g_q, w_uq, g_kv, w_ukv, w_out, norm_ple, w_pe, w_pg, b_pg,
                   norm_final)
  p = p.reshape(depth, n, D_PLE)
  h = x.reshape(n, d)
  seq = lambda a: a.reshape(b, s, a.shape[-1])
  tables = None
  tab = _bias_tables(rel_bias)
  for i in range(depth):
    if tables is None:
      *outs, cos, sin = _in_proj(h, w, i, positions=positions)
      tables = (cos, sin)
    else:
      outs = _in_proj(h, w, i, tables=tables)
    qa, ka, va, ga, gb, qb, kb, vb = outs
    ma = _attn_a(seq(qa), seq(ka), seq(va), seq(ga), tab, i)
    mb = _attn_b(seq(qb), seq(kb), seq(vb), seq(gb))
    h = _out_ple(ma.reshape(n, A_WIDTH), mb.reshape(n, B_WIDTH), h, p, w, i,
                 final_norm=(i == depth - 1))
  return h.reshape(b, s, d)
```

```python
import functools
import math

import jax
import jax.numpy as jnp
from jax import lax
from jax.experimental import pallas as pl
from jax.experimental.pallas import tpu as pltpu

D_MODEL = 1024
CHUNK = 64
D_PLE = 256
EPS = 1e-6
A_HEADS = 8
A_HEAD_DIM = 64
A_WIDTH = A_HEADS * A_HEAD_DIM
A_LEFT_CHUNKS = 8
REL_MAX = 128
B_HEADS = 4
B_NOPE = 128
B_ROPE = 64
B_VDIM = 128
B_WIDTH = B_HEADS * B_VDIM
Q_LORA = 256
KV_LORA = 128
ROPE_THETA = 10000.0

LANES = 128
TOKEN_TILE = 1024
OUT_TOKEN_TILE = 1024
Q_BLOCK = 256
A_STEP_PAIRS = 2
B_STEP_HEADS = 2
A_LEFT = A_LEFT_CHUNKS * CHUNK
A_WINDOW = A_LEFT + Q_BLOCK
B_QW = B_NOPE + LANES
LOG2E = math.log2(math.e)
B_QSCALE = (B_NOPE + B_ROPE) ** -0.5 * LOG2E
VMEM_LIMIT = 56 * 1024 * 1024

BF16 = jnp.bfloat16
F32 = jnp.float32


def _dot(a, b):
  return jnp.dot(a, b, preferred_element_type=F32)


def _dot_nt(a, b):
  return lax.dot_general(a, b, (((1,), (1,)), ((), ())), preferred_element_type=F32)


def _rms(x, g):
  y = x * lax.rsqrt(jnp.mean(x * x, axis=-1, keepdims=True) + EPS)
  return y * g


def _silu(g):
  return g * (1.0 / (1.0 + jnp.exp(-g)))


def _inv_freq_column():
  inv_freq = ROPE_THETA ** (-jnp.arange(0, B_ROPE, 2, dtype=F32) / B_ROPE)
  return inv_freq[:, None]


def _rope_tile(pos_row, inv_freq_col):
  t = pos_row.shape[1]
  ang = pos_row.astype(F32) * inv_freq_col
  c, s = jnp.cos(ang), jnp.sin(ang)
  zeros = jnp.zeros((LANES - B_ROPE, t), F32)
  return jnp.concatenate([c, c, zeros], axis=0).T, jnp.concatenate([-s, s, zeros], axis=0).T


G_LEN = 1024


def _bias_table_kernel(g_ref, tab_ref):
  qi = lax.broadcasted_iota(jnp.int32, (Q_BLOCK, A_WINDOW), 0)
  kk = lax.broadcasted_iota(jnp.int32, (Q_BLOCK, A_WINDOW), 1)
  qc = (qi + A_LEFT) // CHUNK
  kc = kk // CHUNK
  valid = (kc <= qc) & (kc >= qc - A_LEFT_CHUNKS)
  for h in range(A_HEADS):
    g = jnp.broadcast_to(g_ref[0, h], (Q_BLOCK, G_LEN))
    t = pltpu.roll(g, G_LEN - Q_BLOCK, 1, stride=1, stride_axis=0)[:, :A_WINDOW]
    tab_ref[0, h] = jnp.where(valid, (t - g[:, :1]) * LOG2E, -jnp.inf)


def _bias_tables(rel_bias):
  depth = rel_bias.shape[0]
  flat = A_LEFT + Q_BLOCK - REL_MAX
  rev = rel_bias[..., ::-1]
  g = jnp.concatenate([
      jnp.broadcast_to(rel_bias[..., -1:], (depth, A_HEADS, flat)),
      rev[..., : 2 * REL_MAX],
      jnp.broadcast_to(rel_bias[..., :1], (depth, A_HEADS, G_LEN - flat - 2 * REL_MAX)),
  ], axis=-1)[:, :, None, :]
  return pl.pallas_call(
      _bias_table_kernel,
      grid=(depth,),
      in_specs=[pl.BlockSpec((1, A_HEADS, 1, G_LEN), lambda l: (l, 0, 0, 0))],
      out_specs=pl.BlockSpec((1, A_HEADS, Q_BLOCK, A_WINDOW), lambda l: (l, 0, 0, 0)),
      out_shape=jax.ShapeDtypeStruct((depth, A_HEADS, Q_BLOCK, A_WINDOW), F32),
      compiler_params=pltpu.CompilerParams(dimension_semantics=("parallel",)),
      name="bias_table",
  )(g)


def _in_proj_kernel(h_ref, gmix_ref, wq_ref, wk_ref, wv_ref, wga_ref, wgb_ref, wmla_ref,
                    gq_ref, gkv_ref, wuq_ref, wukv_ref, *refs, make_tables):
  if make_tables:
    (pos_ref, invf_ref, q_out, k_out, v_out, ga_out, gb_out, qb_out, kb_out, vb_out,
     cos_out, sin_out) = refs
  else:
    cos_ref, sin_ref, q_out, k_out, v_out, ga_out, gb_out, qb_out, kb_out, vb_out = refs
  xn = _rms(h_ref[...], gmix_ref[...]).astype(BF16)
  zm = _dot_nt(xn, wmla_ref[...])
  cqn = _rms(zm[:, :Q_LORA], gq_ref[...]).astype(BF16)
  ckvn = _rms(zm[:, Q_LORA:Q_LORA + KV_LORA], gkv_ref[...]).astype(BF16)
  qall = _dot(cqn, wuq_ref[...])
  kv = _dot(ckvn, wukv_ref[...])

  q_out[...] = (_dot_nt(xn, wq_ref[...]) * (A_HEAD_DIM ** -0.5 * LOG2E)).astype(BF16)
  k_out[...] = _dot_nt(xn, wk_ref[...]).astype(BF16)
  v_out[...] = _dot_nt(xn, wv_ref[...]).astype(BF16)
  ga_out[...] = _dot_nt(xn, wga_ref[...])
  gb_out[...] = _dot_nt(xn, wgb_ref[...])

  if make_tables:
    cos, sin = _rope_tile(pos_ref[0], invf_ref[...])
    cos_out[...] = cos
    sin_out[...] = sin
  else:
    cos = cos_ref[...]
    sin = sin_ref[...]
  first_half = lax.broadcasted_iota(jnp.int32, (h_ref.shape[0], LANES), 1) < B_ROPE // 2

  def rope(g):
    swapped = jnp.where(first_half, pltpu.roll(g, LANES - B_ROPE // 2, 1), pltpu.roll(g, B_ROPE // 2, 1))
    return g * cos + swapped * sin

  k_rope = rope(zm[:, Q_LORA + KV_LORA:]).astype(BF16)
  ones = jnp.ones((h_ref.shape[0], B_QW - B_VDIM), BF16)
  for h in range(B_HEADS):
    qb = B_QW * h
    qb_out[:, qb:qb + B_NOPE] = (qall[:, qb:qb + B_NOPE] * B_QSCALE).astype(BF16)
    qb_out[:, qb + B_NOPE:qb + B_QW] = (rope(qall[:, qb + B_NOPE:qb + B_QW]) * B_QSCALE).astype(BF16)
    kb_out[:, B_QW * h:B_QW * h + B_NOPE] = kv[:, 2 * LANES * h:2 * LANES * h + B_NOPE].astype(BF16)
    kb_out[:, B_QW * h + B_NOPE:B_QW * (h + 1)] = k_rope
    vb_out[:, B_QW * h:B_QW * h + B_VDIM] = kv[:, 2 * LANES * h + B_NOPE:2 * LANES * (h + 1)].astype(BF16)
    vb_out[:, B_QW * h + B_VDIM:B_QW * (h + 1)] = ones


def _layer_spec(a, layer):
  return pl.BlockSpec((None,) + a.shape[1:], lambda i: (layer,) + (0,) * (a.ndim - 1),
                      pipeline_mode=pl.Buffered(1))


def _in_proj(h, w, layer, positions=None, tables=None):
  n = h.shape[0]
  t = TOKEN_TILE
  tok = lambda width: pl.BlockSpec((t, width), lambda i: (i, 0))
  w_in_block = lambda j: pl.BlockSpec((None, A_WIDTH, D_MODEL), lambda i: (layer, j, 0),
                                      pipeline_mode=pl.Buffered(1))
  small = [w["gmix"], w["wgb"], w["wmla"], w["gq"], w["gkv"], w["wuq"], w["wukv"]]
  in_specs = ([tok(D_MODEL), _layer_spec(small[0], layer)] + [w_in_block(j) for j in range(4)]
              + [_layer_spec(a, layer) for a in small[1:]])
  operands = [h, small[0]] + [w["w_t"]] * 4 + small[1:]
  out_widths = [(A_WIDTH, BF16), (A_WIDTH, BF16), (A_WIDTH, BF16), (A_WIDTH, F32), (B_WIDTH, F32),
                (B_HEADS * B_QW, BF16), (B_HEADS * B_QW, BF16), (B_HEADS * B_QW, BF16)]
  make_tables = tables is None
  if make_tables:
    invf = _inv_freq_column()
    operands += [positions.reshape(n // t, 1, t), invf]
    in_specs += [pl.BlockSpec((1, 1, t), lambda i: (i, 0, 0)), pl.BlockSpec(invf.shape, lambda i: (0, 0))]
    out_widths += [(LANES, F32), (LANES, F32)]
  else:
    operands += list(tables)
    in_specs += [tok(LANES), tok(LANES)]
  return pl.pallas_call(
      functools.partial(_in_proj_kernel, make_tables=make_tables),
      grid=(n // t,),
      in_specs=in_specs,
      out_specs=[tok(wd) for wd, _ in out_widths],
      out_shape=[jax.ShapeDtypeStruct((n, wd), dt) for wd, dt in out_widths],
      compiler_params=pltpu.CompilerParams(dimension_semantics=("parallel",),
                                           vmem_limit_bytes=VMEM_LIMIT),
      name="in_proj",
  )(*operands)


def _attn_a_kernel(q_ref, k_ref, v_ref, gate_ref, tab_ref, o_ref):
  seq_len = q_ref.shape[1]
  lane = lax.broadcasted_iota(jnp.int32, (1, LANES), 1)
  head_rows = ((lane < A_HEAD_DIM).astype(BF16), (lane >= A_HEAD_DIM).astype(BF16))
  lo_mask = lax.broadcasted_iota(jnp.int32, (Q_BLOCK, LANES), 1) < A_HEAD_DIM
  ones = jnp.ones((A_WINDOW, LANES), BF16)

  def window(qb):
    k0 = max(0, Q_BLOCK * qb - A_LEFT)
    return k0, Q_BLOCK * (qb + 1) - k0

  half = Q_BLOCK // 2
  table_zero = ((half, A_LEFT - REL_MAX), (2 * half, A_LEFT - REL_MAX + half))

  def panels(qb):
    _, nk = window(qb)
    first_live = half if nk == A_WINDOW else 0
    return (slice(0, half), slice(0, nk - half)), (slice(half, Q_BLOCK), slice(first_live, nk))

  def scores(qb, pr, hh):
    k0, nk = window(qb)
    ls = slice(LANES * pr, LANES * (pr + 1))
    qp = q_ref[0, Q_BLOCK * qb:Q_BLOCK * (qb + 1), ls]
    s = _dot_nt(qp * head_rows[hh], k_ref[0, k0:k0 + nk, ls])
    col0 = A_WINDOW - nk
    out = []
    for (rs, cs), zero in zip(panels(qb), table_zero):
      cuts = sorted({col0 + cs.start, col0 + cs.stop,
                     *(min(max(z, col0 + cs.start), col0 + cs.stop) for z in zero)})
      pieces = []
      for c0, c1 in zip(cuts[:-1], cuts[1:]):
        piece = s[rs, c0 - col0:c1 - col0]
        if not zero[0] <= c0 < zero[1]:
          piece = piece + tab_ref[2 * pr + hh, rs, c0:c1]
        pieces.append(piece)
      out.append(jnp.concatenate(pieces, axis=1))
    return out

  def weighted(qb, pr, s_panels):
    k0, nk = window(qb)
    vp = jnp.concatenate([v_ref[0, k0:k0 + nk, LANES * pr:LANES * (pr + 1)], ones[:nk]], axis=1)
    rows = []
    for s, (_, cs) in zip(s_panels, panels(qb)):
      e = jnp.exp2(s - jnp.max(s, axis=-1, keepdims=True)).astype(BF16)
      dead = [jnp.zeros((half, w), BF16) for w in (cs.start, nk - cs.stop)]
      rows.append(jnp.concatenate([x for x in (dead[0], e, dead[1]) if x.shape[1]], axis=1))
    pv = _dot(jnp.concatenate(rows, axis=0), vp)
    return pv[:, :LANES] / pv[:, LANES:]

  items = [(qb, pr, hh) for qb in reversed(range(seq_len // Q_BLOCK))
           for pr in range(A_STEP_PAIRS) for hh in range(2)]
  depth = 2
  pending = [scores(*it) for it in items[:depth]]
  outs = []
  for i, (qb, pr, hh) in enumerate(items):
    if i + depth < len(items):
      pending.append(scores(*items[i + depth]))
    outs.append(weighted(qb, pr, pending.pop(0)))
    if hh == 1:
      rows = slice(Q_BLOCK * qb, Q_BLOCK * (qb + 1))
      ls = slice(LANES * pr, LANES * (pr + 1))
      o = jnp.where(lo_mask, outs[0], outs[1])
      o_ref[0, rows, ls] = (o * _silu(gate_ref[0, rows, ls])).astype(BF16)
      outs = []


def _attn_a(q, k, v, gate, tab, layer):
  b, s, _ = q.shape
  pair = pl.BlockSpec((1, s, A_STEP_PAIRS * LANES), lambda p, i: (i, 0, p))
  return pl.pallas_call(
      _attn_a_kernel,
      grid=(A_HEADS // (2 * A_STEP_PAIRS), b),
      in_specs=[pair, pair, pair, pair,
                pl.BlockSpec((None, 2 * A_STEP_PAIRS, Q_BLOCK, A_WINDOW),
                             lambda p, i: (layer, p, 0, 0))],
      out_specs=pair,
      out_shape=jax.ShapeDtypeStruct((b, s, A_WIDTH), BF16),
      compiler_params=pltpu.CompilerParams(dimension_semantics=("parallel", "parallel"),
                                           vmem_limit_bytes=VMEM_LIMIT),
      name="attn_a",
  )(q, k, v, gate, tab)


def _attn_b_kernel(q_ref, k_ref, v_ref, gate_ref, o_ref):
  seq_len = q_ref.shape[1]
  qi = lax.broadcasted_iota(jnp.int32, (Q_BLOCK, Q_BLOCK), 0)
  kj = lax.broadcasted_iota(jnp.int32, (Q_BLOCK, Q_BLOCK), 1)
  diag_ok = (kj // CHUNK) <= (qi // CHUNK)

  def scores(qb, h):
    rows = slice(Q_BLOCK * qb, Q_BLOCK * (qb + 1))
    wide = slice(B_QW * h, B_QW * (h + 1))
    qh = q_ref[0, rows, wide]
    s_d = jnp.where(diag_ok, _dot_nt(qh, k_ref[0, rows, wide]), -jnp.inf)
    s_p = _dot_nt(qh, k_ref[0, :Q_BLOCK * qb, wide]) if qb else None
    return s_d, s_p

  def finish(qb, h, s_d, s_p):
    rows = slice(Q_BLOCK * qb, Q_BLOCK * (qb + 1))
    wide = slice(B_QW * h, B_QW * (h + 1))
    narrow = slice(B_VDIM * h, B_VDIM * (h + 1))
    m = jnp.max(s_d, axis=-1, keepdims=True)
    if qb:
      m = jnp.maximum(m, jnp.max(s_p, axis=-1, keepdims=True))
    pv = _dot(jnp.exp2(s_d - m).astype(BF16), v_ref[0, rows, wide])
    if qb:
      pv = pv + _dot(jnp.exp2(s_p - m).astype(BF16), v_ref[0, :Q_BLOCK * qb, wide])
    o = pv[:, :B_VDIM] / pv[:, B_VDIM:]
    o_ref[0, rows, narrow] = (o * _silu(gate_ref[0, rows, narrow])).astype(BF16)

  items = [(qb, h) for qb in reversed(range(seq_len // Q_BLOCK)) for h in range(B_STEP_HEADS)]
  depth = 2
  pending = [scores(*it) for it in items[:depth]]
  for i, it in enumerate(items):
    if i + depth < len(items):
      pending.append(scores(*items[i + depth]))
    finish(*it, *pending.pop(0))


def _attn_b(q, k, v, gate):
  b, s, _ = q.shape
  wide = pl.BlockSpec((1, s, B_STEP_HEADS * B_QW), lambda i, h: (i, 0, h))
  narrow = pl.BlockSpec((1, s, B_STEP_HEADS * B_VDIM), lambda i, h: (i, 0, h))
  return pl.pallas_call(
      _attn_b_kernel,
      grid=(b, B_HEADS // B_STEP_HEADS),
      in_specs=[wide, wide, wide, narrow],
      out_specs=narrow,
      out_shape=jax.ShapeDtypeStruct((b, s, B_WIDTH), BF16),
      compiler_params=pltpu.CompilerParams(dimension_semantics=("parallel", "parallel"),
                                           vmem_limit_bytes=VMEM_LIMIT),
      name="attn_b",
  )(q, k, v, gate)


def _out_ple_kernel(ma_ref, mb_ref, h_ref, p_ref, woa_ref, wob_ref, gple_ref, wpg_ref, bpg_ref,
                    wpe_ref, gfin_ref, o_ref, *, final_norm):
  n_sub = 4 if final_norm else 2
  sub = h_ref.shape[0] // n_sub

  def stage1(j):
    rs = slice(sub * j, sub * (j + 1))
    h1 = h_ref[rs, :] + (_dot(ma_ref[rs, :], woa_ref[...]) + _dot(mb_ref[rs, :], wob_ref[...]))
    pe = _dot(p_ref[rs, :].astype(BF16), wpe_ref[...])
    return h1, pe

  def stage2(j, h1, pe):
    rs = slice(sub * j, sub * (j + 1))
    xn = _rms(h1, gple_ref[...]).astype(BF16)
    z = _dot(xn, wpg_ref[...]) + bpg_ref[...]
    h2 = h1 + pe * (1.0 / (1.0 + jnp.exp(-z)))
    o_ref[rs, :] = _rms(h2, gfin_ref[...]) if final_norm else h2

  pending = stage1(0)
  for j in range(n_sub):
    nxt = stage1(j + 1) if j + 1 < n_sub else None
    stage2(j, *pending)
    pending = nxt


def _out_ple(ma, mb, h, p, w, layer, final_norm):
  n = h.shape[0]
  t = OUT_TOKEN_TILE
  tok = lambda width: pl.BlockSpec((t, width), lambda i: (i, 0))
  w_out_half = lambda j: pl.BlockSpec((None, A_WIDTH, D_MODEL), lambda i: (layer, j, 0))
  rest = [w["gple"], w["wpg"], w["bpg"], w["wpe"]]
  return pl.pallas_call(
      functools.partial(_out_ple_kernel, final_norm=final_norm),
      grid=(n // t,),
      in_specs=([tok(A_WIDTH), tok(B_WIDTH), tok(D_MODEL),
                 pl.BlockSpec((None, t, D_PLE), lambda i: (layer, i, 0)),
                 w_out_half(0), w_out_half(1)] + [_layer_spec(a, layer) for a in rest]
                + [pl.BlockSpec(w["gfin"].shape, lambda i: (0, 0))]),
      out_specs=tok(D_MODEL),
      out_shape=jax.ShapeDtypeStruct((n, D_MODEL), F32),
      compiler_params=pltpu.CompilerParams(dimension_semantics=("parallel",),
                                           vmem_limit_bytes=VMEM_LIMIT),
      name="out_ple",
  )(ma, mb, h, p, w["w_out"], w["w_out"], *rest, w["gfin"])


def _prep_params(norm_mix, w_in, g_q, w_uq, g_kv, w_ukv, w_out, norm_ple, w_pe, w_pg, b_pg,
                 norm_final):
  depth = w_in.shape[0]
  w_t = jnp.swapaxes(w_in, 1, 2).astype(BF16)
  mla0 = 4 * A_WIDTH
  gb0 = mla0 + Q_LORA + KV_LORA + B_ROPE
  wmla = jnp.pad(w_t[:, mla0:gb0], ((0, 0), (0, LANES - B_ROPE), (0, 0)))
  uq =w_uq.reshape(depth, Q_LORA, B_HEADS, B_NOPE + B_ROPE)
  zpad = jnp.zeros((depth, Q_LORA, B_HEADS, LANES - B_ROPE), F32)
  wuq = jnp.concatenate([uq, zpad], axis=-1)
  return {
      "gmix": norm_mix[:, None, :],
      "w_t": w_t, "wgb": w_t[:, gb0:], "wmla": wmla,
      "gq": g_q[:, None, :], "gkv": g_kv[:, None, :],
      "wuq": wuq.reshape(depth, Q_LORA, B_HEADS * B_QW).astype(BF16),
      "wukv": w_ukv.astype(BF16),
      "w_out": w_out.astype(BF16),
      "gple": norm_ple[:, None, :], "wpg": w_pg.astype(BF16), "bpg": b_pg[:, None, :],
      "wpe": w_pe.astype(BF16),
      "gfin": norm_final[None, :],
  }


def kernel(x, p, positions, norm_mix, w_in, rel_bias, g_q, w_uq, g_kv, w_ukv, w_out,
           norm_ple, w_pe, w_pg, b_pg, norm_final):
  b, s, d = x.shape
  depth = w_in.shape[0]
  n = b * s
  w = _prep_params(norm_mix, w_in, g_q, w_uq, g_kv, w_ukv, w_out, norm_ple, w_pe, w_pg, b_pg,
                   norm_final)
  p = p.reshape(depth, n, D_PLE)
  h = x.reshape(n, d)
  seq = lambda a: a.reshape(b, s, a.shape[-1])
  tables = None
  tab = _bias_tables(rel_bias)
  for i in range(depth):
    if tables is None:
      *outs, cos, sin = _in_proj(h, w, i, positions=positions)
      tables = (cos, sin)
    else:
      outs = _in_proj(h, w, i, tables=tables)
    qa, ka, va, ga, gb, qb, kb, vb = outs
    ma = _attn_a(seq(qa), seq(ka), seq(va), seq(ga), tab, i)
    mb = _attn_b(seq(qb), seq(kb), seq(vb), seq(gb))
    h = _out_ple(ma.reshape(n, A_WIDTH), mb.reshape(n, B_WIDTH), h, p, w, i,
                 final_norm=(i == depth - 1))
  return h.reshape(b, s, d)
```

```python
import functools
import math

import jax
import jax.numpy as jnp
from jax import lax
from jax.experimental import pallas as pl
from jax.experimental.pallas import tpu as pltpu

D_MODEL = 1024
CHUNK = 64
D_PLE = 256
EPS = 1e-6
A_HEADS = 8
A_HEAD_DIM = 64
A_WIDTH = A_HEADS * A_HEAD_DIM
A_LEFT_CHUNKS = 8
REL_MAX = 128
B_HEADS = 4
B_NOPE = 128
B_ROPE = 64
B_VDIM = 128
B_WIDTH = B_HEADS * B_VDIM
Q_LORA = 256
KV_LORA = 128
ROPE_THETA = 10000.0

LANES = 128
TOKEN_TILE = 1024
OUT_TOKEN_TILE = 1024
Q_BLOCK = 256
A_STEP_PAIRS = 4
B_STEP_HEADS = 4
A_LEFT = A_LEFT_CHUNKS * CHUNK
A_WINDOW = A_LEFT + Q_BLOCK
B_QW = B_NOPE + LANES
LOG2E = math.log2(math.e)
B_QSCALE = (B_NOPE + B_ROPE) ** -0.5 * LOG2E
VMEM_LIMIT = 56 * 1024 * 1024

BF16 = jnp.bfloat16
F32 = jnp.float32


def _dot(a, b):
  return jnp.dot(a, b, preferred_element_type=F32)


def _dot_nt(a, b):
  return lax.dot_general(a, b, (((1,), (1,)), ((), ())), preferred_element_type=F32)


def _rms(x, g):
  y = x * lax.rsqrt(jnp.mean(x * x, axis=-1, keepdims=True) + EPS)
  return y * g


def _silu(g):
  return g * (1.0 / (1.0 + jnp.exp(-g)))


def _inv_freq_column():
  inv_freq = ROPE_THETA ** (-jnp.arange(0, B_ROPE, 2, dtype=F32) / B_ROPE)
  return inv_freq[:, None]


def _rope_tile(pos_row, inv_freq_col):
  t = pos_row.shape[1]
  ang = pos_row.astype(F32) * inv_freq_col
  c, s = jnp.cos(ang), jnp.sin(ang)
  zeros = jnp.zeros((LANES - B_ROPE, t), F32)
  return jnp.concatenate([c, c, zeros], axis=0).T, jnp.concatenate([-s, s, zeros], axis=0).T


G_LEN = 1024


def _bias_table_kernel(g_ref, tab_ref):
  qi = lax.broadcasted_iota(jnp.int32, (Q_BLOCK, A_WINDOW), 0)
  kk = lax.broadcasted_iota(jnp.int32, (Q_BLOCK, A_WINDOW), 1)
  qc = (qi + A_LEFT) // CHUNK
  kc = kk // CHUNK
  valid = (kc <= qc) & (kc >= qc - A_LEFT_CHUNKS)
  for h in range(A_HEADS):
    g = jnp.broadcast_to(g_ref[0, h], (Q_BLOCK, G_LEN))
    t = pltpu.roll(g, G_LEN - Q_BLOCK, 1, stride=1, stride_axis=0)[:, :A_WINDOW]
    tab_ref[0, h] = jnp.where(valid, (t - g[:, :1]) * LOG2E, -jnp.inf)


def _bias_tables(rel_bias):
  depth = rel_bias.shape[0]
  flat = A_LEFT + Q_BLOCK - REL_MAX
  rev = rel_bias[..., ::-1]
  g = jnp.concatenate([
      jnp.broadcast_to(rel_bias[..., -1:], (depth, A_HEADS, flat)),
      rev[..., : 2 * REL_MAX],
      jnp.broadcast_to(rel_bias[..., :1], (depth, A_HEADS, G_LEN - flat - 2 * REL_MAX)),
  ], axis=-1)[:, :, None, :]
  return pl.pallas_call(
      _bias_table_kernel,
      grid=(depth,),
      in_specs=[pl.BlockSpec((1, A_HEADS, 1, G_LEN), lambda l: (l, 0, 0, 0))],
      out_specs=pl.BlockSpec((1, A_HEADS, Q_BLOCK, A_WINDOW), lambda l: (l, 0, 0, 0)),
      out_shape=jax.ShapeDtypeStruct((depth, A_HEADS, Q_BLOCK, A_WINDOW), F32),
      compiler_params=pltpu.CompilerParams(dimension_semantics=("parallel",)),
      name="bias_table",
  )(g)


def _in_proj_kernel(h_ref, gmix_ref, wq_ref, wk_ref, wv_ref, wga_ref, wgb_ref, wmla_ref,
                    gq_ref, gkv_ref, wuq_ref, wukv_ref, *refs, make_tables):
  if make_tables:
    (pos_ref, invf_ref, q_out, k_out, v_out, ga_out, gb_out, qb_out, kb_out, vb_out,
     cos_out, sin_out) = refs
  else:
    cos_ref, sin_ref, q_out, k_out, v_out, ga_out, gb_out, qb_out, kb_out, vb_out = refs
  xn = _rms(h_ref[...], gmix_ref[...]).astype(BF16)
  zm = _dot_nt(xn, wmla_ref[...])
  cqn = _rms(zm[:, :Q_LORA], gq_ref[...]).astype(BF16)
  ckvn = _rms(zm[:, Q_LORA:Q_LORA + KV_LORA], gkv_ref[...]).astype(BF16)
  qall = _dot(cqn, wuq_ref[...])
  kv = _dot(ckvn, wukv_ref[...])

  q_out[...] = (_dot_nt(xn, wq_ref[...]) * (A_HEAD_DIM ** -0.5 * LOG2E)).astype(BF16)
  k_out[...] = _dot_nt(xn, wk_ref[...]).astype(BF16)
  v_out[...] = _dot_nt(xn, wv_ref[...]).astype(BF16)
  ga_out[...] = _dot_nt(xn, wga_ref[...])
  gb_out[...] = _dot_nt(xn, wgb_ref[...])

  if make_tables:
    cos, sin = _rope_tile(pos_ref[0], invf_ref[...])
    cos_out[...] = cos
    sin_out[...] = sin
  else:
    cos = cos_ref[...]
    sin = sin_ref[...]
  first_half = lax.broadcasted_iota(jnp.int32, (h_ref.shape[0], LANES), 1) < B_ROPE // 2

  def rope(g):
    swapped = jnp.where(first_half, pltpu.roll(g, LANES - B_ROPE // 2, 1), pltpu.roll(g, B_ROPE // 2, 1))
    return g * cos + swapped * sin

  k_rope = rope(zm[:, Q_LORA + KV_LORA:]).astype(BF16)
  ones = jnp.ones((h_ref.shape[0], B_QW - B_VDIM), BF16)
  for h in range(B_HEADS):
    qb = B_QW * h
    qb_out[:, qb:qb + B_NOPE] = (qall[:, qb:qb + B_NOPE] * B_QSCALE).astype(BF16)
    qb_out[:, qb + B_NOPE:qb + B_QW] = (rope(qall[:, qb + B_NOPE:qb + B_QW]) * B_QSCALE).astype(BF16)
    kb_out[:, B_QW * h:B_QW * h + B_NOPE] = kv[:, 2 * LANES * h:2 * LANES * h + B_NOPE].astype(BF16)
    kb_out[:, B_QW * h + B_NOPE:B_QW * (h + 1)] = k_rope
    vb_out[:, B_QW * h:B_QW * h + B_VDIM] = kv[:, 2 * LANES * h + B_NOPE:2 * LANES * (h + 1)].astype(BF16)
    vb_out[:, B_QW * h + B_VDIM:B_QW * (h + 1)] = ones


def _layer_spec(a, layer):
  return pl.BlockSpec((None,) + a.shape[1:], lambda i: (layer,) + (0,) * (a.ndim - 1),
                      pipeline_mode=pl.Buffered(1))


def _in_proj(h, w, layer, positions=None, tables=None):
  n = h.shape[0]
  t = TOKEN_TILE
  tok = lambda width: pl.BlockSpec((t, width), lambda i: (i, 0))
  w_in_block = lambda j: pl.BlockSpec((None, A_WIDTH, D_MODEL), lambda i: (layer, j, 0),
                                      pipeline_mode=pl.Buffered(1))
  small = [w["gmix"], w["wgb"], w["wmla"], w["gq"], w["gkv"], w["wuq"], w["wukv"]]
  in_specs = ([tok(D_MODEL), _layer_spec(small[0], layer)] + [w_in_block(j) for j in range(4)]
              + [_layer_spec(a, layer) for a in small[1:]])
  operands = [h, small[0]] + [w["w_t"]] * 4 + small[1:]
  out_widths = [(A_WIDTH, BF16), (A_WIDTH, BF16), (A_WIDTH, BF16), (A_WIDTH, F32), (B_WIDTH, F32),
                (B_HEADS * B_QW, BF16), (B_HEADS * B_QW, BF16), (B_HEADS * B_QW, BF16)]
  make_tables = tables is None
  if make_tables:
    invf = _inv_freq_column()
    operands += [positions.reshape(n // t, 1, t), invf]
    in_specs += [pl.BlockSpec((1, 1, t), lambda i: (i, 0, 0)), pl.BlockSpec(invf.shape, lambda i: (0, 0))]
    out_widths += [(LANES, F32), (LANES, F32)]
  else:
    operands += list(tables)
    in_specs += [tok(LANES), tok(LANES)]
  return pl.pallas_call(
      functools.partial(_in_proj_kernel, make_tables=make_tables),
      grid=(n // t,),
      in_specs=in_specs,
      out_specs=[tok(wd) for wd, _ in out_widths],
      out_shape=[jax.ShapeDtypeStruct((n, wd), dt) for wd, dt in out_widths],
      compiler_params=pltpu.CompilerParams(dimension_semantics=("parallel",),
                                           vmem_limit_bytes=VMEM_LIMIT),
      name="in_proj",
  )(*operands)


def _attn_a_kernel(q_ref, k_ref, v_ref, gate_ref, tab_ref, o_ref):
  seq_len = q_ref.shape[1]
  lane = lax.broadcasted_iota(jnp.int32, (1, LANES), 1)
  head_rows = ((lane < A_HEAD_DIM).astype(BF16), (lane >= A_HEAD_DIM).astype(BF16))
  lo_mask = lax.broadcasted_iota(jnp.int32, (Q_BLOCK, LANES), 1) < A_HEAD_DIM
  ones = jnp.ones((A_WINDOW, LANES), BF16)

  def window(qb):
    k0 = max(0, Q_BLOCK * qb - A_LEFT)
    return k0, Q_BLOCK * (qb + 1) - k0

  half = Q_BLOCK // 2
  table_zero = ((half, A_LEFT - REL_MAX), (2 * half, A_LEFT - REL_MAX + half))

  def panels(qb):
    _, nk = window(qb)
    first_live = half if nk == A_WINDOW else 0
    return (slice(0, half), slice(0, nk - half)), (slice(half, Q_BLOCK), slice(first_live, nk))

  def scores(qb, pr, hh):
    k0, nk = window(qb)
    ls = slice(LANES * pr, LANES * (pr + 1))
    qp = q_ref[0, Q_BLOCK * qb:Q_BLOCK * (qb + 1), ls]
    s = _dot_nt(qp * head_rows[hh], k_ref[0, k0:k0 + nk, ls])
    col0 = A_WINDOW - nk
    out = []
    for (rs, cs), zero in zip(panels(qb), table_zero):
      cuts = sorted({col0 + cs.start, col0 + cs.stop,
                     *(min(max(z, col0 + cs.start), col0 + cs.stop) for z in zero)})
      pieces = []
      for c0, c1 in zip(cuts[:-1], cuts[1:]):
        piece = s[rs, c0 - col0:c1 - col0]
        if not zero[0] <= c0 < zero[1]:
          piece = piece + tab_ref[2 * pr + hh, rs, c0:c1]
        pieces.append(piece)
      out.append(jnp.concatenate(pieces, axis=1))
    return out

  def weighted(qb, pr, s_panels):
    k0, nk = window(qb)
    vp = jnp.concatenate([v_ref[0, k0:k0 + nk, LANES * pr:LANES * (pr + 1)], ones[:nk]], axis=1)
    rows = []
    for s, (_, cs) in zip(s_panels, panels(qb)):
      e = jnp.exp2(s - jnp.max(s, axis=-1, keepdims=True)).astype(BF16)
      dead = [jnp.zeros((half, w), BF16) for w in (cs.start, nk - cs.stop)]
      rows.append(jnp.concatenate([x for x in (dead[0], e, dead[1]) if x.shape[1]], axis=1))
    pv = _dot(jnp.concatenate(rows, axis=0), vp)
    return pv[:, :LANES] / pv[:, LANES:]

  items = [(qb, pr, hh) for qb in reversed(range(seq_len // Q_BLOCK))
           for pr in range(A_STEP_PAIRS) for hh in range(2)]
  depth = 2
  pending = [scores(*it) for it in items[:depth]]
  outs = []
  for i, (qb, pr, hh) in enumerate(items):
    if i + depth < len(items):
      pending.append(scores(*items[i + depth]))
    outs.append(weighted(qb, pr, pending.pop(0)))
    if hh == 1:
      rows = slice(Q_BLOCK * qb, Q_BLOCK * (qb + 1))
      ls = slice(LANES * pr, LANES * (pr + 1))
      o = jnp.where(lo_mask, outs[0], outs[1])
      o_ref[0, rows, ls] = (o * _silu(gate_ref[0, rows, ls])).astype(BF16)
      outs = []


def _attn_a(q, k, v, gate, tab, layer):
  b, s, _ = q.shape
  pair = pl.BlockSpec((1, s, A_STEP_PAIRS * LANES), lambda p, i: (i, 0, p))
  return pl.pallas_call(
      _attn_a_kernel,
      grid=(A_HEADS // (2 * A_STEP_PAIRS), b),
      in_specs=[pair, pair, pair, pair,
                pl.BlockSpec((None, 2 * A_STEP_PAIRS, Q_BLOCK, A_WINDOW),
                             lambda p, i: (layer, p, 0, 0))],
      out_specs=pair,
      out_shape=jax.ShapeDtypeStruct((b, s, A_WIDTH), BF16),
      compiler_params=pltpu.CompilerParams(dimension_semantics=("parallel", "parallel"),
                                           vmem_limit_bytes=VMEM_LIMIT),
      name="attn_a",
  )(q, k, v, gate, tab)


def _attn_b_kernel(q_ref, k_ref, v_ref, gate_ref, o_ref):
  seq_len = q_ref.shape[1]
  qi = lax.broadcasted_iota(jnp.int32, (Q_BLOCK, Q_BLOCK), 0)
  kj = lax.broadcasted_iota(jnp.int32, (Q_BLOCK, Q_BLOCK), 1)
  diag_ok = (kj // CHUNK) <= (qi // CHUNK)

  def scores(qb, h):
    rows = slice(Q_BLOCK * qb, Q_BLOCK * (qb + 1))
    wide = slice(B_QW * h, B_QW * (h + 1))
    qh = q_ref[0, rows, wide]
    s_d = jnp.where(diag_ok, _dot_nt(qh, k_ref[0, rows, wide]), -jnp.inf)
    s_p = _dot_nt(qh, k_ref[0, :Q_BLOCK * qb, wide]) if qb else None
    return s_d, s_p

  def finish(qb, h, s_d, s_p):
    rows = slice(Q_BLOCK * qb, Q_BLOCK * (qb + 1))
    wide = slice(B_QW * h, B_QW * (h + 1))
    narrow = slice(B_VDIM * h, B_VDIM * (h + 1))
    m = jnp.max(s_d, axis=-1, keepdims=True)
    if qb:
      m = jnp.maximum(m, jnp.max(s_p, axis=-1, keepdims=True))
    pv = _dot(jnp.exp2(s_d - m).astype(BF16), v_ref[0, rows, wide])
    if qb:
      pv = pv + _dot(jnp.exp2(s_p - m).astype(BF16), v_ref[0, :Q_BLOCK * qb, wide])
    o = pv[:, :B_VDIM] / pv[:, B_VDIM:]
    o_ref[0, rows, narrow] = (o * _silu(gate_ref[0, rows, narrow])).astype(BF16)

  items = [(qb, h) for qb in reversed(range(seq_len // Q_BLOCK)) for h in range(B_STEP_HEADS)]
  depth = 2
  pending = [scores(*it) for it in items[:depth]]
  for i, it in enumerate(items):
    if i + depth < len(items):
      pending.append(scores(*items[i + depth]))
    finish(*it, *pending.pop(0))


def _attn_b(q, k, v, gate):
  b, s, _ = q.shape
  wide = pl.BlockSpec((1, s, B_STEP_HEADS * B_QW), lambda i, h: (i, 0, h))
  narrow = pl.BlockSpec((1, s, B_STEP_HEADS * B_VDIM), lambda i, h: (i, 0, h))
  return pl.pallas_call(
      _attn_b_kernel,
      grid=(b, B_HEADS // B_STEP_HEADS),
      in_specs=[wide, wide, wide, narrow],
      out_specs=narrow,
      out_shape=jax.ShapeDtypeStruct((b, s, B_WIDTH), BF16),
      compiler_params=pltpu.CompilerParams(dimension_semantics=("parallel", "parallel"),
                                           vmem_limit_bytes=VMEM_LIMIT),
      name="attn_b",
  )(q, k, v, gate)


def _out_ple_kernel(ma_ref, mb_ref, h_ref, p_ref, woa_ref, wob_ref, gple_ref, wpg_ref, bpg_ref,
                    wpe_ref, gfin_ref, o_ref, *, final_norm):
  n_sub = 4 if final_norm else 1
  sub = h_ref.shape[0] // n_sub

  def stage1(j):
    rs = slice(sub * j, sub * (j + 1))
    h1 = h_ref[rs, :] + (_dot(ma_ref[rs, :], woa_ref[...]) + _dot(mb_ref[rs, :], wob_ref[...]))
    pe = _dot(p_ref[rs, :].astype(BF16), wpe_ref[...])
    return h1, pe

  def stage2(j, h1, pe):
    rs = slice(sub * j, sub * (j + 1))
    xn = _rms(h1, gple_ref[...]).astype(BF16)
    z = _dot(xn, wpg_ref[...]) + bpg_ref[...]
    h2 = h1 + pe * (1.0 / (1.0 + jnp.exp(-z)))
    o_ref[rs, :] = _rms(h2, gfin_ref[...]) if final_norm else h2

  pending = stage1(0)
  for j in range(n_sub):
    nxt = stage1(j + 1) if j + 1 < n_sub else None
    stage2(j, *pending)
    pending = nxt


def _out_ple(ma, mb, h, p, w, layer, final_norm):
  n = h.shape[0]
  t = OUT_TOKEN_TILE
  tok = lambda width: pl.BlockSpec((t, width), lambda i: (i, 0))
  w_out_half = lambda j: pl.BlockSpec((None, A_WIDTH, D_MODEL), lambda i: (layer, j, 0))
  rest = [w["gple"], w["wpg"], w["bpg"], w["wpe"]]
  return pl.pallas_call(
      functools.partial(_out_ple_kernel, final_norm=final_norm),
      grid=(n // t,),
      in_specs=([tok(A_WIDTH), tok(B_WIDTH), tok(D_MODEL),
                 pl.BlockSpec((None, t, D_PLE), lambda i: (layer, i, 0)),
                 w_out_half(0), w_out_half(1)] + [_layer_spec(a, layer) for a in rest]
                + [pl.BlockSpec(w["gfin"].shape, lambda i: (0, 0))]),
      out_specs=tok(D_MODEL),
      out_shape=jax.ShapeDtypeStruct((n, D_MODEL), F32),
      compiler_params=pltpu.CompilerParams(dimension_semantics=("parallel",),
                                           vmem_limit_bytes=VMEM_LIMIT),
      name="out_ple",
  )(ma, mb, h, p, w["w_out"], w["w_out"], *rest, w["gfin"])


def _prep_params(norm_mix, w_in, g_q, w_uq, g_kv, w_ukv, w_out, norm_ple, w_pe, w_pg, b_pg,
                 norm_final):
  depth = w_in.shape[0]
  w_t = jnp.swapaxes(w_in, 1, 2).astype(BF16)
  mla0 = 4 * A_WIDTH
  gb0 = mla0 + Q_LORA + KV_LORA + B_ROPE
  wmla = jnp.pad(w_t[:, mla0:gb0], ((0, 0), (0, LANES - B_ROPE), (0, 0)))
  uq =w_uq.reshape(depth, Q_LORA, B_HEADS, B_NOPE + B_ROPE)
  zpad = jnp.zeros((depth, Q_LORA, B_HEADS, LANES - B_ROPE), F32)
  wuq = jnp.concatenate([uq, zpad], axis=-1)
  return {
      "gmix": norm_mix[:, None, :],
      "w_t": w_t, "wgb": w_t[:, gb0:], "wmla": wmla,
      "gq": g_q[:, None, :], "gkv": g_kv[:, None, :],
      "wuq": wuq.reshape(depth, Q_LORA, B_HEADS * B_QW).astype(BF16),
      "wukv": w_ukv.astype(BF16),
      "w_out": w_out.astype(BF16),
      "gple": norm_ple[:, None, :], "wpg": w_pg.astype(BF16), "bpg": b_pg[:, None, :],
      "wpe": w_pe.astype(BF16),
      "gfin": norm_final[None, :],
  }


def kernel(x, p, positions, norm_mix, w_in, rel_bias, g_q, w_uq, g_kv, w_ukv, w_out,
           norm_ple, w_pe, w_pg, b_pg, norm_final):
  b, s, d = x.shape
  depth = w_in.shape[0]
  n = b * s
  w = _prep_params(norm_mix, w_in, g_q, w_uq, g_kv, w_ukv, w_out, norm_ple, w_pe, w_pg, b_pg,
                   norm_final)
  p = p.reshape(depth, n, D_PLE)
  h = x.reshape(n, d)
  seq = lambda a: a.reshape(b, s, a.shape[-1])
  tables = None
  tab = _bias_tables(rel_bias)
  for i in range(depth):
    if tables is None:
      *outs, cos, sin = _in_proj(h, w, i, positions=positions)
      tables = (cos, sin)
    else:
      outs = _in_proj(h, w, i, tables=tables)
    qa, ka, va, ga, gb, qb, kb, vb = outs
    ma = _attn_a(seq(qa), seq(ka), seq(va), seq(ga), tab, i)
    mb = _attn_b(seq(qb), seq(kb), seq(vb), seq(gb))
    h = _out_ple(ma.reshape(n, A_WIDTH), mb.reshape(n, B_WIDTH), h, p, w, i,
                 final_norm=(i == depth - 1))
  return h.reshape(b, s, d)
```

```python
import functools
import math

import jax
import jax.numpy as jnp
from jax import lax
from jax.experimental import pallas as pl
from jax.experimental.pallas import tpu as pltpu

D_MODEL = 1024
CHUNK = 64
D_PLE = 256
EPS = 1e-6
A_HEADS = 8
A_HEAD_DIM = 64
A_WIDTH = A_HEADS * A_HEAD_DIM
A_LEFT_CHUNKS = 8
REL_MAX = 128
B_HEADS = 4
B_NOPE = 128
B_ROPE = 64
B_VDIM = 128
B_WIDTH = B_HEADS * B_VDIM
Q_LORA = 256
KV_LORA = 128
ROPE_THETA = 10000.0

LANES = 128
TOKEN_TILE = 1024
OUT_TOKEN_TILE = 1024
Q_BLOCK = 256
A_STEP_PAIRS = 4
B_STEP_HEADS = 4
A_LEFT = A_LEFT_CHUNKS * CHUNK
A_WINDOW = A_LEFT + Q_BLOCK
B_QW = B_NOPE + LANES
LOG2E = math.log2(math.e)
B_QSCALE = (B_NOPE + B_ROPE) ** -0.5 * LOG2E
VMEM_LIMIT = 56 * 1024 * 1024

BF16 = jnp.bfloat16
F32 = jnp.float32


def _dot(a, b):
  return jnp.dot(a, b, preferred_element_type=F32)


def _dot_nt(a, b):
  return lax.dot_general(a, b, (((1,), (1,)), ((), ())), preferred_element_type=F32)


def _rms(x, g):
  y = x * lax.rsqrt(jnp.mean(x * x, axis=-1, keepdims=True) + EPS)
  return y * g


def _silu(g):
  return g * (1.0 / (1.0 + jnp.exp(-g)))


def _inv_freq_column():
  inv_freq = ROPE_THETA ** (-jnp.arange(0, B_ROPE, 2, dtype=F32) / B_ROPE)
  return inv_freq[:, None]


def _rope_tile(pos_row, inv_freq_col):
  t = pos_row.shape[1]
  ang = pos_row.astype(F32) * inv_freq_col
  c, s = jnp.cos(ang), jnp.sin(ang)
  zeros = jnp.zeros((LANES - B_ROPE, t), F32)
  return jnp.concatenate([c, c, zeros], axis=0).T, jnp.concatenate([-s, s, zeros], axis=0).T


G_LEN = 1024


def _bias_table_kernel(g_ref, tab_ref):
  qi = lax.broadcasted_iota(jnp.int32, (Q_BLOCK, A_WINDOW), 0)
  kk = lax.broadcasted_iota(jnp.int32, (Q_BLOCK, A_WINDOW), 1)
  qc = (qi + A_LEFT) // CHUNK
  kc = kk // CHUNK
  valid = (kc <= qc) & (kc >= qc - A_LEFT_CHUNKS)
  for h in range(A_HEADS):
    g = jnp.broadcast_to(g_ref[0, h], (Q_BLOCK, G_LEN))
    t = pltpu.roll(g, G_LEN - Q_BLOCK, 1, stride=1, stride_axis=0)[:, :A_WINDOW]
    tab_ref[0, h] = jnp.where(valid, (t - g[:, :1]) * LOG2E, -jnp.inf)


def _bias_tables(rel_bias):
  depth = rel_bias.shape[0]
  flat = A_LEFT + Q_BLOCK - REL_MAX
  rev = rel_bias[..., ::-1]
  g = jnp.concatenate([
      jnp.broadcast_to(rel_bias[..., -1:], (depth, A_HEADS, flat)),
      rev[..., : 2 * REL_MAX],
      jnp.broadcast_to(rel_bias[..., :1], (depth, A_HEADS, G_LEN - flat - 2 * REL_MAX)),
  ], axis=-1)[:, :, None, :]
  return pl.pallas_call(
      _bias_table_kernel,
      grid=(depth,),
      in_specs=[pl.BlockSpec((1, A_HEADS, 1, G_LEN), lambda l: (l, 0, 0, 0))],
      out_specs=pl.BlockSpec((1, A_HEADS, Q_BLOCK, A_WINDOW), lambda l: (l, 0, 0, 0)),
      out_shape=jax.ShapeDtypeStruct((depth, A_HEADS, Q_BLOCK, A_WINDOW), F32),
      compiler_params=pltpu.CompilerParams(dimension_semantics=("parallel",)),
      name="bias_table",
  )(g)


def _in_proj_kernel(h_ref, gmix_ref, wq_ref, wk_ref, wv_ref, wga_ref, wgb_ref, wmla_ref,
                    gq_ref, gkv_ref, wuq_ref, wukv_ref, *refs, make_tables):
  if make_tables:
    (pos_ref, invf_ref, q_out, k_out, v_out, ga_out, gb_out, qb_out, kb_out, vb_out,
     cos_out, sin_out) = refs
  else:
    cos_ref, sin_ref, q_out, k_out, v_out, ga_out, gb_out, qb_out, kb_out, vb_out = refs
  xn = _rms(h_ref[...], gmix_ref[...]).astype(BF16)
  zm = _dot_nt(xn, wmla_ref[...])
  cqn = _rms(zm[:, :Q_LORA], gq_ref[...]).astype(BF16)
  ckvn = _rms(zm[:, Q_LORA:Q_LORA + KV_LORA], gkv_ref[...]).astype(BF16)
  qall = _dot(cqn, wuq_ref[...])
  kv = _dot(ckvn, wukv_ref[...])

  q_out[...] = (_dot_nt(xn, wq_ref[...]) * (A_HEAD_DIM ** -0.5 * LOG2E)).astype(BF16)
  k_out[...] = _dot_nt(xn, wk_ref[...]).astype(BF16)
  v_out[...] = _dot_nt(xn, wv_ref[...]).astype(BF16)
  ga_out[...] = _dot_nt(xn, wga_ref[...])
  gb_out[...] = _dot_nt(xn, wgb_ref[...])

  if make_tables:
    cos, sin = _rope_tile(pos_ref[0], invf_ref[...])
    cos_out[...] = cos
    sin_out[...] = sin
  else:
    cos = cos_ref[...]
    sin = sin_ref[...]
  first_half = lax.broadcasted_iota(jnp.int32, (h_ref.shape[0], LANES), 1) < B_ROPE // 2

  def rope(g):
    swapped = jnp.where(first_half, pltpu.roll(g, LANES - B_ROPE // 2, 1), pltpu.roll(g, B_ROPE // 2, 1))
    return g * cos + swapped * sin

  k_rope = rope(zm[:, Q_LORA + KV_LORA:]).astype(BF16)
  ones = jnp.ones((h_ref.shape[0], B_QW - B_VDIM), BF16)
  for h in range(B_HEADS):
    qb = B_QW * h
    qb_out[:, qb:qb + B_NOPE] = (qall[:, qb:qb + B_NOPE] * B_QSCALE).astype(BF16)
    qb_out[:, qb + B_NOPE:qb + B_QW] = (rope(qall[:, qb + B_NOPE:qb + B_QW]) * B_QSCALE).astype(BF16)
    kb_out[:, B_QW * h:B_QW * h + B_NOPE] = kv[:, 2 * LANES * h:2 * LANES * h + B_NOPE].astype(BF16)
    kb_out[:, B_QW * h + B_NOPE:B_QW * (h + 1)] = k_rope
    vb_out[:, B_QW * h:B_QW * h + B_VDIM] = kv[:, 2 * LANES * h + B_NOPE:2 * LANES * (h + 1)].astype(BF16)
    vb_out[:, B_QW * h + B_VDIM:B_QW * (h + 1)] = ones


def _layer_spec(a, layer):
  return pl.BlockSpec((None,) + a.shape[1:], lambda i: (layer,) + (0,) * (a.ndim - 1),
                      pipeline_mode=pl.Buffered(1))


def _in_proj(h, w, layer, positions=None, tables=None):
  n = h.shape[0]
  t = TOKEN_TILE
  tok = lambda width: pl.BlockSpec((t, width), lambda i: (i, 0))
  w_in_block = lambda j: pl.BlockSpec((None, A_WIDTH, D_MODEL), lambda i: (layer, j, 0),
                                      pipeline_mode=pl.Buffered(1))
  small = [w["gmix"], w["wgb"], w["wmla"], w["gq"], w["gkv"], w["wuq"], w["wukv"]]
  in_specs = ([tok(D_MODEL), _layer_spec(small[0], layer)] + [w_in_block(j) for j in range(4)]
              + [_layer_spec(a, layer) for a in small[1:]])
  operands = [h, small[0]] + [w["w_t"]] * 4 + small[1:]
  out_widths = [(A_WIDTH, BF16), (A_WIDTH, BF16), (A_WIDTH, BF16), (A_WIDTH, F32), (B_WIDTH, F32),
                (B_HEADS * B_QW, BF16), (B_HEADS * B_QW, BF16), (B_HEADS * B_QW, BF16)]
  make_tables = tables is None
  if make_tables:
    invf = _inv_freq_column()
    operands += [positions.reshape(n // t, 1, t), invf]
    in_specs += [pl.BlockSpec((1, 1, t), lambda i: (i, 0, 0)), pl.BlockSpec(invf.shape, lambda i: (0, 0))]
    out_widths += [(LANES, F32), (LANES, F32)]
  else:
    operands += list(tables)
    in_specs += [tok(LANES), tok(LANES)]
  return pl.pallas_call(
      functools.partial(_in_proj_kernel, make_tables=make_tables),
      grid=(n // t,),
      in_specs=in_specs,
      out_specs=[tok(wd) for wd, _ in out_widths],
      out_shape=[jax.ShapeDtypeStruct((n, wd), dt) for wd, dt in out_widths],
      compiler_params=pltpu.CompilerParams(dimension_semantics=("parallel",),
                                           vmem_limit_bytes=VMEM_LIMIT),
      name="in_proj",
  )(*operands)


def _attn_a_kernel(q_ref, k_ref, v_ref, gate_ref, tab_ref, o_ref):
  seq_len = q_ref.shape[1]
  lane = lax.broadcasted_iota(jnp.int32, (1, LANES), 1)
  head_rows = ((lane < A_HEAD_DIM).astype(BF16), (lane >= A_HEAD_DIM).astype(BF16))
  lo_mask = lax.broadcasted_iota(jnp.int32, (Q_BLOCK, LANES), 1) < A_HEAD_DIM
  ones = jnp.ones((A_WINDOW, LANES), BF16)

  def window(qb):
    k0 = max(0, Q_BLOCK * qb - A_LEFT)
    return k0, Q_BLOCK * (qb + 1) - k0

  half = Q_BLOCK // 2
  table_zero = ((half, A_LEFT - REL_MAX), (2 * half, A_LEFT - REL_MAX + half))

  def panels(qb):
    _, nk = window(qb)
    first_live = half if nk == A_WINDOW else 0
    return (slice(0, half), slice(0, nk - half)), (slice(half, Q_BLOCK), slice(first_live, nk))

  def scores(qb, pr, hh):
    k0, nk = window(qb)
    ls = slice(LANES * pr, LANES * (pr + 1))
    qp = q_ref[0, Q_BLOCK * qb:Q_BLOCK * (qb + 1), ls]
    s = _dot_nt(qp * head_rows[hh], k_ref[0, k0:k0 + nk, ls])
    col0 = A_WINDOW - nk
    out = []
    for (rs, cs), zero in zip(panels(qb), table_zero):
      cuts = sorted({col0 + cs.start, col0 + cs.stop,
                     *(min(max(z, col0 + cs.start), col0 + cs.stop) for z in zero)})
      pieces = []
      for c0, c1 in zip(cuts[:-1], cuts[1:]):
        piece = s[rs, c0 - col0:c1 - col0]
        if not zero[0] <= c0 < zero[1]:
          piece = piece + tab_ref[2 * pr + hh, rs, c0:c1]
        pieces.append(piece)
      out.append(jnp.concatenate(pieces, axis=1))
    return out

  def weighted(qb, pr, s_panels):
    k0, nk = window(qb)
    vp = jnp.concatenate([v_ref[0, k0:k0 + nk, LANES * pr:LANES * (pr + 1)], ones[:nk]], axis=1)
    rows = []
    for s, (_, cs) in zip(s_panels, panels(qb)):
      e = jnp.exp2(s - jnp.max(s, axis=-1, keepdims=True)).astype(BF16)
      dead = [jnp.zeros((half, w), BF16) for w in (cs.start, nk - cs.stop)]
      rows.append(jnp.concatenate([x for x in (dead[0], e, dead[1]) if x.shape[1]], axis=1))
    pv = _dot(jnp.concatenate(rows, axis=0), vp)
    return pv[:, :LANES] / pv[:, LANES:]

  items = [(qb, pr, hh) for qb in reversed(range(seq_len // Q_BLOCK))
           for pr in range(A_STEP_PAIRS) for hh in range(2)]
  depth = 2
  pending = [scores(*it) for it in items[:depth]]
  outs = []
  for i, (qb, pr, hh) in enumerate(items):
    if i + depth < len(items):
      pending.append(scores(*items[i + depth]))
    outs.append(weighted(qb, pr, pending.pop(0)))
    if hh == 1:
      rows = slice(Q_BLOCK * qb, Q_BLOCK * (qb + 1))
      ls = slice(LANES * pr, LANES * (pr + 1))
      o = jnp.where(lo_mask, outs[0], outs[1])
      o_ref[0, rows, ls] = (o * _silu(gate_ref[0, rows, ls])).astype(BF16)
      outs = []


def _attn_a(q, k, v, gate, tab, layer):
  b, s, _ = q.shape
  pair = pl.BlockSpec((1, s, A_STEP_PAIRS * LANES), lambda p, i: (i, 0, p))
  return pl.pallas_call(
      _attn_a_kernel,
      grid=(A_HEADS // (2 * A_STEP_PAIRS), b),
      in_specs=[pair, pair, pair, pair,
                pl.BlockSpec((None, 2 * A_STEP_PAIRS, Q_BLOCK, A_WINDOW),
                             lambda p, i: (layer, p, 0, 0))],
      out_specs=pair,
      out_shape=jax.ShapeDtypeStruct((b, s, A_WIDTH), BF16),
      compiler_params=pltpu.CompilerParams(dimension_semantics=("parallel", "parallel"),
                                           vmem_limit_bytes=VMEM_LIMIT),
      name="attn_a",
  )(q, k, v, gate, tab)


def _attn_b_kernel(q_ref, k_ref, v_ref, gate_ref, o_ref):
  seq_len = q_ref.shape[1]
  qi = lax.broadcasted_iota(jnp.int32, (Q_BLOCK, Q_BLOCK), 0)
  kj = lax.broadcasted_iota(jnp.int32, (Q_BLOCK, Q_BLOCK), 1)
  diag_ok = (kj // CHUNK) <= (qi // CHUNK)

  def scores(qb, h):
    rows = slice(Q_BLOCK * qb, Q_BLOCK * (qb + 1))
    wide = slice(B_QW * h, B_QW * (h + 1))
    qh = q_ref[0, rows, wide]
    s = _dot_nt(qh, k_ref[0, :Q_BLOCK * (qb + 1), wide])
    s_d = jnp.where(diag_ok, s[:, Q_BLOCK * qb:], -jnp.inf)
    return s_d, (s[:, :Q_BLOCK * qb] if qb else None)

  def finish(qb, h, s_d, s_p):
    rows = slice(Q_BLOCK * qb, Q_BLOCK * (qb + 1))
    wide = slice(B_QW * h, B_QW * (h + 1))
    narrow = slice(B_VDIM * h, B_VDIM * (h + 1))
    m = jnp.max(s_d, axis=-1, keepdims=True)
    if qb:
      m = jnp.maximum(m, jnp.max(s_p, axis=-1, keepdims=True))
    e = jnp.exp2(s_d - m).astype(BF16)
    if qb:
      e = jnp.concatenate([jnp.exp2(s_p - m).astype(BF16), e], axis=1)
    pv = _dot(e, v_ref[0, :Q_BLOCK * (qb + 1), wide])
    o = pv[:, :B_VDIM] / pv[:, B_VDIM:]
    o_ref[0, rows, narrow] = (o * _silu(gate_ref[0, rows, narrow])).astype(BF16)

  items = [(qb, h) for qb in reversed(range(seq_len // Q_BLOCK)) for h in range(B_STEP_HEADS)]
  depth = 2
  pending = [scores(*it) for it in items[:depth]]
  for i, it in enumerate(items):
    if i + depth < len(items):
      pending.append(scores(*items[i + depth]))
    finish(*it, *pending.pop(0))


def _attn_b(q, k, v, gate):
  b, s, _ = q.shape
  wide = pl.BlockSpec((1, s, B_STEP_HEADS * B_QW), lambda i, h: (i, 0, h))
  narrow = pl.BlockSpec((1, s, B_STEP_HEADS * B_VDIM), lambda i, h: (i, 0, h))
  return pl.pallas_call(
      _attn_b_kernel,
      grid=(b, B_HEADS // B_STEP_HEADS),
      in_specs=[wide, wide, wide, narrow],
      out_specs=narrow,
      out_shape=jax.ShapeDtypeStruct((b, s, B_WIDTH), BF16),
      compiler_params=pltpu.CompilerParams(dimension_semantics=("parallel", "parallel"),
                                           vmem_limit_bytes=VMEM_LIMIT),
      name="attn_b",
  )(q, k, v, gate)


def _out_ple_kernel(ma_ref, mb_ref, h_ref, p_ref, woa_ref, wob_ref, gple_ref, wpg_ref, bpg_ref,
                    wpe_ref, gfin_ref, o_ref, *, final_norm):
  n_sub = 1
  sub = h_ref.shape[0] // n_sub

  def stage1(j):
    rs = slice(sub * j, sub * (j + 1))
    h1 = h_ref[rs, :] + (_dot(ma_ref[rs, :], woa_ref[...]) + _dot(mb_ref[rs, :], wob_ref[...]))
    pe = _dot(p_ref[rs, :].astype(BF16), wpe_ref[...])
    return h1, pe

  def stage2(j, h1, pe):
    rs = slice(sub * j, sub * (j + 1))
    xn = _rms(h1, gple_ref[...]).astype(BF16)
    z = _dot(xn, wpg_ref[...]) + bpg_ref[...]
    h2 = h1 + pe * (1.0 / (1.0 + jnp.exp(-z)))
    o_ref[rs, :] = _rms(h2, gfin_ref[...]) if final_norm else h2

  pending = stage1(0)
  for j in range(n_sub):
    nxt = stage1(j + 1) if j + 1 < n_sub else None
    stage2(j, *pending)
    pending = nxt


def _out_ple(ma, mb, h, p, w, layer, final_norm):
  n = h.shape[0]
  t = OUT_TOKEN_TILE
  tok = lambda width: pl.BlockSpec((t, width), lambda i: (i, 0))
  w_out_half = lambda j: pl.BlockSpec((None, A_WIDTH, D_MODEL), lambda i: (layer, j, 0))
  rest = [w["gple"], w["wpg"], w["bpg"], w["wpe"]]
  return pl.pallas_call(
      functools.partial(_out_ple_kernel, final_norm=final_norm),
      grid=(n // t,),
      in_specs=([tok(A_WIDTH), tok(B_WIDTH), tok(D_MODEL),
                 pl.BlockSpec((None, t, D_PLE), lambda i: (layer, i, 0)),
                 w_out_half(0), w_out_half(1)] + [_layer_spec(a, layer) for a in rest]
                + [pl.BlockSpec(w["gfin"].shape, lambda i: (0, 0))]),
      out_specs=tok(D_MODEL),
      out_shape=jax.ShapeDtypeStruct((n, D_MODEL), F32),
      compiler_params=pltpu.CompilerParams(dimension_semantics=("parallel",),
                                           vmem_limit_bytes=VMEM_LIMIT),
      name="out_ple",
  )(ma, mb, h, p, w["w_out"], w["w_out"], *rest, w["gfin"])


def _prep_params(norm_mix, w_in, g_q, w_uq, g_kv, w_ukv, w_out, norm_ple, w_pe, w_pg, b_pg,
                 norm_final):
  depth = w_in.shape[0]
  w_t = jnp.swapaxes(w_in, 1, 2).astype(BF16)
  mla0 = 4 * A_WIDTH
  gb0 = mla0 + Q_LORA + KV_LORA + B_ROPE
  wmla = jnp.pad(w_t[:, mla0:gb0], ((0, 0), (0, LANES - B_ROPE), (0, 0)))
  uq =w_uq.reshape(depth, Q_LORA, B_HEADS, B_NOPE + B_ROPE)
  zpad = jnp.zeros((depth, Q_LORA, B_HEADS, LANES - B_ROPE), F32)
  wuq = jnp.concatenate([uq, zpad], axis=-1)
  return {
      "gmix": norm_mix[:, None, :],
      "w_t": w_t, "wgb": w_t[:, gb0:], "wmla": wmla,
      "gq": g_q[:, None, :], "gkv": g_kv[:, None, :],
      "wuq": wuq.reshape(depth, Q_LORA, B_HEADS * B_QW).astype(BF16),
      "wukv": w_ukv.astype(BF16),
      "w_out": w_out.astype(BF16),
      "gple": norm_ple[:, None, :], "wpg": w_pg.astype(BF16), "bpg": b_pg[:, None, :],
      "wpe": w_pe.astype(BF16),
      "gfin": norm_final[None, :],
  }


def kernel(x, p, positions, norm_mix, w_in, rel_bias, g_q, w_uq, g_kv, w_ukv, w_out,
           norm_ple, w_pe, w_pg, b_pg, norm_final):
  b, s, d = x.shape
  depth = w_in.shape[0]
  n = b * s
  w = _prep_params(norm_mix, w_in, g_q, w_uq, g_kv, w_ukv, w_out, norm_ple, w_pe, w_pg, b_pg,
                   norm_final)
  p = p.reshape(depth, n, D_PLE)
  h = x.reshape(n, d)
  seq = lambda a: a.reshape(b, s, a.shape[-1])
  tables = None
  tab = _bias_tables(rel_bias)
  for i in range(depth):
    if tables is None:
      *outs, cos, sin = _in_proj(h, w, i, positions=positions)
      tables = (cos, sin)
    else:
      outs = _in_proj(h, w, i, tables=tables)
    qa, ka, va, ga, gb, qb, kb, vb = outs
    ma = _attn_a(seq(qa), seq(ka), seq(va), seq(ga), tab, i)
    mb = _attn_b(seq(qb), seq(kb), seq(vb), seq(gb))
    h = _out_ple(ma.reshape(n, A_WIDTH), mb.reshape(n, B_WIDTH), h, p, w, i,
                 final_norm=(i == depth - 1))
  return h.reshape(b, s, d)
```

```python
import functools
import math

import jax
import jax.numpy as jnp
from jax import lax
from jax.experimental import pallas as pl
from jax.experimental.pallas import tpu as pltpu

D_MODEL = 1024
CHUNK = 64
D_PLE = 256
EPS = 1e-6
A_HEADS = 8
A_HEAD_DIM = 64
A_WIDTH = A_HEADS * A_HEAD_DIM
A_LEFT_CHUNKS = 8
REL_MAX = 128
B_HEADS = 4
B_NOPE = 128
B_ROPE = 64
B_VDIM = 128
B_WIDTH = B_HEADS * B_VDIM
Q_LORA = 256
KV_LORA = 128
ROPE_THETA = 10000.0

LANES = 128
TOKEN_TILE = 1024
OUT_TOKEN_TILE = 1024
Q_BLOCK = 256
A_STEP_PAIRS = 4
B_STEP_HEADS = 4
A_LEFT = A_LEFT_CHUNKS * CHUNK
A_WINDOW = A_LEFT + Q_BLOCK
B_QW = B_NOPE + LANES
LOG2E = math.log2(math.e)
B_QSCALE = (B_NOPE + B_ROPE) ** -0.5 * LOG2E
VMEM_LIMIT = 56 * 1024 * 1024

BF16 = jnp.bfloat16
F32 = jnp.float32


def _dot(a, b):
  return jnp.dot(a, b, preferred_element_type=F32)


def _dot_nt(a, b):
  return lax.dot_general(a, b, (((1,), (1,)), ((), ())), preferred_element_type=F32)


def _rms(x, g):
  y = x * lax.rsqrt(jnp.mean(x * x, axis=-1, keepdims=True) + EPS)
  return y * g


def _silu(g):
  return g * (1.0 / (1.0 + jnp.exp(-g)))


def _inv_freq_column():
  inv_freq = ROPE_THETA ** (-jnp.arange(0, B_ROPE, 2, dtype=F32) / B_ROPE)
  return inv_freq[:, None]


def _rope_tile(pos_row, inv_freq_col):
  t = pos_row.shape[1]
  ang = pos_row.astype(F32) * inv_freq_col
  c, s = jnp.cos(ang), jnp.sin(ang)
  zeros = jnp.zeros((LANES - B_ROPE, t), F32)
  return jnp.concatenate([c, c, zeros], axis=0).T, jnp.concatenate([-s, s, zeros], axis=0).T


G_LEN = 1024


def _bias_table_kernel(g_ref, tab_ref):
  qi = lax.broadcasted_iota(jnp.int32, (Q_BLOCK, A_WINDOW), 0)
  kk = lax.broadcasted_iota(jnp.int32, (Q_BLOCK, A_WINDOW), 1)
  qc = (qi + A_LEFT) // CHUNK
  kc = kk // CHUNK
  valid = (kc <= qc) & (kc >= qc - A_LEFT_CHUNKS)
  for h in range(A_HEADS):
    g = jnp.broadcast_to(g_ref[0, h], (Q_BLOCK, G_LEN))
    t = pltpu.roll(g, G_LEN - Q_BLOCK, 1, stride=1, stride_axis=0)[:, :A_WINDOW]
    tab_ref[0, h] = jnp.where(valid, (t - g[:, :1]) * LOG2E, -jnp.inf)


def _bias_tables(rel_bias):
  depth = rel_bias.shape[0]
  flat = A_LEFT + Q_BLOCK - REL_MAX
  rev = rel_bias[..., ::-1]
  g = jnp.concatenate([
      jnp.broadcast_to(rel_bias[..., -1:], (depth, A_HEADS, flat)),
      rev[..., : 2 * REL_MAX],
      jnp.broadcast_to(rel_bias[..., :1], (depth, A_HEADS, G_LEN - flat - 2 * REL_MAX)),
  ], axis=-1)[:, :, None, :]
  return pl.pallas_call(
      _bias_table_kernel,
      grid=(depth,),
      in_specs=[pl.BlockSpec((1, A_HEADS, 1, G_LEN), lambda l: (l, 0, 0, 0))],
      out_specs=pl.BlockSpec((1, A_HEADS, Q_BLOCK, A_WINDOW), lambda l: (l, 0, 0, 0)),
      out_shape=jax.ShapeDtypeStruct((depth, A_HEADS, Q_BLOCK, A_WINDOW), F32),
      compiler_params=pltpu.CompilerParams(dimension_semantics=("parallel",)),
      name="bias_table",
  )(g)


def _in_proj_kernel(h_ref, gmix_ref, wq_ref, wk_ref, wv_ref, wga_ref, wgb_ref, wmla_ref,
                    gq_ref, gkv_ref, wuq_ref, wukv_ref, *refs, make_tables):
  if make_tables:
    (pos_ref, invf_ref, q_out, k_out, v_out, ga_out, gb_out, qb_out, kb_out, vb_out,
     cos_out, sin_out) = refs
  else:
    cos_ref, sin_ref, q_out, k_out, v_out, ga_out, gb_out, qb_out, kb_out, vb_out = refs
  xn = _rms(h_ref[...], gmix_ref[...]).astype(BF16)
  zm = _dot_nt(xn, wmla_ref[...])
  cqn = _rms(zm[:, :Q_LORA], gq_ref[...]).astype(BF16)
  ckvn = _rms(zm[:, Q_LORA:Q_LORA + KV_LORA], gkv_ref[...]).astype(BF16)
  qall = _dot(cqn, wuq_ref[...])
  kv = _dot(ckvn, wukv_ref[...])

  q_out[...] = (_dot_nt(xn, wq_ref[...]) * (A_HEAD_DIM ** -0.5 * LOG2E)).astype(BF16)
  k_out[...] = _dot_nt(xn, wk_ref[...]).astype(BF16)
  v_out[...] = _dot_nt(xn, wv_ref[...]).astype(BF16)
  ga_out[...] = _dot_nt(xn, wga_ref[...])
  gb_out[...] = _dot_nt(xn, wgb_ref[...])

  if make_tables:
    cos, sin = _rope_tile(pos_ref[0], invf_ref[...])
    cos_out[...] = cos
    sin_out[...] = sin
  else:
    cos = cos_ref[...]
    sin = sin_ref[...]
  first_half = lax.broadcasted_iota(jnp.int32, (h_ref.shape[0], LANES), 1) < B_ROPE // 2

  def rope(g):
    swapped = jnp.where(first_half, pltpu.roll(g, LANES - B_ROPE // 2, 1), pltpu.roll(g, B_ROPE // 2, 1))
    return g * cos + swapped * sin

  k_rope = rope(zm[:, Q_LORA + KV_LORA:]).astype(BF16)
  for h in range(B_HEADS):
    qb = B_QW * h
    qb_out[:, qb:qb + B_NOPE] = (qall[:, qb:qb + B_NOPE] * B_QSCALE).astype(BF16)
    qb_out[:, qb + B_NOPE:qb + B_QW] = (rope(qall[:, qb + B_NOPE:qb + B_QW]) * B_QSCALE).astype(BF16)
    kb_out[:, B_QW * h:B_QW * h + B_NOPE] = kv[:, 2 * LANES * h:2 * LANES * h + B_NOPE].astype(BF16)
    kb_out[:, B_QW * h + B_NOPE:B_QW * (h + 1)] = k_rope
    vb_out[:, B_VDIM * h:B_VDIM * (h + 1)] = kv[:, 2 * LANES * h + B_NOPE:2 * LANES * (h + 1)].astype(BF16)


def _layer_spec(a, layer):
  return pl.BlockSpec((None,) + a.shape[1:], lambda i: (layer,) + (0,) * (a.ndim - 1),
                      pipeline_mode=pl.Buffered(1))


def _in_proj(h, w, layer, positions=None, tables=None):
  n = h.shape[0]
  t = TOKEN_TILE
  tok = lambda width: pl.BlockSpec((t, width), lambda i: (i, 0))
  w_in_block = lambda j: pl.BlockSpec((None, A_WIDTH, D_MODEL), lambda i: (layer, j, 0),
                                      pipeline_mode=pl.Buffered(1))
  small = [w["gmix"], w["wgb"], w["wmla"], w["gq"], w["gkv"], w["wuq"], w["wukv"]]
  in_specs = ([tok(D_MODEL), _layer_spec(small[0], layer)] + [w_in_block(j) for j in range(4)]
              + [_layer_spec(a, layer) for a in small[1:]])
  operands = [h, small[0]] + [w["w_t"]] * 4 + small[1:]
  out_widths = [(A_WIDTH, BF16), (A_WIDTH, BF16), (A_WIDTH, BF16), (A_WIDTH, F32), (B_WIDTH, F32),
                (B_HEADS * B_QW, BF16), (B_HEADS * B_QW, BF16), (B_WIDTH, BF16)]
  make_tables = tables is None
  if make_tables:
    invf = _inv_freq_column()
    operands += [positions.reshape(n // t, 1, t), invf]
    in_specs += [pl.BlockSpec((1, 1, t), lambda i: (i, 0, 0)), pl.BlockSpec(invf.shape, lambda i: (0, 0))]
    out_widths += [(LANES, F32), (LANES, F32)]
  else:
    operands += list(tables)
    in_specs += [tok(LANES), tok(LANES)]
  return pl.pallas_call(
      functools.partial(_in_proj_kernel, make_tables=make_tables),
      grid=(n // t,),
      in_specs=in_specs,
      out_specs=[tok(wd) for wd, _ in out_widths],
      out_shape=[jax.ShapeDtypeStruct((n, wd), dt) for wd, dt in out_widths],
      compiler_params=pltpu.CompilerParams(dimension_semantics=("parallel",),
                                           vmem_limit_bytes=VMEM_LIMIT),
      name="in_proj",
  )(*operands)


def _attn_a_kernel(q_ref, k_ref, v_ref, gate_ref, tab_ref, o_ref):
  seq_len = q_ref.shape[1]
  lane = lax.broadcasted_iota(jnp.int32, (1, LANES), 1)
  head_rows = ((lane < A_HEAD_DIM).astype(BF16), (lane >= A_HEAD_DIM).astype(BF16))
  lo_mask = lax.broadcasted_iota(jnp.int32, (Q_BLOCK, LANES), 1) < A_HEAD_DIM
  ones = jnp.ones((A_WINDOW, LANES), BF16)

  def window(qb):
    k0 = max(0, Q_BLOCK * qb - A_LEFT)
    return k0, Q_BLOCK * (qb + 1) - k0

  half = Q_BLOCK // 2
  table_zero = ((half, A_LEFT - REL_MAX), (2 * half, A_LEFT - REL_MAX + half))

  def panels(qb):
    _, nk = window(qb)
    first_live = half if nk == A_WINDOW else 0
    return (slice(0, half), slice(0, nk - half)), (slice(half, Q_BLOCK), slice(first_live, nk))

  def scores(qb, pr, hh):
    k0, nk = window(qb)
    ls = slice(LANES * pr, LANES * (pr + 1))
    qp = q_ref[0, Q_BLOCK * qb:Q_BLOCK * (qb + 1), ls]
    s = _dot_nt(qp * head_rows[hh], k_ref[0, k0:k0 + nk, ls])
    col0 = A_WINDOW - nk
    out = []
    for (rs, cs), zero in zip(panels(qb), table_zero):
      cuts = sorted({col0 + cs.start, col0 + cs.stop,
                     *(min(max(z, col0 + cs.start), col0 + cs.stop) for z in zero)})
      pieces = []
      for c0, c1 in zip(cuts[:-1], cuts[1:]):
        piece = s[rs, c0 - col0:c1 - col0]
        if not zero[0] <= c0 < zero[1]:
          piece = piece + tab_ref[2 * pr + hh, rs, c0:c1]
        pieces.append(piece)
      out.append(jnp.concatenate(pieces, axis=1))
    return out

  def weighted(qb, pr, s_panels):
    k0, nk = window(qb)
    vp = jnp.concatenate([v_ref[0, k0:k0 + nk, LANES * pr:LANES * (pr + 1)], ones[:nk]], axis=1)
    rows = []
    for s, (_, cs) in zip(s_panels, panels(qb)):
      e = jnp.exp2(s - jnp.max(s, axis=-1, keepdims=True)).astype(BF16)
      dead = [jnp.zeros((half, w), BF16) for w in (cs.start, nk - cs.stop)]
      rows.append(jnp.concatenate([x for x in (dead[0], e, dead[1]) if x.shape[1]], axis=1))
    pv = _dot(jnp.concatenate(rows, axis=0), vp)
    return pv[:, :LANES] / pv[:, LANES:]

  items = [(qb, pr, hh) for qb in reversed(range(seq_len // Q_BLOCK))
           for pr in range(A_STEP_PAIRS) for hh in range(2)]
  depth = 2
  pending = [scores(*it) for it in items[:depth]]
  outs = []
  for i, (qb, pr, hh) in enumerate(items):
    if i + depth < len(items):
      pending.append(scores(*items[i + depth]))
    outs.append(weighted(qb, pr, pending.pop(0)))
    if hh == 1:
      rows = slice(Q_BLOCK * qb, Q_BLOCK * (qb + 1))
      ls = slice(LANES * pr, LANES * (pr + 1))
      o = jnp.where(lo_mask, outs[0], outs[1])
      o_ref[0, rows, ls] = (o * _silu(gate_ref[0, rows, ls])).astype(BF16)
      outs = []


def _attn_a(q, k, v, gate, tab, layer):
  b, s, _ = q.shape
  pair = pl.BlockSpec((1, s, A_STEP_PAIRS * LANES), lambda p, i: (i, 0, p))
  return pl.pallas_call(
      _attn_a_kernel,
      grid=(A_HEADS // (2 * A_STEP_PAIRS), b),
      in_specs=[pair, pair, pair, pair,
                pl.BlockSpec((None, 2 * A_STEP_PAIRS, Q_BLOCK, A_WINDOW),
                             lambda p, i: (layer, p, 0, 0))],
      out_specs=pair,
      out_shape=jax.ShapeDtypeStruct((b, s, A_WIDTH), BF16),
      compiler_params=pltpu.CompilerParams(dimension_semantics=("parallel", "parallel"),
                                           vmem_limit_bytes=VMEM_LIMIT),
      name="attn_a",
  )(q, k, v, gate, tab)


def _attn_b_kernel(q_ref, k_ref, v_ref, gate_ref, o_ref):
  seq_len = q_ref.shape[1]
  ones = jnp.ones((seq_len, B_QW - B_VDIM), BF16)
  qi = lax.broadcasted_iota(jnp.int32, (Q_BLOCK, Q_BLOCK), 0)
  kj = lax.broadcasted_iota(jnp.int32, (Q_BLOCK, Q_BLOCK), 1)
  diag_ok = (kj // CHUNK) <= (qi // CHUNK)

  def scores(qb, h):
    rows = slice(Q_BLOCK * qb, Q_BLOCK * (qb + 1))
    wide = slice(B_QW * h, B_QW * (h + 1))
    qh = q_ref[0, rows, wide]
    s = _dot_nt(qh, k_ref[0, :Q_BLOCK * (qb + 1), wide])
    s_d = jnp.where(diag_ok, s[:, Q_BLOCK * qb:], -jnp.inf)
    return s_d, (s[:, :Q_BLOCK * qb] if qb else None)

  def finish(qb, h, s_d, s_p):
    rows = slice(Q_BLOCK * qb, Q_BLOCK * (qb + 1))
    wide = slice(B_QW * h, B_QW * (h + 1))
    narrow = slice(B_VDIM * h, B_VDIM * (h + 1))
    m = jnp.max(s_d, axis=-1, keepdims=True)
    if qb:
      m = jnp.maximum(m, jnp.max(s_p, axis=-1, keepdims=True))
    e = jnp.exp2(s_d - m).astype(BF16)
    if qb:
      e = jnp.concatenate([jnp.exp2(s_p - m).astype(BF16), e], axis=1)
    nk = Q_BLOCK * (qb + 1)
    pv = _dot(e, jnp.concatenate([v_ref[0, :nk, narrow], ones[:nk]], axis=1))
    o = pv[:, :B_VDIM] / pv[:, B_VDIM:]
    o_ref[0, rows, narrow] = (o * _silu(gate_ref[0, rows, narrow])).astype(BF16)

  items = [(qb, h) for qb in reversed(range(seq_len // Q_BLOCK)) for h in range(B_STEP_HEADS)]
  depth = 2
  pending = [scores(*it) for it in items[:depth]]
  for i, it in enumerate(items):
    if i + depth < len(items):
      pending.append(scores(*items[i + depth]))
    finish(*it, *pending.pop(0))


def _attn_b(q, k, v, gate):
  b, s, _ = q.shape
  wide = pl.BlockSpec((1, s, B_STEP_HEADS * B_QW), lambda i, h: (i, 0, h))
  narrow = pl.BlockSpec((1, s, B_STEP_HEADS * B_VDIM), lambda i, h: (i, 0, h))
  return pl.pallas_call(
      _attn_b_kernel,
      grid=(b, B_HEADS // B_STEP_HEADS),
      in_specs=[wide, wide, narrow, narrow],
      out_specs=narrow,
      out_shape=jax.ShapeDtypeStruct((b, s, B_WIDTH), BF16),
      compiler_params=pltpu.CompilerParams(dimension_semantics=("parallel", "parallel"),
                                           vmem_limit_bytes=VMEM_LIMIT),
      name="attn_b",
  )(q, k, v, gate)


def _out_ple_kernel(ma_ref, mb_ref, h_ref, p_ref, woa_ref, wob_ref, gple_ref, wpg_ref, bpg_ref,
                    wpe_ref, gfin_ref, o_ref, *, final_norm):
  n_sub = 1
  sub = h_ref.shape[0] // n_sub

  def stage1(j):
    rs = slice(sub * j, sub * (j + 1))
    h1 = h_ref[rs, :] + (_dot(ma_ref[rs, :], woa_ref[...]) + _dot(mb_ref[rs, :], wob_ref[...]))
    pe = _dot(p_ref[rs, :].astype(BF16), wpe_ref[...])
    return h1, pe

  def stage2(j, h1, pe):
    rs = slice(sub * j, sub * (j + 1))
    xn = _rms(h1, gple_ref[...]).astype(BF16)
    z = _dot(xn, wpg_ref[...]) + bpg_ref[...]
    h2 = h1 + pe * (1.0 / (1.0 + jnp.exp(-z)))
    o_ref[rs, :] = _rms(h2, gfin_ref[...]) if final_norm else h2

  pending = stage1(0)
  for j in range(n_sub):
    nxt = stage1(j + 1) if j + 1 < n_sub else None
    stage2(j, *pending)
    pending = nxt


def _out_ple(ma, mb, h, p, w, layer, final_norm):
  n = h.shape[0]
  t = OUT_TOKEN_TILE
  tok = lambda width: pl.BlockSpec((t, width), lambda i: (i, 0))
  w_out_half = lambda j: pl.BlockSpec((None, A_WIDTH, D_MODEL), lambda i: (layer, j, 0))
  rest = [w["gple"], w["wpg"], w["bpg"], w["wpe"]]
  return pl.pallas_call(
      functools.partial(_out_ple_kernel, final_norm=final_norm),
      grid=(n // t,),
      in_specs=([tok(A_WIDTH), tok(B_WIDTH), tok(D_MODEL),
                 pl.BlockSpec((None, t, D_PLE), lambda i: (layer, i, 0)),
                 w_out_half(0), w_out_half(1)] + [_layer_spec(a, layer) for a in rest]
                + [pl.BlockSpec(w["gfin"].shape, lambda i: (0, 0))]),
      out_specs=tok(D_MODEL),
      out_shape=jax.ShapeDtypeStruct((n, D_MODEL), F32),
      compiler_params=pltpu.CompilerParams(dimension_semantics=("parallel",),
                                           vmem_limit_bytes=VMEM_LIMIT),
      name="out_ple",
  )(ma, mb, h, p, w["w_out"], w["w_out"], *rest, w["gfin"])


def _prep_params(norm_mix, w_in, g_q, w_uq, g_kv, w_ukv, w_out, norm_ple, w_pe, w_pg, b_pg,
                 norm_final):
  depth = w_in.shape[0]
  w_t = jnp.swapaxes(w_in, 1, 2).astype(BF16)
  mla0 = 4 * A_WIDTH
  gb0 = mla0 + Q_LORA + KV_LORA + B_ROPE
  wmla = jnp.pad(w_t[:, mla0:gb0], ((0, 0), (0, LANES - B_ROPE), (0, 0)))
  uq =w_uq.reshape(depth, Q_LORA, B_HEADS, B_NOPE + B_ROPE)
  zpad = jnp.zeros((depth, Q_LORA, B_HEADS, LANES - B_ROPE), F32)
  wuq = jnp.concatenate([uq, zpad], axis=-1)
  return {
      "gmix": norm_mix[:, None, :],
      "w_t": w_t, "wgb": w_t[:, gb0:], "wmla": wmla,
      "gq": g_q[:, None, :], "gkv": g_kv[:, None, :],
      "wuq": wuq.reshape(depth, Q_LORA, B_HEADS * B_QW).astype(BF16),
      "wukv": w_ukv.astype(BF16),
      "w_out": w_out.astype(BF16),
      "gple": norm_ple[:, None, :], "wpg": w_pg.astype(BF16), "bpg": b_pg[:, None, :],
      "wpe": w_pe.astype(BF16),
      "gfin": norm_final[None, :],
  }


def kernel(x, p, positions, norm_mix, w_in, rel_bias, g_q, w_uq, g_kv, w_ukv, w_out,
           norm_ple, w_pe, w_pg, b_pg, norm_final):
  b, s, d = x.shape
  depth = w_in.shape[0]
  n = b * s
  w = _prep_params(norm_mix, w_in, g_q, w_uq, g_kv, w_ukv, w_out, norm_ple, w_pe, w_pg, b_pg,
                   norm_final)
  p = p.reshape(depth, n, D_PLE)
  h = x.reshape(n, d)
  seq = lambda a: a.reshape(b, s, a.shape[-1])
  tables = None
  tab = _bias_tables(rel_bias)
  for i in range(depth):
    if tables is None:
      *outs, cos, sin = _in_proj(h, w, i, positions=positions)
      tables = (cos, sin)
    else:
      outs = _in_proj(h, w, i, tables=tables)
    qa, ka, va, ga, gb, qb, kb, vb = outs
    ma = _attn_a(seq(qa), seq(ka), seq(va), seq(ga), tab, i)
    mb = _attn_b(seq(qb), seq(kb), seq(vb), seq(gb))
    h = _out_ple(ma.reshape(n, A_WIDTH), mb.reshape(n, B_WIDTH), h, p, w, i,
                 final_norm=(i == depth - 1))
  return h.reshape(b, s, d)
```

```python
import functools
import math

import jax
import jax.numpy as jnp
from jax import lax
from jax.experimental import pallas as pl
from jax.experimental.pallas import tpu as pltpu

D_MODEL = 1024
CHUNK = 64
D_PLE = 256
EPS = 1e-6
A_HEADS = 8
A_HEAD_DIM = 64
A_WIDTH = A_HEADS * A_HEAD_DIM
A_LEFT_CHUNKS = 8
REL_MAX = 128
B_HEADS = 4
B_NOPE = 128
B_ROPE = 64
B_VDIM = 128
B_WIDTH = B_HEADS * B_VDIM
Q_LORA = 256
KV_LORA = 128
ROPE_THETA = 10000.0

LANES = 128
TOKEN_TILE = 1024
OUT_TOKEN_TILE = 1024
Q_BLOCK = 256
A_STEP_PAIRS = 4
B_STEP_HEADS = 4
A_LEFT = A_LEFT_CHUNKS * CHUNK
A_WINDOW = A_LEFT + Q_BLOCK
B_QW = B_NOPE + LANES
LOG2E = math.log2(math.e)
B_QSCALE = (B_NOPE + B_ROPE) ** -0.5 * LOG2E
VMEM_LIMIT = 56 * 1024 * 1024

BF16 = jnp.bfloat16
F32 = jnp.float32


def _dot(a, b):
  return jnp.dot(a, b, preferred_element_type=F32)


def _dot_nt(a, b):
  return lax.dot_general(a, b, (((1,), (1,)), ((), ())), preferred_element_type=F32)


def _rms(x, g):
  y = x * lax.rsqrt(jnp.mean(x * x, axis=-1, keepdims=True) + EPS)
  return y * g


def _silu(g):
  return g * (1.0 / (1.0 + jnp.exp(-g)))


def _inv_freq_column():
  inv_freq = ROPE_THETA ** (-jnp.arange(0, B_ROPE, 2, dtype=F32) / B_ROPE)
  return inv_freq[:, None]


def _rope_tile(pos_row, inv_freq_col):
  t = pos_row.shape[1]
  ang = pos_row.astype(F32) * inv_freq_col
  c, s = jnp.cos(ang), jnp.sin(ang)
  zeros = jnp.zeros((LANES - B_ROPE, t), F32)
  return jnp.concatenate([c, c, zeros], axis=0).T, jnp.concatenate([-s, s, zeros], axis=0).T


G_LEN = 1024


def _bias_table_kernel(g_ref, tab_ref):
  qi = lax.broadcasted_iota(jnp.int32, (Q_BLOCK, A_WINDOW), 0)
  kk = lax.broadcasted_iota(jnp.int32, (Q_BLOCK, A_WINDOW), 1)
  qc = (qi + A_LEFT) // CHUNK
  kc = kk // CHUNK
  valid = (kc <= qc) & (kc >= qc - A_LEFT_CHUNKS)
  for h in range(A_HEADS):
    g = jnp.broadcast_to(g_ref[0, h], (Q_BLOCK, G_LEN))
    t = pltpu.roll(g, G_LEN - Q_BLOCK, 1, stride=1, stride_axis=0)[:, :A_WINDOW]
    tab_ref[0, h] = jnp.where(valid, (t - g[:, :1]) * LOG2E, -jnp.inf)


def _bias_tables(rel_bias):
  depth = rel_bias.shape[0]
  flat = A_LEFT + Q_BLOCK - REL_MAX
  rev = rel_bias[..., ::-1]
  g = jnp.concatenate([
      jnp.broadcast_to(rel_bias[..., -1:], (depth, A_HEADS, flat)),
      rev[..., : 2 * REL_MAX],
      jnp.broadcast_to(rel_bias[..., :1], (depth, A_HEADS, G_LEN - flat - 2 * REL_MAX)),
  ], axis=-1)[:, :, None, :]
  return pl.pallas_call(
      _bias_table_kernel,
      grid=(depth,),
      in_specs=[pl.BlockSpec((1, A_HEADS, 1, G_LEN), lambda l: (l, 0, 0, 0))],
      out_specs=pl.BlockSpec((1, A_HEADS, Q_BLOCK, A_WINDOW), lambda l: (l, 0, 0, 0)),
      out_shape=jax.ShapeDtypeStruct((depth, A_HEADS, Q_BLOCK, A_WINDOW), F32),
      compiler_params=pltpu.CompilerParams(dimension_semantics=("parallel",)),
      name="bias_table",
  )(g)


def _in_proj_kernel(h_ref, gmix_ref, wq_ref, wk_ref, wv_ref, wga_ref, wgb_ref, wmla_ref,
                    gq_ref, gkv_ref, wuq_ref, wukv_ref, *refs, make_tables):
  if make_tables:
    (pos_ref, invf_ref, q_out, k_out, v_out, ga_out, gb_out, qb_out, kn_out, kr_out, vb_out,
     cos_out, sin_out) = refs
  else:
    cos_ref, sin_ref, q_out, k_out, v_out, ga_out, gb_out, qb_out, kn_out, kr_out, vb_out = refs
  xn = _rms(h_ref[...], gmix_ref[...]).astype(BF16)
  zm = _dot_nt(xn, wmla_ref[...])
  cqn = _rms(zm[:, :Q_LORA], gq_ref[...]).astype(BF16)
  ckvn = _rms(zm[:, Q_LORA:Q_LORA + KV_LORA], gkv_ref[...]).astype(BF16)
  qall = _dot(cqn, wuq_ref[...])
  kv = _dot(ckvn, wukv_ref[...])

  q_out[...] = (_dot_nt(xn, wq_ref[...]) * (A_HEAD_DIM ** -0.5 * LOG2E)).astype(BF16)
  k_out[...] = _dot_nt(xn, wk_ref[...]).astype(BF16)
  v_out[...] = _dot_nt(xn, wv_ref[...]).astype(BF16)
  ga_out[...] = _dot_nt(xn, wga_ref[...]).astype(BF16)
  gb_out[...] = _dot_nt(xn, wgb_ref[...]).astype(BF16)

  if make_tables:
    cos, sin = _rope_tile(pos_ref[0], invf_ref[...])
    cos_out[...] = cos
    sin_out[...] = sin
  else:
    cos = cos_ref[...]
    sin = sin_ref[...]
  first_half = lax.broadcasted_iota(jnp.int32, (h_ref.shape[0], LANES), 1) < B_ROPE // 2

  def rope(g):
    swapped = jnp.where(first_half, pltpu.roll(g, LANES - B_ROPE // 2, 1), pltpu.roll(g, B_ROPE // 2, 1))
    return g * cos + swapped * sin

  kr_out[...] = rope(zm[:, Q_LORA + KV_LORA:]).astype(BF16)
  for h in range(B_HEADS):
    qb = B_QW * h
    qb_out[:, qb:qb + B_NOPE] = (qall[:, qb:qb + B_NOPE] * B_QSCALE).astype(BF16)
    qb_out[:, qb + B_NOPE:qb + B_QW] = (rope(qall[:, qb + B_NOPE:qb + B_QW]) * B_QSCALE).astype(BF16)
    kn_out[:, B_NOPE * h:B_NOPE * (h + 1)] = kv[:, 2 * LANES * h:2 * LANES * h + B_NOPE].astype(BF16)
    vb_out[:, B_VDIM * h:B_VDIM * (h + 1)] = kv[:, 2 * LANES * h + B_NOPE:2 * LANES * (h + 1)].astype(BF16)


def _layer_spec(a, layer):
  return pl.BlockSpec((None,) + a.shape[1:], lambda i: (layer,) + (0,) * (a.ndim - 1),
                      pipeline_mode=pl.Buffered(1))


def _in_proj(h, w, layer, positions=None, tables=None):
  n = h.shape[0]
  t = TOKEN_TILE
  tok = lambda width: pl.BlockSpec((t, width), lambda i: (i, 0))
  w_in_block = lambda j: pl.BlockSpec((None, A_WIDTH, D_MODEL), lambda i: (layer, j, 0),
                                      pipeline_mode=pl.Buffered(1))
  small = [w["gmix"], w["wgb"], w["wmla"], w["gq"], w["gkv"], w["wuq"], w["wukv"]]
  in_specs = ([tok(D_MODEL), _layer_spec(small[0], layer)] + [w_in_block(j) for j in range(4)]
              + [_layer_spec(a, layer) for a in small[1:]])
  operands = [h, small[0]] + [w["w_t"]] * 4 + small[1:]
  out_widths = [(A_WIDTH, BF16), (A_WIDTH, BF16), (A_WIDTH, BF16), (A_WIDTH, BF16), (B_WIDTH, BF16),
                (B_HEADS * B_QW, BF16), (B_HEADS * B_NOPE, BF16), (LANES, BF16), (B_WIDTH, BF16)]
  make_tables = tables is None
  if make_tables:
    invf = _inv_freq_column()
    operands += [positions.reshape(n // t, 1, t), invf]
    in_specs += [pl.BlockSpec((1, 1, t), lambda i: (i, 0, 0)), pl.BlockSpec(invf.shape, lambda i: (0, 0))]
    out_widths += [(LANES, F32), (LANES, F32)]
  else:
    operands += list(tables)
    in_specs += [tok(LANES), tok(LANES)]
  return pl.pallas_call(
      functools.partial(_in_proj_kernel, make_tables=make_tables),
      grid=(n // t,),
      in_specs=in_specs,
      out_specs=[tok(wd) for wd, _ in out_widths],
      out_shape=[jax.ShapeDtypeStruct((n, wd), dt) for wd, dt in out_widths],
      compiler_params=pltpu.CompilerParams(dimension_semantics=("parallel",),
                                           vmem_limit_bytes=VMEM_LIMIT),
      name="in_proj",
  )(*operands)


def _attn_a_kernel(q_ref, k_ref, v_ref, gate_ref, tab_ref, o_ref):
  seq_len = q_ref.shape[1]
  lane = lax.broadcasted_iota(jnp.int32, (1, LANES), 1)
  head_rows = ((lane < A_HEAD_DIM).astype(BF16), (lane >= A_HEAD_DIM).astype(BF16))
  lo_mask = lax.broadcasted_iota(jnp.int32, (Q_BLOCK, LANES), 1) < A_HEAD_DIM
  ones = jnp.ones((A_WINDOW, LANES), BF16)

  def window(qb):
    k0 = max(0, Q_BLOCK * qb - A_LEFT)
    return k0, Q_BLOCK * (qb + 1) - k0

  half = Q_BLOCK // 2
  table_zero = ((half, A_LEFT - REL_MAX), (2 * half, A_LEFT - REL_MAX + half))

  def panels(qb):
    _, nk = window(qb)
    first_live = half if nk == A_WINDOW else 0
    return (slice(0, half), slice(0, nk - half)), (slice(half, Q_BLOCK), slice(first_live, nk))

  def scores(qb, pr, hh):
    k0, nk = window(qb)
    ls = slice(LANES * pr, LANES * (pr + 1))
    qp = q_ref[0, Q_BLOCK * qb:Q_BLOCK * (qb + 1), ls]
    s = _dot_nt(qp * head_rows[hh], k_ref[0, k0:k0 + nk, ls])
    col0 = A_WINDOW - nk
    out = []
    for (rs, cs), zero in zip(panels(qb), table_zero):
      cuts = sorted({col0 + cs.start, col0 + cs.stop,
                     *(min(max(z, col0 + cs.start), col0 + cs.stop) for z in zero)})
      pieces = []
      for c0, c1 in zip(cuts[:-1], cuts[1:]):
        piece = s[rs, c0 - col0:c1 - col0]
        if not zero[0] <= c0 < zero[1]:
          piece = piece + tab_ref[2 * pr + hh, rs, c0:c1]
        pieces.append(piece)
      out.append(jnp.concatenate(pieces, axis=1))
    return out

  def weighted(qb, pr, s_panels):
    k0, nk = window(qb)
    vp = jnp.concatenate([v_ref[0, k0:k0 + nk, LANES * pr:LANES * (pr + 1)], ones[:nk]], axis=1)
    rows = []
    for s, (_, cs) in zip(s_panels, panels(qb)):
      e = jnp.exp2(s - jnp.max(s, axis=-1, keepdims=True)).astype(BF16)
      dead = [jnp.zeros((half, w), BF16) for w in (cs.start, nk - cs.stop)]
      rows.append(jnp.concatenate([x for x in (dead[0], e, dead[1]) if x.shape[1]], axis=1))
    pv = _dot(jnp.concatenate(rows, axis=0), vp)
    return pv[:, :LANES] / pv[:, LANES:]

  items = [(qb, pr, hh) for qb in reversed(range(seq_len // Q_BLOCK))
           for pr in range(A_STEP_PAIRS) for hh in range(2)]
  depth = 2
  pending = [scores(*it) for it in items[:depth]]
  outs = []
  for i, (qb, pr, hh) in enumerate(items):
    if i + depth < len(items):
      pending.append(scores(*items[i + depth]))
    outs.append(weighted(qb, pr, pending.pop(0)))
    if hh == 1:
      rows = slice(Q_BLOCK * qb, Q_BLOCK * (qb + 1))
      ls = slice(LANES * pr, LANES * (pr + 1))
      o = jnp.where(lo_mask, outs[0], outs[1])
      o_ref[0, rows, ls] = (o * _silu(gate_ref[0, rows, ls].astype(F32))).astype(BF16)
      outs = []


def _attn_a(q, k, v, gate, tab, layer):
  b, s, _ = q.shape
  pair = pl.BlockSpec((1, s, A_STEP_PAIRS * LANES), lambda p, i: (i, 0, p))
  return pl.pallas_call(
      _attn_a_kernel,
      grid=(A_HEADS // (2 * A_STEP_PAIRS), b),
      in_specs=[pair, pair, pair, pair,
                pl.BlockSpec((None, 2 * A_STEP_PAIRS, Q_BLOCK, A_WINDOW),
                             lambda p, i: (layer, p, 0, 0))],
      out_specs=pair,
      out_shape=jax.ShapeDtypeStruct((b, s, A_WIDTH), BF16),
      compiler_params=pltpu.CompilerParams(dimension_semantics=("parallel", "parallel"),
                                           vmem_limit_bytes=VMEM_LIMIT),
      name="attn_a",
  )(q, k, v, gate, tab)


def _attn_b_kernel(q_ref, kn_ref, kr_ref, v_ref, gate_ref, o_ref):
  seq_len = q_ref.shape[1]
  ones = jnp.ones((seq_len, B_QW - B_VDIM), BF16)
  qi = lax.broadcasted_iota(jnp.int32, (Q_BLOCK, Q_BLOCK), 0)
  kj = lax.broadcasted_iota(jnp.int32, (Q_BLOCK, Q_BLOCK), 1)
  diag_ok = (kj // CHUNK) <= (qi // CHUNK)

  def scores(qb, h):
    rows = slice(Q_BLOCK * qb, Q_BLOCK * (qb + 1))
    wide = slice(B_QW * h, B_QW * (h + 1))
    qh = q_ref[0, rows, wide]
    nk = Q_BLOCK * (qb + 1)
    kh = jnp.concatenate([kn_ref[0, :nk, B_NOPE * h:B_NOPE * (h + 1)], kr_ref[0, :nk, :]], axis=1)
    s = _dot_nt(qh, kh)
    s_d = jnp.where(diag_ok, s[:, Q_BLOCK * qb:], -jnp.inf)
    return s_d, (s[:, :Q_BLOCK * qb] if qb else None)

  def finish(qb, h, s_d, s_p):
    rows = slice(Q_BLOCK * qb, Q_BLOCK * (qb + 1))
    wide = slice(B_QW * h, B_QW * (h + 1))
    narrow = slice(B_VDIM * h, B_VDIM * (h + 1))
    m = jnp.max(s_d, axis=-1, keepdims=True)
    if qb:
      m = jnp.maximum(m, jnp.max(s_p, axis=-1, keepdims=True))
    e = jnp.exp2(s_d - m).astype(BF16)
    if qb:
      e = jnp.concatenate([jnp.exp2(s_p - m).astype(BF16), e], axis=1)
    nk = Q_BLOCK * (qb + 1)
    pv = _dot(e, jnp.concatenate([v_ref[0, :nk, narrow], ones[:nk]], axis=1))
    o = pv[:, :B_VDIM] / pv[:, B_VDIM:]
    o_ref[0, rows, narrow] = (o * _silu(gate_ref[0, rows, narrow].astype(F32))).astype(BF16)

  items = [(qb, h) for qb in reversed(range(seq_len // Q_BLOCK)) for h in range(B_STEP_HEADS)]
  depth = 2
  pending = [scores(*it) for it in items[:depth]]
  for i, it in enumerate(items):
    if i + depth < len(items):
      pending.append(scores(*items[i + depth]))
    finish(*it, *pending.pop(0))


def _attn_b(q, kn, kr, v, gate):
  b, s, _ = q.shape
  wide = pl.BlockSpec((1, s, B_STEP_HEADS * B_QW), lambda i, h: (i, 0, h))
  narrow = pl.BlockSpec((1, s, B_STEP_HEADS * B_VDIM), lambda i, h: (i, 0, h))
  return pl.pallas_call(
      _attn_b_kernel,
      grid=(b, B_HEADS // B_STEP_HEADS),
      in_specs=[wide, narrow, pl.BlockSpec((1, s, LANES), lambda i, h: (i, 0, 0)), narrow, narrow],
      out_specs=narrow,
      out_shape=jax.ShapeDtypeStruct((b, s, B_WIDTH), BF16),
      compiler_params=pltpu.CompilerParams(dimension_semantics=("parallel", "parallel"),
                                           vmem_limit_bytes=VMEM_LIMIT),
      name="attn_b",
  )(q, kn, kr, v, gate)


def _out_ple_kernel(ma_ref, mb_ref, h_ref, p_ref, woa_ref, wob_ref, gple_ref, wpg_ref, bpg_ref,
                    wpe_ref, gfin_ref, o_ref, *, final_norm):
  n_sub = 1
  sub = h_ref.shape[0] // n_sub

  def stage1(j):
    rs = slice(sub * j, sub * (j + 1))
    h1 = h_ref[rs, :] + (_dot(ma_ref[rs, :], woa_ref[...]) + _dot(mb_ref[rs, :], wob_ref[...]))
    pe = _dot(p_ref[rs, :].astype(BF16), wpe_ref[...])
    return h1, pe

  def stage2(j, h1, pe):
    rs = slice(sub * j, sub * (j + 1))
    xn = _rms(h1, gple_ref[...]).astype(BF16)
    z = _dot(xn, wpg_ref[...]) + bpg_ref[...]
    h2 = h1 + pe * (1.0 / (1.0 + jnp.exp(-z)))
    o_ref[rs, :] = _rms(h2, gfin_ref[...]) if final_norm else h2

  pending = stage1(0)
  for j in range(n_sub):
    nxt = stage1(j + 1) if j + 1 < n_sub else None
    stage2(j, *pending)
    pending = nxt


def _out_ple(ma, mb, h, p, w, layer, final_norm):
  n = h.shape[0]
  t = OUT_TOKEN_TILE
  tok = lambda width: pl.BlockSpec((t, width), lambda i: (i, 0))
  w_out_half = lambda j: pl.BlockSpec((None, A_WIDTH, D_MODEL), lambda i: (layer, j, 0))
  rest = [w["gple"], w["wpg"], w["bpg"], w["wpe"]]
  return pl.pallas_call(
      functools.partial(_out_ple_kernel, final_norm=final_norm),
      grid=(n // t,),
      in_specs=([tok(A_WIDTH), tok(B_WIDTH), tok(D_MODEL),
                 pl.BlockSpec((None, t, D_PLE), lambda i: (layer, i, 0)),
                 w_out_half(0), w_out_half(1)] + [_layer_spec(a, layer) for a in rest]
                + [pl.BlockSpec(w["gfin"].shape, lambda i: (0, 0))]),
      out_specs=tok(D_MODEL),
      out_shape=jax.ShapeDtypeStruct((n, D_MODEL), F32),
      compiler_params=pltpu.CompilerParams(dimension_semantics=("parallel",),
                                           vmem_limit_bytes=VMEM_LIMIT),
      name="out_ple",
  )(ma, mb, h, p, w["w_out"], w["w_out"], *rest, w["gfin"])


def _prep_params(norm_mix, w_in, g_q, w_uq, g_kv, w_ukv, w_out, norm_ple, w_pe, w_pg, b_pg,
                 norm_final):
  depth = w_in.shape[0]
  w_t = jnp.swapaxes(w_in, 1, 2).astype(BF16)
  mla0 = 4 * A_WIDTH
  gb0 = mla0 + Q_LORA + KV_LORA + B_ROPE
  wmla = jnp.pad(w_t[:, mla0:gb0], ((0, 0), (0, LANES - B_ROPE), (0, 0)))
  uq =w_uq.reshape(depth, Q_LORA, B_HEADS, B_NOPE + B_ROPE)
  zpad = jnp.zeros((depth, Q_LORA, B_HEADS, LANES - B_ROPE), F32)
  wuq = jnp.concatenate([uq, zpad], axis=-1)
  return {
      "gmix": norm_mix[:, None, :],
      "w_t": w_t, "wgb": w_t[:, gb0:], "wmla": wmla,
      "gq": g_q[:, None, :], "gkv": g_kv[:, None, :],
      "wuq": wuq.reshape(depth, Q_LORA, B_HEADS * B_QW).astype(BF16),
      "wukv": w_ukv.astype(BF16),
      "w_out": w_out.astype(BF16),
      "gple": norm_ple[:, None, :], "wpg": w_pg.astype(BF16), "bpg": b_pg[:, None, :],
      "wpe": w_pe.astype(BF16),
      "gfin": norm_final[None, :],
  }


def kernel(x, p, positions, norm_mix, w_in, rel_bias, g_q, w_uq, g_kv, w_ukv, w_out,
           norm_ple, w_pe, w_pg, b_pg, norm_final):
  b, s, d = x.shape
  depth = w_in.shape[0]
  n = b * s
  w = _prep_params(norm_mix, w_in, g_q, w_uq, g_kv, w_ukv, w_out, norm_ple, w_pe, w_pg, b_pg,
                   norm_final)
  p = p.reshape(depth, n, D_PLE)
  h = x.reshape(n, d)
  seq = lambda a: a.reshape(b, s, a.shape[-1])
  tables = None
  tab = _bias_tables(rel_bias)
  for i in range(depth):
    if tables is None:
      *outs, cos, sin = _in_proj(h, w, i, positions=positions)
      tables = (cos, sin)
    else:
      outs = _in_proj(h, w, i, tables=tables)
    qa, ka, va, ga, gb, qb, kn, kr, vb = outs
    ma = _attn_a(seq(qa), seq(ka), seq(va), seq(ga), tab, i)
    mb = _attn_b(seq(qb), seq(kn), seq(kr), seq(vb), seq(gb))
    h = _out_ple(ma.reshape(n, A_WIDTH), mb.reshape(n, B_WIDTH), h, p, w, i,
                 final_norm=(i == depth - 1))
  return h.reshape(b, s, d)
```

```python
import functools
import math

import jax
import jax.numpy as jnp
from jax import lax
from jax.experimental import pallas as pl
from jax.experimental.pallas import tpu as pltpu

D_MODEL = 1024
CHUNK = 64
D_PLE = 256
EPS = 1e-6
A_HEADS = 8
A_HEAD_DIM = 64
A_WIDTH = A_HEADS * A_HEAD_DIM
A_LEFT_CHUNKS = 8
REL_MAX = 128
B_HEADS = 4
B_NOPE = 128
B_ROPE = 64
B_VDIM = 128
B_WIDTH = B_HEADS * B_VDIM
Q_LORA = 256
KV_LORA = 128
ROPE_THETA = 10000.0

LANES = 128
TOKEN_TILE = 1024
OUT_TOKEN_TILE = 1024
Q_BLOCK = 256
A_STEP_PAIRS = 4
B_STEP_HEADS = 4
A_LEFT = A_LEFT_CHUNKS * CHUNK
A_WINDOW = A_LEFT + Q_BLOCK
B_QW = B_NOPE + LANES
LOG2E = math.log2(math.e)
B_QSCALE = (B_NOPE + B_ROPE) ** -0.5 * LOG2E
VMEM_LIMIT = 56 * 1024 * 1024

BF16 = jnp.bfloat16
F32 = jnp.float32


def _dot(a, b):
  return jnp.dot(a, b, preferred_element_type=F32)


def _dot_nt(a, b):
  return lax.dot_general(a, b, (((1,), (1,)), ((), ())), preferred_element_type=F32)


def _rms(x, g):
  y = x * lax.rsqrt(jnp.mean(x * x, axis=-1, keepdims=True) + EPS)
  return y * g


def _silu(g):
  return g * (1.0 / (1.0 + jnp.exp(-g)))


def _inv_freq_column():
  inv_freq = ROPE_THETA ** (-jnp.arange(0, B_ROPE, 2, dtype=F32) / B_ROPE)
  return inv_freq[:, None]


def _rope_tile(pos_row, inv_freq_col):
  t = pos_row.shape[1]
  ang = pos_row.astype(F32) * inv_freq_col
  c, s = jnp.cos(ang), jnp.sin(ang)
  zeros = jnp.zeros((LANES - B_ROPE, t), F32)
  return jnp.concatenate([c, c, zeros], axis=0).T, jnp.concatenate([-s, s, zeros], axis=0).T


G_LEN = 1024


def _bias_table_kernel(g_ref, tab_ref):
  qi = lax.broadcasted_iota(jnp.int32, (Q_BLOCK, A_WINDOW), 0)
  kk = lax.broadcasted_iota(jnp.int32, (Q_BLOCK, A_WINDOW), 1)
  qc = (qi + A_LEFT) // CHUNK
  kc = kk // CHUNK
  valid = (kc <= qc) & (kc >= qc - A_LEFT_CHUNKS)
  for h in range(A_HEADS):
    g = jnp.broadcast_to(g_ref[0, h], (Q_BLOCK, G_LEN))
    t = pltpu.roll(g, G_LEN - Q_BLOCK, 1, stride=1, stride_axis=0)[:, :A_WINDOW]
    tab_ref[0, h] = jnp.where(valid, (t - g[:, :1]) * LOG2E, -jnp.inf)


def _bias_tables(rel_bias):
  depth = rel_bias.shape[0]
  flat = A_LEFT + Q_BLOCK - REL_MAX
  rev = rel_bias[..., ::-1]
  g = jnp.concatenate([
      jnp.broadcast_to(rel_bias[..., -1:], (depth, A_HEADS, flat)),
      rev[..., : 2 * REL_MAX],
      jnp.broadcast_to(rel_bias[..., :1], (depth, A_HEADS, G_LEN - flat - 2 * REL_MAX)),
  ], axis=-1)[:, :, None, :]
  return pl.pallas_call(
      _bias_table_kernel,
      grid=(depth,),
      in_specs=[pl.BlockSpec((1, A_HEADS, 1, G_LEN), lambda l: (l, 0, 0, 0))],
      out_specs=pl.BlockSpec((1, A_HEADS, Q_BLOCK, A_WINDOW), lambda l: (l, 0, 0, 0)),
      out_shape=jax.ShapeDtypeStruct((depth, A_HEADS, Q_BLOCK, A_WINDOW), F32),
      compiler_params=pltpu.CompilerParams(dimension_semantics=("parallel",)),
      name="bias_table",
  )(g)


def _in_proj_kernel(h_ref, gmix_ref, wq_ref, wk_ref, wv_ref, wga_ref, wgb_ref, wmla_ref,
                    gq_ref, gkv_ref, wuq_ref, wukv_ref, *refs, make_tables):
  if make_tables:
    (pos_ref, invf_ref, q_out, k_out, v_out, ga_out, gb_out, qb_out, kn_out, kr_out, vb_out,
     cos_out, sin_out) = refs
  else:
    cos_ref, sin_ref, q_out, k_out, v_out, ga_out, gb_out, qb_out, kn_out, kr_out, vb_out = refs
  xn = _rms(h_ref[...], gmix_ref[...]).astype(BF16)
  zm = _dot_nt(xn, wmla_ref[...])
  cqn = _rms(zm[:, :Q_LORA], gq_ref[...]).astype(BF16)
  ckvn = _rms(zm[:, Q_LORA:Q_LORA + KV_LORA], gkv_ref[...]).astype(BF16)
  qall = _dot(cqn, wuq_ref[...])
  kv = _dot(ckvn, wukv_ref[...])

  q_out[...] = (_dot_nt(xn, wq_ref[...]) * (A_HEAD_DIM ** -0.5 * LOG2E)).astype(BF16)
  k_out[...] = _dot_nt(xn, wk_ref[...]).astype(BF16)
  v_out[...] = _dot_nt(xn, wv_ref[...]).astype(BF16)
  ga_out[...] = _dot_nt(xn, wga_ref[...]).astype(BF16)
  gb_out[...] = _dot_nt(xn, wgb_ref[...]).astype(BF16)

  if make_tables:
    cos, sin = _rope_tile(pos_ref[0], invf_ref[...])
    cos_out[...] = cos
    sin_out[...] = sin
  else:
    cos = cos_ref[...]
    sin = sin_ref[...]
  first_half = lax.broadcasted_iota(jnp.int32, (h_ref.shape[0], LANES), 1) < B_ROPE // 2

  def rope(g):
    swapped = jnp.where(first_half, pltpu.roll(g, LANES - B_ROPE // 2, 1), pltpu.roll(g, B_ROPE // 2, 1))
    return g * cos + swapped * sin

  kr_out[...] = rope(zm[:, Q_LORA + KV_LORA:]).astype(BF16)
  for h in range(B_HEADS):
    qb = B_QW * h
    qb_out[:, qb:qb + B_NOPE] = (qall[:, qb:qb + B_NOPE] * B_QSCALE).astype(BF16)
    qb_out[:, qb + B_NOPE:qb + B_QW] = (rope(qall[:, qb + B_NOPE:qb + B_QW]) * B_QSCALE).astype(BF16)
    kn_out[:, B_NOPE * h:B_NOPE * (h + 1)] = kv[:, 2 * LANES * h:2 * LANES * h + B_NOPE].astype(BF16)
    vb_out[:, B_VDIM * h:B_VDIM * (h + 1)] = kv[:, 2 * LANES * h + B_NOPE:2 * LANES * (h + 1)].astype(BF16)


def _layer_spec(a, layer):
  return pl.BlockSpec((None,) + a.shape[1:], lambda i: (layer,) + (0,) * (a.ndim - 1),
                      pipeline_mode=pl.Buffered(1))


def _in_proj(h, w, layer, positions=None, tables=None):
  n = h.shape[0]
  t = TOKEN_TILE
  tok = lambda width: pl.BlockSpec((t, width), lambda i: (i, 0))
  w_in_block = lambda j: pl.BlockSpec((None, A_WIDTH, D_MODEL), lambda i: (layer, j, 0),
                                      pipeline_mode=pl.Buffered(1))
  small = [w["gmix"], w["wgb"], w["wmla"], w["gq"], w["gkv"], w["wuq"], w["wukv"]]
  in_specs = ([tok(D_MODEL), _layer_spec(small[0], layer)] + [w_in_block(j) for j in range(4)]
              + [_layer_spec(a, layer) for a in small[1:]])
  operands = [h, small[0]] + [w["w_t"]] * 4 + small[1:]
  out_widths = [(A_WIDTH, BF16), (A_WIDTH, BF16), (A_WIDTH, BF16), (A_WIDTH, BF16), (B_WIDTH, BF16),
                (B_HEADS * B_QW, BF16), (B_HEADS * B_NOPE, BF16), (LANES, BF16), (B_WIDTH, BF16)]
  make_tables = tables is None
  if make_tables:
    invf = _inv_freq_column()
    operands += [positions.reshape(n // t, 1, t), invf]
    in_specs += [pl.BlockSpec((1, 1, t), lambda i: (i, 0, 0)), pl.BlockSpec(invf.shape, lambda i: (0, 0))]
    out_widths += [(LANES, F32), (LANES, F32)]
  else:
    operands += list(tables)
    in_specs += [tok(LANES), tok(LANES)]
  return pl.pallas_call(
      functools.partial(_in_proj_kernel, make_tables=make_tables),
      grid=(n // t,),
      in_specs=in_specs,
      out_specs=[tok(wd) for wd, _ in out_widths],
      out_shape=[jax.ShapeDtypeStruct((n, wd), dt) for wd, dt in out_widths],
      compiler_params=pltpu.CompilerParams(dimension_semantics=("parallel",),
                                           vmem_limit_bytes=VMEM_LIMIT),
      name="in_proj",
  )(*operands)


def _attn_a_kernel(q_ref, k_ref, v_ref, gate_ref, tab_ref, o_ref):
  seq_len = q_ref.shape[1]
  lane = lax.broadcasted_iota(jnp.int32, (1, LANES), 1)
  head_rows = ((lane < A_HEAD_DIM).astype(BF16), (lane >= A_HEAD_DIM).astype(BF16))
  lo_mask = lax.broadcasted_iota(jnp.int32, (Q_BLOCK, LANES), 1) < A_HEAD_DIM
  ones = jnp.ones((A_WINDOW, LANES), BF16)

  def window(qb):
    k0 = max(0, Q_BLOCK * qb - A_LEFT)
    return k0, Q_BLOCK * (qb + 1) - k0

  half = Q_BLOCK // 2
  table_zero = ((half, A_LEFT - REL_MAX), (2 * half, A_LEFT - REL_MAX + half))

  def panels(qb):
    _, nk = window(qb)
    first_live = half if nk == A_WINDOW else 0
    return (slice(0, half), slice(0, nk - half)), (slice(half, Q_BLOCK), slice(first_live, nk))

  def scores(qb, pr, hh):
    k0, nk = window(qb)
    ls = slice(LANES * pr, LANES * (pr + 1))
    qp = q_ref[0, Q_BLOCK * qb:Q_BLOCK * (qb + 1), ls]
    s = _dot_nt(qp * head_rows[hh], k_ref[0, k0:k0 + nk, ls])
    col0 = A_WINDOW - nk
    out = []
    for (rs, cs), zero in zip(panels(qb), table_zero):
      cuts = sorted({col0 + cs.start, col0 + cs.stop,
                     *(min(max(z, col0 + cs.start), col0 + cs.stop) for z in zero)})
      pieces = []
      for c0, c1 in zip(cuts[:-1], cuts[1:]):
        piece = s[rs, c0 - col0:c1 - col0]
        if not zero[0] <= c0 < zero[1]:
          piece = piece + tab_ref[2 * pr + hh, rs, c0:c1]
        pieces.append(piece)
      out.append(jnp.concatenate(pieces, axis=1))
    return out

  def weighted(qb, pr, s_panels):
    k0, nk = window(qb)
    vp = jnp.concatenate([v_ref[0, k0:k0 + nk, LANES * pr:LANES * (pr + 1)], ones[:nk]], axis=1)
    rows = []
    for s, (_, cs) in zip(s_panels, panels(qb)):
      e = jnp.exp2(s - jnp.max(s, axis=-1, keepdims=True)).astype(BF16)
      dead = [jnp.zeros((half, w), BF16) for w in (cs.start, nk - cs.stop)]
      rows.append(jnp.concatenate([x for x in (dead[0], e, dead[1]) if x.shape[1]], axis=1))
    pv = _dot(jnp.concatenate(rows, axis=0), vp)
    return pv[:, :LANES] / pv[:, LANES:]

  items = [(qb, pr, hh) for qb in reversed(range(seq_len // Q_BLOCK))
           for pr in range(A_STEP_PAIRS) for hh in range(2)]
  depth = 2
  pending = [scores(*it) for it in items[:depth]]
  outs = []
  for i, (qb, pr, hh) in enumerate(items):
    if i + depth < len(items):
      pending.append(scores(*items[i + depth]))
    outs.append(weighted(qb, pr, pending.pop(0)))
    if hh == 1:
      rows = slice(Q_BLOCK * qb, Q_BLOCK * (qb + 1))
      ls = slice(LANES * pr, LANES * (pr + 1))
      o = jnp.where(lo_mask, outs[0], outs[1])
      o_ref[0, rows, ls] = (o * _silu(gate_ref[0, rows, ls].astype(F32))).astype(BF16)
      outs = []


def _attn_a(q, k, v, gate, tab, layer):
  b, s, _ = q.shape
  pair = pl.BlockSpec((1, s, A_STEP_PAIRS * LANES), lambda p, i: (i, 0, p))
  return pl.pallas_call(
      _attn_a_kernel,
      grid=(A_HEADS // (2 * A_STEP_PAIRS), b),
      in_specs=[pair, pair, pair, pair,
                pl.BlockSpec((None, 2 * A_STEP_PAIRS, Q_BLOCK, A_WINDOW),
                             lambda p, i: (layer, p, 0, 0))],
      out_specs=pair,
      out_shape=jax.ShapeDtypeStruct((b, s, A_WIDTH), BF16),
      compiler_params=pltpu.CompilerParams(dimension_semantics=("parallel", "parallel"),
                                           vmem_limit_bytes=VMEM_LIMIT),
      name="attn_a",
  )(q, k, v, gate, tab)


def _attn_b_kernel(q_ref, kn_ref, kr_ref, v_ref, gate_ref, o_ref):
  seq_len = q_ref.shape[1]
  ones = jnp.ones((seq_len, B_QW - B_VDIM), BF16)
  qi = lax.broadcasted_iota(jnp.int32, (Q_BLOCK, Q_BLOCK), 0)
  kj = lax.broadcasted_iota(jnp.int32, (Q_BLOCK, Q_BLOCK), 1)
  diag_ok = (kj // CHUNK) <= (qi // CHUNK)

  def scores(qb, h):
    rows = slice(Q_BLOCK * qb, Q_BLOCK * (qb + 1))
    wide = slice(B_QW * h, B_QW * (h + 1))
    qh = q_ref[0, rows, wide]
    nk = Q_BLOCK * (qb + 1)
    kh = jnp.concatenate([kn_ref[0, :nk, B_NOPE * h:B_NOPE * (h + 1)], kr_ref[0, :nk, :]], axis=1)
    s = _dot_nt(qh, kh)
    s_d = jnp.where(diag_ok, s[:, Q_BLOCK * qb:], -jnp.inf)
    return s_d, (s[:, :Q_BLOCK * qb] if qb else None)

  def finish(qb, h, s_d, s_p):
    rows = slice(Q_BLOCK * qb, Q_BLOCK * (qb + 1))
    narrow = slice(B_VDIM * h, B_VDIM * (h + 1))
    m = jnp.max(s_d, axis=-1, keepdims=True)
    if qb:
      m = jnp.maximum(m, jnp.max(s_p, axis=-1, keepdims=True))
    e = jnp.exp2(s_d - m).astype(BF16)
    if qb:
      e = jnp.concatenate([jnp.exp2(s_p - m).astype(BF16), e], axis=1)
    nk = Q_BLOCK * (qb + 1)
    pv = _dot(e, jnp.concatenate([v_ref[0, :nk, narrow], ones[:nk]], axis=1))
    o = pv[:, :B_VDIM] / pv[:, B_VDIM:]
    o_ref[0, rows, narrow] = (o * _silu(gate_ref[0, rows, narrow].astype(F32))).astype(BF16)

  items = [(qb, h) for qb in reversed(range(seq_len // Q_BLOCK)) for h in range(B_STEP_HEADS)]
  depth = 2
  pending = [scores(*it) for it in items[:depth]]
  for i, it in enumerate(items):
    if i + depth < len(items):
      pending.append(scores(*items[i + depth]))
    finish(*it, *pending.pop(0))


def _attn_b(q, kn, kr, v, gate):
  b, s, _ = q.shape
  wide = pl.BlockSpec((1, s, B_STEP_HEADS * B_QW), lambda i, h: (i, 0, h))
  narrow = pl.BlockSpec((1, s, B_STEP_HEADS * B_VDIM), lambda i, h: (i, 0, h))
  return pl.pallas_call(
      _attn_b_kernel,
      grid=(b, B_HEADS // B_STEP_HEADS),
      in_specs=[wide, narrow, pl.BlockSpec((1, s, LANES), lambda i, h: (i, 0, 0)), narrow, narrow],
      out_specs=narrow,
      out_shape=jax.ShapeDtypeStruct((b, s, B_WIDTH), BF16),
      compiler_params=pltpu.CompilerParams(dimension_semantics=("parallel", "parallel"),
                                           vmem_limit_bytes=VMEM_LIMIT),
      name="attn_b",
  )(q, kn, kr, v, gate)


def _out_ple_kernel(ma_ref, mb_ref, h_ref, p_ref, woa_ref, wob_ref, gple_ref, wpg_ref, bpg_ref,
                    wpe_ref, gfin_ref, o_ref, *, final_norm):
  h1 = h_ref[...] + (_dot(ma_ref[...], woa_ref[...]) + _dot(mb_ref[...], wob_ref[...]))
  pe = _dot(p_ref[...].astype(BF16), wpe_ref[...])
  xn = _rms(h1, gple_ref[...]).astype(BF16)
  z = _dot(xn, wpg_ref[...]) + bpg_ref[...]
  h2 = h1 + pe * (1.0 / (1.0 + jnp.exp(-z)))
  o_ref[...] = _rms(h2, gfin_ref[...]) if final_norm else h2


def _out_ple(ma, mb, h, p, w, layer, final_norm):
  n = h.shape[0]
  t = OUT_TOKEN_TILE
  tok = lambda width: pl.BlockSpec((t, width), lambda i: (i, 0))
  w_out_half = lambda j: pl.BlockSpec((None, A_WIDTH, D_MODEL), lambda i: (layer, j, 0))
  rest = [w["gple"], w["wpg"], w["bpg"], w["wpe"]]
  return pl.pallas_call(
      functools.partial(_out_ple_kernel, final_norm=final_norm),
      grid=(n // t,),
      in_specs=([tok(A_WIDTH), tok(B_WIDTH), tok(D_MODEL),
                 pl.BlockSpec((None, t, D_PLE), lambda i: (layer, i, 0)),
                 w_out_half(0), w_out_half(1)] + [_layer_spec(a, layer) for a in rest]
                + [pl.BlockSpec(w["gfin"].shape, lambda i: (0, 0))]),
      out_specs=tok(D_MODEL),
      out_shape=jax.ShapeDtypeStruct((n, D_MODEL), F32),
      compiler_params=pltpu.CompilerParams(dimension_semantics=("parallel",),
                                           vmem_limit_bytes=VMEM_LIMIT),
      name="out_ple",
  )(ma, mb, h, p, w["w_out"], w["w_out"], *rest, w["gfin"])


def _prep_params(norm_mix, w_in, g_q, w_uq, g_kv, w_ukv, w_out, norm_ple, w_pe, w_pg, b_pg,
                 norm_final):
  depth = w_in.shape[0]
  w_t = jnp.swapaxes(w_in, 1, 2).astype(BF16)
  mla0 = 4 * A_WIDTH
  gb0 = mla0 + Q_LORA + KV_LORA + B_ROPE
  wmla = jnp.pad(w_t[:, mla0:gb0], ((0, 0), (0, LANES - B_ROPE), (0, 0)))
  uq =w_uq.reshape(depth, Q_LORA, B_HEADS, B_NOPE + B_ROPE)
  zpad = jnp.zeros((depth, Q_LORA, B_HEADS, LANES - B_ROPE), F32)
  wuq = jnp.concatenate([uq, zpad], axis=-1)
  return {
      "gmix": norm_mix[:, None, :],
      "w_t": w_t, "wgb": w_t[:, gb0:], "wmla": wmla,
      "gq": g_q[:, None, :], "gkv": g_kv[:, None, :],
      "wuq": wuq.reshape(depth, Q_LORA, B_HEADS * B_QW).astype(BF16),
      "wukv": w_ukv.astype(BF16),
      "w_out": w_out.astype(BF16),
      "gple": norm_ple[:, None, :], "wpg": w_pg.astype(BF16), "bpg": b_pg[:, None, :],
      "wpe": w_pe.astype(BF16),
      "gfin": norm_final[None, :],
  }


def kernel(x, p, positions, norm_mix, w_in, rel_bias, g_q, w_uq, g_kv, w_ukv, w_out,
           norm_ple, w_pe, w_pg, b_pg, norm_final):
  b, s, d = x.shape
  depth = w_in.shape[0]
  n = b * s
  w = _prep_params(norm_mix, w_in, g_q, w_uq, g_kv, w_ukv, w_out, norm_ple, w_pe, w_pg, b_pg,
                   norm_final)
  p = p.reshape(depth, n, D_PLE)
  h = x.reshape(n, d)
  seq = lambda a: a.reshape(b, s, a.shape[-1])
  tables = None
  tab = _bias_tables(rel_bias)
  for i in range(depth):
    if tables is None:
      *outs, cos, sin = _in_proj(h, w, i, positions=positions)
      tables = (cos, sin)
    else:
      outs = _in_proj(h, w, i, tables=tables)
    qa, ka, va, ga, gb, qb, kn, kr, vb = outs
    ma = _attn_a(seq(qa), seq(ka), seq(va), seq(ga), tab, i)
    mb = _attn_b(seq(qb), seq(kn), seq(kr), seq(vb), seq(gb))
    h = _out_ple(ma.reshape(n, A_WIDTH), mb.reshape(n, B_WIDTH), h, p, w, i,
                 final_norm=(i == depth - 1))
  return h.reshape(b, s, d)
```

```python
import functools
import math

import jax
import jax.numpy as jnp
from jax import lax
from jax.experimental import pallas as pl
from jax.experimental.pallas import tpu as pltpu

D_MODEL = 1024
CHUNK = 64
D_PLE = 256
EPS = 1e-6
A_HEADS = 8
A_HEAD_DIM = 64
A_WIDTH = A_HEADS * A_HEAD_DIM
A_LEFT_CHUNKS = 8
REL_MAX = 128
B_HEADS = 4
B_NOPE = 128
B_ROPE = 64
B_VDIM = 128
B_WIDTH = B_HEADS * B_VDIM
Q_LORA = 256
KV_LORA = 128
ROPE_THETA = 10000.0

LANES = 128
TOKEN_TILE = 1024
OUT_TOKEN_TILE = 1024
Q_BLOCK = 256
A_STEP_PAIRS = 4
B_STEP_HEADS = 4
A_LEFT = A_LEFT_CHUNKS * CHUNK
A_WINDOW = A_LEFT + Q_BLOCK
B_QW = B_NOPE + LANES
LOG2E = math.log2(math.e)
B_QSCALE = (B_NOPE + B_ROPE) ** -0.5 * LOG2E
VMEM_LIMIT = 56 * 1024 * 1024
SEAM_TOKEN_TILE = 512
SEAM_VMEM_LIMIT = VMEM_LIMIT

BF16 = jnp.bfloat16
F32 = jnp.float32


def _dot(a, b):
  return jnp.dot(a, b, preferred_element_type=F32)


def _dot_nt(a, b):
  return lax.dot_general(a, b, (((1,), (1,)), ((), ())), preferred_element_type=F32)


def _rms(x, g):
  y = x * lax.rsqrt(jnp.mean(x * x, axis=-1, keepdims=True) + EPS)
  return y * g


def _silu(g):
  return g * (1.0 / (1.0 + jnp.exp(-g)))


def _inv_freq_column():
  inv_freq = ROPE_THETA ** (-jnp.arange(0, B_ROPE, 2, dtype=F32) / B_ROPE)
  return inv_freq[:, None]


def _rope_tile(pos_row, inv_freq_col):
  t = pos_row.shape[1]
  ang = pos_row.astype(F32) * inv_freq_col
  c, s = jnp.cos(ang), jnp.sin(ang)
  zeros = jnp.zeros((LANES - B_ROPE, t), F32)
  return jnp.concatenate([c, c, zeros], axis=0).T, jnp.concatenate([-s, s, zeros], axis=0).T


G_LEN = 1024


def _bias_table_kernel(g_ref, tab_ref):
  qi = lax.broadcasted_iota(jnp.int32, (Q_BLOCK, A_WINDOW), 0)
  kk = lax.broadcasted_iota(jnp.int32, (Q_BLOCK, A_WINDOW), 1)
  qc = (qi + A_LEFT) // CHUNK
  kc = kk // CHUNK
  valid = (kc <= qc) & (kc >= qc - A_LEFT_CHUNKS)
  for h in range(A_HEADS):
    g = jnp.broadcast_to(g_ref[0, h], (Q_BLOCK, G_LEN))
    t = pltpu.roll(g, G_LEN - Q_BLOCK, 1, stride=1, stride_axis=0)[:, :A_WINDOW]
    tab_ref[0, h] = jnp.where(valid, (t - g[:, :1]) * LOG2E, -jnp.inf)


def _bias_tables(rel_bias):
  depth = rel_bias.shape[0]
  flat = A_LEFT + Q_BLOCK - REL_MAX
  rev = rel_bias[..., ::-1]
  g = jnp.concatenate([
      jnp.broadcast_to(rel_bias[..., -1:], (depth, A_HEADS, flat)),
      rev[..., : 2 * REL_MAX],
      jnp.broadcast_to(rel_bias[..., :1], (depth, A_HEADS, G_LEN - flat - 2 * REL_MAX)),
  ], axis=-1)[:, :, None, :]
  return pl.pallas_call(
      _bias_table_kernel,
      grid=(depth,),
      in_specs=[pl.BlockSpec((1, A_HEADS, 1, G_LEN), lambda l: (l, 0, 0, 0))],
      out_specs=pl.BlockSpec((1, A_HEADS, Q_BLOCK, A_WINDOW), lambda l: (l, 0, 0, 0)),
      out_shape=jax.ShapeDtypeStruct((depth, A_HEADS, Q_BLOCK, A_WINDOW), F32),
      compiler_params=pltpu.CompilerParams(dimension_semantics=("parallel",)),
      name="bias_table",
  )(g)


def _in_proj_kernel(h_ref, gmix_ref, wq_ref, wk_ref, wv_ref, wga_ref, wgb_ref, wmla_ref,
                    gq_ref, gkv_ref, wuq_ref, wukv_ref, *refs, make_tables):
  if make_tables:
    (pos_ref, invf_ref, q_out, k_out, v_out, ga_out, gb_out, qb_out, kn_out, kr_out, vb_out,
     cos_out, sin_out) = refs
  else:
    cos_ref, sin_ref, q_out, k_out, v_out, ga_out, gb_out, qb_out, kn_out, kr_out, vb_out = refs
  xn = _rms(h_ref[...], gmix_ref[...]).astype(BF16)
  zm = _dot_nt(xn, wmla_ref[...])
  cqn = _rms(zm[:, :Q_LORA], gq_ref[...]).astype(BF16)
  ckvn = _rms(zm[:, Q_LORA:Q_LORA + KV_LORA], gkv_ref[...]).astype(BF16)
  qall = _dot(cqn, wuq_ref[...])
  kv = _dot(ckvn, wukv_ref[...])

  q_out[...] = (_dot_nt(xn, wq_ref[...]) * (A_HEAD_DIM ** -0.5 * LOG2E)).astype(BF16)
  k_out[...] = _dot_nt(xn, wk_ref[...]).astype(BF16)
  v_out[...] = _dot_nt(xn, wv_ref[...]).astype(BF16)
  ga_out[...] = _dot_nt(xn, wga_ref[...]).astype(BF16)
  gb_out[...] = _dot_nt(xn, wgb_ref[...]).astype(BF16)

  if make_tables:
    cos, sin = _rope_tile(pos_ref[0], invf_ref[...])
    cos_out[...] = cos
    sin_out[...] = sin
  else:
    cos = cos_ref[...]
    sin = sin_ref[...]
  first_half = lax.broadcasted_iota(jnp.int32, (h_ref.shape[0], LANES), 1) < B_ROPE // 2

  def rope(g):
    swapped = jnp.where(first_half, pltpu.roll(g, LANES - B_ROPE // 2, 1), pltpu.roll(g, B_ROPE // 2, 1))
    return g * cos + swapped * sin

  kr_out[...] = rope(zm[:, Q_LORA + KV_LORA:]).astype(BF16)
  for h in range(B_HEADS):
    qb = B_QW * h
    qb_out[:, qb:qb + B_NOPE] = (qall[:, qb:qb + B_NOPE] * B_QSCALE).astype(BF16)
    qb_out[:, qb + B_NOPE:qb + B_QW] = (rope(qall[:, qb + B_NOPE:qb + B_QW]) * B_QSCALE).astype(BF16)
    kn_out[:, B_NOPE * h:B_NOPE * (h + 1)] = kv[:, 2 * LANES * h:2 * LANES * h + B_NOPE].astype(BF16)
    vb_out[:, B_VDIM * h:B_VDIM * (h + 1)] = kv[:, 2 * LANES * h + B_NOPE:2 * LANES * (h + 1)].astype(BF16)


def _layer_spec(a, layer):
  return pl.BlockSpec((None,) + a.shape[1:], lambda i: (layer,) + (0,) * (a.ndim - 1),
                      pipeline_mode=pl.Buffered(1))


def _in_proj(h, w, layer, positions=None, tables=None):
  n = h.shape[0]
  t = TOKEN_TILE
  tok = lambda width: pl.BlockSpec((t, width), lambda i: (i, 0))
  w_in_block = lambda j: pl.BlockSpec((None, A_WIDTH, D_MODEL), lambda i: (layer, j, 0),
                                      pipeline_mode=pl.Buffered(1))
  small = [w["gmix"], w["wgb"], w["wmla"], w["gq"], w["gkv"], w["wuq"], w["wukv"]]
  in_specs = ([tok(D_MODEL), _layer_spec(small[0], layer)] + [w_in_block(j) for j in range(4)]
              + [_layer_spec(a, layer) for a in small[1:]])
  operands = [h, small[0]] + [w["w_t"]] * 4 + small[1:]
  out_widths = [(A_WIDTH, BF16), (A_WIDTH, BF16), (A_WIDTH, BF16), (A_WIDTH, BF16), (B_WIDTH, BF16),
                (B_HEADS * B_QW, BF16), (B_HEADS * B_NOPE, BF16), (LANES, BF16), (B_WIDTH, BF16)]
  make_tables = tables is None
  if make_tables:
    invf = _inv_freq_column()
    operands += [positions.reshape(n // t, 1, t), invf]
    in_specs += [pl.BlockSpec((1, 1, t), lambda i: (i, 0, 0)), pl.BlockSpec(invf.shape, lambda i: (0, 0))]
    out_widths += [(LANES, F32), (LANES, F32)]
  else:
    operands += list(tables)
    in_specs += [tok(LANES), tok(LANES)]
  return pl.pallas_call(
      functools.partial(_in_proj_kernel, make_tables=make_tables),
      grid=(n // t,),
      in_specs=in_specs,
      out_specs=[tok(wd) for wd, _ in out_widths],
      out_shape=[jax.ShapeDtypeStruct((n, wd), dt) for wd, dt in out_widths],
      compiler_params=pltpu.CompilerParams(dimension_semantics=("parallel",),
                                           vmem_limit_bytes=VMEM_LIMIT),
      name="in_proj",
  )(*operands)


def _attn_a_kernel(q_ref, k_ref, v_ref, gate_ref, tab_ref, o_ref):
  seq_len = q_ref.shape[1]
  lane = lax.broadcasted_iota(jnp.int32, (1, LANES), 1)
  head_rows = ((lane < A_HEAD_DIM).astype(BF16), (lane >= A_HEAD_DIM).astype(BF16))
  lo_mask = lax.broadcasted_iota(jnp.int32, (Q_BLOCK, LANES), 1) < A_HEAD_DIM
  ones = jnp.ones((A_WINDOW, LANES), BF16)

  def window(qb):
    k0 = max(0, Q_BLOCK * qb - A_LEFT)
    return k0, Q_BLOCK * (qb + 1) - k0

  half = Q_BLOCK // 2
  table_zero = ((half, A_LEFT - REL_MAX), (2 * half, A_LEFT - REL_MAX + half))

  def panels(qb):
    _, nk = window(qb)
    first_live = half if nk == A_WINDOW else 0
    return (slice(0, half), slice(0, nk - half)), (slice(half, Q_BLOCK), slice(first_live, nk))

  def scores(qb, pr, hh):
    k0, nk = window(qb)
    ls = slice(LANES * pr, LANES * (pr + 1))
    qp = q_ref[0, Q_BLOCK * qb:Q_BLOCK * (qb + 1), ls]
    s = _dot_nt(qp * head_rows[hh], k_ref[0, k0:k0 + nk, ls])
    col0 = A_WINDOW - nk
    out = []
    for (rs, cs), zero in zip(panels(qb), table_zero):
      cuts = sorted({col0 + cs.start, col0 + cs.stop,
                     *(min(max(z, col0 + cs.start), col0 + cs.stop) for z in zero)})
      pieces = []
      for c0, c1 in zip(cuts[:-1], cuts[1:]):
        piece = s[rs, c0 - col0:c1 - col0]
        if not zero[0] <= c0 < zero[1]:
          piece = piece + tab_ref[2 * pr + hh, rs, c0:c1]
        pieces.append(piece)
      out.append(jnp.concatenate(pieces, axis=1))
    return out

  def weighted(qb, pr, s_panels):
    k0, nk = window(qb)
    vp = jnp.concatenate([v_ref[0, k0:k0 + nk, LANES * pr:LANES * (pr + 1)], ones[:nk]], axis=1)
    rows = []
    for s, (_, cs) in zip(s_panels, panels(qb)):
      e = jnp.exp2(s - jnp.max(s, axis=-1, keepdims=True)).astype(BF16)
      dead = [jnp.zeros((half, w), BF16) for w in (cs.start, nk - cs.stop)]
      rows.append(jnp.concatenate([x for x in (dead[0], e, dead[1]) if x.shape[1]], axis=1))
    pv = _dot(jnp.concatenate(rows, axis=0), vp)
    return pv[:, :LANES] / pv[:, LANES:]

  items = [(qb, pr, hh) for qb in reversed(range(seq_len // Q_BLOCK))
           for pr in range(A_STEP_PAIRS) for hh in range(2)]
  depth = 2
  pending = [scores(*it) for it in items[:depth]]
  outs = []
  for i, (qb, pr, hh) in enumerate(items):
    if i + depth < len(items):
      pending.append(scores(*items[i + depth]))
    outs.append(weighted(qb, pr, pending.pop(0)))
    if hh == 1:
      rows = slice(Q_BLOCK * qb, Q_BLOCK * (qb + 1))
      ls = slice(LANES * pr, LANES * (pr + 1))
      o = jnp.where(lo_mask, outs[0], outs[1])
      o_ref[0, rows, ls] = (o * _silu(gate_ref[0, rows, ls].astype(F32))).astype(BF16)
      outs = []


def _attn_a(q, k, v, gate, tab, layer):
  b, s, _ = q.shape
  pair = pl.BlockSpec((1, s, A_STEP_PAIRS * LANES), lambda p, i: (i, 0, p))
  return pl.pallas_call(
      _attn_a_kernel,
      grid=(A_HEADS // (2 * A_STEP_PAIRS), b),
      in_specs=[pair, pair, pair, pair,
                pl.BlockSpec((None, 2 * A_STEP_PAIRS, Q_BLOCK, A_WINDOW),
                             lambda p, i: (layer, p, 0, 0))],
      out_specs=pair,
      out_shape=jax.ShapeDtypeStruct((b, s, A_WIDTH), BF16),
      compiler_params=pltpu.CompilerParams(dimension_semantics=("parallel", "parallel"),
                                           vmem_limit_bytes=VMEM_LIMIT),
      name="attn_a",
  )(q, k, v, gate, tab)


def _attn_b_kernel(q_ref, kn_ref, kr_ref, v_ref, gate_ref, o_ref):
  seq_len = q_ref.shape[1]
  ones = jnp.ones((seq_len, B_QW - B_VDIM), BF16)
  qi = lax.broadcasted_iota(jnp.int32, (Q_BLOCK, Q_BLOCK), 0)
  kj = lax.broadcasted_iota(jnp.int32, (Q_BLOCK, Q_BLOCK), 1)
  diag_ok = (kj // CHUNK) <= (qi // CHUNK)

  def scores(qb, h):
    rows = slice(Q_BLOCK * qb, Q_BLOCK * (qb + 1))
    wide = slice(B_QW * h, B_QW * (h + 1))
    qh = q_ref[0, rows, wide]
    nk = Q_BLOCK * (qb + 1)
    kh = jnp.concatenate([kn_ref[0, :nk, B_NOPE * h:B_NOPE * (h + 1)], kr_ref[0, :nk, :]], axis=1)
    s = _dot_nt(qh, kh)
    s_d = jnp.where(diag_ok, s[:, Q_BLOCK * qb:], -jnp.inf)
    return s_d, (s[:, :Q_BLOCK * qb] if qb else None)

  def finish(qb, h, s_d, s_p):
    rows = slice(Q_BLOCK * qb, Q_BLOCK * (qb + 1))
    narrow = slice(B_VDIM * h, B_VDIM * (h + 1))
    m = jnp.max(s_d, axis=-1, keepdims=True)
    if qb:
      m = jnp.maximum(m, jnp.max(s_p, axis=-1, keepdims=True))
    e = jnp.exp2(s_d - m).astype(BF16)
    if qb:
      e = jnp.concatenate([jnp.exp2(s_p - m).astype(BF16), e], axis=1)
    nk = Q_BLOCK * (qb + 1)
    pv = _dot(e, jnp.concatenate([v_ref[0, :nk, narrow], ones[:nk]], axis=1))
    o = pv[:, :B_VDIM] / pv[:, B_VDIM:]
    o_ref[0, rows, narrow] = (o * _silu(gate_ref[0, rows, narrow].astype(F32))).astype(BF16)

  items = [(qb, h) for qb in reversed(range(seq_len // Q_BLOCK)) for h in range(B_STEP_HEADS)]
  depth = 2
  pending = [scores(*it) for it in items[:depth]]
  for i, it in enumerate(items):
    if i + depth < len(items):
      pending.append(scores(*items[i + depth]))
    finish(*it, *pending.pop(0))


def _attn_b(q, kn, kr, v, gate):
  b, s, _ = q.shape
  wide = pl.BlockSpec((1, s, B_STEP_HEADS * B_QW), lambda i, h: (i, 0, h))
  narrow = pl.BlockSpec((1, s, B_STEP_HEADS * B_VDIM), lambda i, h: (i, 0, h))
  return pl.pallas_call(
      _attn_b_kernel,
      grid=(b, B_HEADS // B_STEP_HEADS),
      in_specs=[wide, narrow, pl.BlockSpec((1, s, LANES), lambda i, h: (i, 0, 0)), narrow, narrow],
      out_specs=narrow,
      out_shape=jax.ShapeDtypeStruct((b, s, B_WIDTH), BF16),
      compiler_params=pltpu.CompilerParams(dimension_semantics=("parallel", "parallel"),
                                           vmem_limit_bytes=VMEM_LIMIT),
      name="attn_b",
  )(q, kn, kr, v, gate)


def _out_ple_kernel(ma_ref, mb_ref, h_ref, p_ref, woa_ref, wob_ref, gple_ref, wpg_ref, bpg_ref,
                    wpe_ref, gfin_ref, o_ref, *, final_norm):
  h1 = h_ref[...] + (_dot(ma_ref[...], woa_ref[...]) + _dot(mb_ref[...], wob_ref[...]))
  pe = _dot(p_ref[...].astype(BF16), wpe_ref[...])
  xn = _rms(h1, gple_ref[...]).astype(BF16)
  z = _dot(xn, wpg_ref[...]) + bpg_ref[...]
  h2 = h1 + pe * (1.0 / (1.0 + jnp.exp(-z)))
  o_ref[...] = _rms(h2, gfin_ref[...]) if final_norm else h2


def _out_ple(ma, mb, h, p, w, layer, final_norm):
  n = h.shape[0]
  t = OUT_TOKEN_TILE
  tok = lambda width: pl.BlockSpec((t, width), lambda i: (i, 0))
  w_out_half = lambda j: pl.BlockSpec((None, A_WIDTH, D_MODEL), lambda i: (layer, j, 0))
  rest = [w["gple"], w["wpg"], w["bpg"], w["wpe"]]
  return pl.pallas_call(
      functools.partial(_out_ple_kernel, final_norm=final_norm),
      grid=(n // t,),
      in_specs=([tok(A_WIDTH), tok(B_WIDTH), tok(D_MODEL),
                 pl.BlockSpec((None, t, D_PLE), lambda i: (layer, i, 0)),
                 w_out_half(0), w_out_half(1)] + [_layer_spec(a, layer) for a in rest]
                + [pl.BlockSpec(w["gfin"].shape, lambda i: (0, 0))]),
      out_specs=tok(D_MODEL),
      out_shape=jax.ShapeDtypeStruct((n, D_MODEL), F32),
      compiler_params=pltpu.CompilerParams(dimension_semantics=("parallel",),
                                           vmem_limit_bytes=VMEM_LIMIT),
      name="out_ple",
  )(ma, mb, h, p, w["w_out"], w["w_out"], *rest, w["gfin"])


N_OUT_PLE_IN = 11
N_IN_PROJ_W = 11


def _seam_kernel(*refs):
  ple_in = refs[:N_OUT_PLE_IN]
  proj_w = refs[N_OUT_PLE_IN:N_OUT_PLE_IN + N_IN_PROJ_W]
  cos_ref, sin_ref, h_out = refs[N_OUT_PLE_IN + N_IN_PROJ_W:N_OUT_PLE_IN + N_IN_PROJ_W + 3]
  proj_out = refs[N_OUT_PLE_IN + N_IN_PROJ_W + 3:]
  _out_ple_kernel(*ple_in, h_out, final_norm=False)
  _in_proj_kernel(h_out, *proj_w, cos_ref, sin_ref, *proj_out, make_tables=False)


def _seam(ma, mb, h, p, w, layer, tables):
  n = h.shape[0]
  t = SEAM_TOKEN_TILE
  nxt = layer + 1
  tok = lambda width: pl.BlockSpec((t, width), lambda i: (i, 0))
  w_out_half = lambda j: pl.BlockSpec((None, A_WIDTH, D_MODEL), lambda i: (layer, j, 0),
                                      pipeline_mode=pl.Buffered(1))
  w_in_block = lambda j: pl.BlockSpec((None, A_WIDTH, D_MODEL), lambda i: (nxt, j, 0),
                                      pipeline_mode=pl.Buffered(1))
  ple_rest = [w["gple"], w["wpg"], w["bpg"], w["wpe"]]
  small = [w["gmix"], w["wgb"], w["wmla"], w["gq"], w["gkv"], w["wuq"], w["wukv"]]
  in_specs = ([tok(A_WIDTH), tok(B_WIDTH), tok(D_MODEL),
               pl.BlockSpec((None, t, D_PLE), lambda i: (layer, i, 0)),
               w_out_half(0), w_out_half(1)] + [_layer_spec(a, layer) for a in ple_rest]
              + [pl.BlockSpec(w["gfin"].shape, lambda i: (0, 0))]
              + [_layer_spec(small[0], nxt)] + [w_in_block(j) for j in range(4)]
              + [_layer_spec(a, nxt) for a in small[1:]] + [tok(LANES), tok(LANES)])
  operands = ([ma, mb, h, p, w["w_out"], w["w_out"], *ple_rest, w["gfin"], small[0]]
              + [w["w_t"]] * 4 + small[1:] + list(tables))
  out_widths = [(D_MODEL, F32),
                (A_WIDTH, BF16), (A_WIDTH, BF16), (A_WIDTH, BF16), (A_WIDTH, BF16), (B_WIDTH, BF16),
                (B_HEADS * B_QW, BF16), (B_HEADS * B_NOPE, BF16), (LANES, BF16), (B_WIDTH, BF16)]
  return pl.pallas_call(
      _seam_kernel,
      grid=(n // t,),
      in_specs=in_specs,
      out_specs=[tok(wd) for wd, _ in out_widths],
      out_shape=[jax.ShapeDtypeStruct((n, wd), dt) for wd, dt in out_widths],
      compiler_params=pltpu.CompilerParams(dimension_semantics=("parallel",),
                                           vmem_limit_bytes=SEAM_VMEM_LIMIT),
      name="seam",
  )(*operands)


def _prep_params(norm_mix, w_in, g_q, w_uq, g_kv, w_ukv, w_out, norm_ple, w_pe, w_pg, b_pg,
                 norm_final):
  depth = w_in.shape[0]
  w_t = jnp.swapaxes(w_in, 1, 2).astype(BF16)
  mla0 = 4 * A_WIDTH
  gb0 = mla0 + Q_LORA + KV_LORA + B_ROPE
  wmla = jnp.pad(w_t[:, mla0:gb0], ((0, 0), (0, LANES - B_ROPE), (0, 0)))
  uq =w_uq.reshape(depth, Q_LORA, B_HEADS, B_NOPE + B_ROPE)
  zpad = jnp.zeros((depth, Q_LORA, B_HEADS, LANES - B_ROPE), F32)
  wuq = jnp.concatenate([uq, zpad], axis=-1)
  return {
      "gmix": norm_mix[:, None, :],
      "w_t": w_t, "wgb": w_t[:, gb0:], "wmla": wmla,
      "gq": g_q[:, None, :], "gkv": g_kv[:, None, :],
      "wuq": wuq.reshape(depth, Q_LORA, B_HEADS * B_QW).astype(BF16),
      "wukv": w_ukv.astype(BF16),
      "w_out": w_out.astype(BF16),
      "gple": norm_ple[:, None, :], "wpg": w_pg.astype(BF16), "bpg": b_pg[:, None, :],
      "wpe": w_pe.astype(BF16),
      "gfin": norm_final[None, :],
  }


def kernel(x, p, positions, norm_mix, w_in, rel_bias, g_q, w_uq, g_kv, w_ukv, w_out,
           norm_ple, w_pe, w_pg, b_pg, norm_final):
  b, s, d = x.shape
  depth = w_in.shape[0]
  n = b * s
  w = _prep_params(norm_mix, w_in, g_q, w_uq, g_kv, w_ukv, w_out, norm_ple, w_pe, w_pg, b_pg,
                   norm_final)
  p = p.reshape(depth, n, D_PLE)
  h = x.reshape(n, d)
  seq = lambda a: a.reshape(b, s, a.shape[-1])
  tables = None
  tab = _bias_tables(rel_bias)
  *outs, cos, sin = _in_proj(h, w, 0, positions=positions)
  tables = (cos, sin)
  for i in range(depth):
    qa, ka, va, ga, gb, qb, kn, kr, vb = outs
    ma = _attn_a(seq(qa), seq(ka), seq(va), seq(ga), tab, i).reshape(n, A_WIDTH)
    mb = _attn_b(seq(qb), seq(kn), seq(kr), seq(vb), seq(gb)).reshape(n, B_WIDTH)
    if i == depth - 1:
      h = _out_ple(ma, mb, h, p, w, i, final_norm=True)
    else:
      h, *outs = _seam(ma, mb, h, p, w, i, tables)
  return h.reshape(b, s, d)
```
